```python
import jax, jax.numpy as jnp
from jax import lax
import numpy as np

D_MODEL = 2048
BATCH = 8
SEQ = 2048
DEPTH = 4
DEC_BATCH = 16
DEC_SEQ = 32
PAST_LEN = 4096

CHUNK = 64
Q_BLOCK = 128
ATT_HEADS = 8
ATT_HD = 128
KV_HEADS = 2
KV_GROUP = ATT_HEADS // KV_HEADS
IDX_HEADS = 8
IDX_DIM = 64
TOPK_MAX = 256
RET_HEADS = 8
RET_DK = 128
RET_DV = 256
ROPE_BASE = 10000.0
CONV_CH = 1024
CONV_W = 31
N_EXPERTS = 16
N_GROUPS = 4
EXP_PER_GROUP = N_EXPERTS // N_GROUPS
TOP_GROUPS = 1
GROUP_SCORE_K = 2
TOP_K = 2
D_EXPERT = 512
DN_ALPHA = (2 * DEPTH) ** 0.25
DN_BETA = (8 * DEPTH) ** -0.25
LN_EPS = 1e-5

IN_SIZES = (ATT_HEADS * ATT_HD, KV_HEADS * ATT_HD, KV_HEADS * ATT_HD,
            IDX_HEADS * IDX_DIM, IDX_DIM, IDX_HEADS,
            RET_HEADS * RET_DK, RET_HEADS * RET_DK,
            RET_HEADS * RET_DV, RET_HEADS * RET_DV,
            2 * CONV_CH,
            3 * D_MODEL)
IN_WIDTH = sum(IN_SIZES)

kernel_name = "hybrid_dsa_retention_conformer_moe_stream"

F32 = jnp.float32


def _split_points(sizes):
    pts, acc = [], 0
    for s in sizes[:-1]:
        acc += s
        pts.append(acc)
    return pts


def layer_norm(x, g, b):
    xf = x.astype(F32)
    mu = xf.mean(-1, keepdims=True)
    var = jnp.square(xf - mu).mean(-1, keepdims=True)
    y = (xf - mu) * lax.rsqrt(var + LN_EPS) * g.astype(F32) + b.astype(F32)
    return y.astype(x.dtype)


def head_group_norm(o, g):
    B, T, H, dv = o.shape
    of = o.astype(F32)
    mu = of.mean(-1, keepdims=True)
    var = jnp.square(of - mu).mean(-1, keepdims=True)
    y = ((of - mu) * lax.rsqrt(var + LN_EPS)).reshape(B, T, H * dv) * g.astype(F32)
    return y.astype(o.dtype)


def rotary(x, pos):
    half = x.shape[-1] // 2
    inv = 1.0 / (ROPE_BASE ** (jnp.arange(half, dtype=F32) / half))
    ang = pos.astype(F32)[:, None] * inv[None, :]
    cos = jnp.cos(ang)[None, :, None, :].astype(x.dtype)
    sin = jnp.sin(ang)[None, :, None, :].astype(x.dtype)
    x1, x2 = x[..., :half], x[..., half:]
    return jnp.concatenate([x1 * cos - x2 * sin, x2 * cos + x1 * sin], axis=-1)


def dsa_attention(q, k_all, v_all, q_idx, k_idx_all, w_idx, offset):
    B, T = q.shape[:2]
    L = k_all.shape[1]
    k_sel = min(TOPK_MAX, L // 4)
    qb = min(Q_BLOCK, T)
    nb = T // qb
    pos = offset + jnp.arange(T, dtype=jnp.int32)
    limit = jnp.minimum((pos // CHUNK + 1) * CHUNK, L)
    key_pos = jnp.arange(L, dtype=jnp.int32)
    gather = jax.vmap(lambda rows, idx: rows[idx])

    def blocks(a):
        return a.reshape((B, nb, qb) + a.shape[2:]).swapaxes(0, 1)

    def one_block(args):
        q_b, qi_b, w_b, lim = args
        dots = jnp.einsum('bthd,bsd->bths', qi_b, k_idx_all).astype(F32) * (IDX_DIM ** -0.5)
        score = jnp.einsum('bths,bth->bts', jax.nn.relu(dots), w_b.astype(F32)) * (IDX_HEADS ** -0.5)
        score = jnp.where((key_pos[None, :] < lim[:, None])[None], score, -jnp.inf)
        _, sel = lax.top_k(score, k_sel)
        valid = sel < lim[None, :, None]
        k_s = gather(k_all, sel)
        v_s = gather(v_all, sel)
        q_g = q_b.reshape(B, qb, KV_HEADS, KV_GROUP, ATT_HD)
        logits = jnp.einsum('btgrd,btkgd->btgrk', q_g, k_s).astype(F32) * (ATT_HD ** -0.5)
        logits = jnp.where(valid[:, :, None, None, :], logits, -jnp.inf)
        p = jax.nn.softmax(logits, axis=-1).astype(v_s.dtype)
        o = jnp.einsum('btgrk,btkgd->btgrd', p, v_s)
        return o.reshape(B, qb, ATT_HEADS * ATT_HD)

    out = lax.map(one_block, (blocks(q), blocks(q_idx), blocks(w_idx), limit.reshape(nb, qb)))
    return out.swapaxes(0, 1).reshape(B, T, ATT_HEADS * ATT_HD)


def retention(q, k, v, s0):
    B, T, H, _ = q.shape
    dv = v.shape[-1]
    cl = min(CHUNK, T)
    nc = T // cl
    dt = q.dtype
    log_g = jnp.log1p(-jnp.exp2(-5.0 - jnp.arange(RET_HEADS, dtype=F32)))
    n = jnp.arange(cl, dtype=F32)
    diff = n[:, None] - n[None, :]
    decay = jnp.where(diff >= 0, jnp.exp(log_g[:, None, None] * jnp.maximum(diff, 0.0)), 0.0).astype(dt)
    xi = jnp.exp(log_g[:, None] * (n + 1.0)).astype(dt)[..., None]
    zeta = jnp.exp(log_g[:, None] * (cl - 1.0 - n)).astype(dt)[..., None]
    g_chunk = jnp.exp(log_g * cl).astype(dt)[:, None, None]

    def step(s, xs):
        qc, kc, vc = xs
        inner = jnp.einsum('bhnd,bhmd->bhnm', qc, kc) * decay
        o = jnp.einsum('bhnm,bhmv->bhnv', inner, vc) + jnp.einsum('bhnd,bhdv->bhnv', qc, s) * xi
        s = s * g_chunk + jnp.einsum('bhmd,bhmv->bhdv', kc * zeta, vc)
        return s, o

    def chunks(a):
        return a.reshape(B, nc, cl, H, a.shape[-1]).transpose(1, 0, 3, 2, 4)

    s_fin, o = lax.scan(step, s0.astype(dt), (chunks(q), chunks(k), chunks(v)))
    o = o.transpose(1, 0, 3, 2, 4).reshape(B, T, H, dv)
    return o, s_fin


def depthwise_causal_conv(u_pad, w, b):
    out = lax.conv_general_dilated(u_pad, w[:, None, :].astype(u_pad.dtype), window_strides=(1,), padding='VALID',
                                   dimension_numbers=('NWC', 'WIO', 'NWC'), feature_group_count=CONV_CH)
    return out + b


def token_mixer(x, offset, past_k, past_v, past_ik, ret_s, conv_s,
                w_in, b_gate, w_att_o, ret_gn_g, w_ret_o, conv_dw, conv_dw_b, conv_ln_g, conv_ln_b, w_conv_o, w_out):
    B, T, _ = x.shape
    proj = jnp.einsum('btd,de->bte', x, w_in)
    (q_a, k_a, v_a, q_i, k_i, w_i, q_r, k_r, v_r, g_r, c_in, gate_in) = jnp.split(proj, _split_points(IN_SIZES), axis=-1)
    q_a = q_a.reshape(B, T, ATT_HEADS, ATT_HD)
    k_a = k_a.reshape(B, T, KV_HEADS, ATT_HD)
    v_a = v_a.reshape(B, T, KV_HEADS, ATT_HD)
    q_i = q_i.reshape(B, T, IDX_HEADS, IDX_DIM)
    k_all = jnp.concatenate([past_k.astype(k_a.dtype), k_a], axis=1)
    v_all = jnp.concatenate([past_v.astype(v_a.dtype), v_a], axis=1)
    ik_all = jnp.concatenate([past_ik.astype(k_i.dtype), k_i], axis=1)
    y_a = dsa_attention(q_a, k_all, v_all, q_i, ik_all, w_i, offset) @ w_att_o
    pos = offset + jnp.arange(T, dtype=jnp.int32)
    q_r = rotary(q_r.reshape(B, T, RET_HEADS, RET_DK), pos)
    k_r = rotary(k_r.reshape(B, T, RET_HEADS, RET_DK), pos) * (RET_DK ** -0.5)
    o_r, ret_new = retention(q_r, k_r, v_r.reshape(B, T, RET_HEADS, RET_DV), ret_s)
    y_b = (jax.nn.silu(g_r) * head_group_norm(o_r, ret_gn_g)) @ w_ret_o
    c_a, c_b = jnp.split(c_in, 2, axis=-1)
    u = c_a * jax.nn.sigmoid(c_b)
    u_pad = jnp.concatenate([conv_s.astype(u.dtype), u], axis=1)
    conv_new = u_pad[:, -(CONV_W - 1):]
    h = depthwise_causal_conv(u_pad, conv_dw, conv_dw_b)
    y_c = jax.nn.silu(layer_norm(h, conv_ln_g, conv_ln_b)) @ w_conv_o
    g_a, g_b, g_c = jnp.split(jax.nn.sigmoid(gate_in + b_gate), 3, axis=-1)
    out = (g_a * y_a + g_b * y_b + g_c * y_c) @ w_out
    return out, (k_a, v_a, k_i, ret_new, conv_new)


def moe_ffn(x, w_router, b_router, w1, w3, w2):
    B, T, D = x.shape
    xt = x.reshape(B * T, D)
    aff = jax.nn.softmax((xt @ w_router + b_router).astype(F32), axis=-1)
    grp_score = lax.top_k(aff.reshape(-1, N_GROUPS, EXP_PER_GROUP), GROUP_SCORE_K)[0].sum(-1)
    _, gsel = lax.top_k(grp_score, TOP_GROUPS)
    expert_group = jnp.arange(N_EXPERTS) // EXP_PER_GROUP
    in_grp = (expert_group[None, None, :] == gsel[:, :, None]).any(axis=1)
    top_v, top_i = lax.top_k(jnp.where(in_grp, aff, -1.0), TOP_K)
    top_w = top_v / top_v.sum(-1, keepdims=True)
    gate = jnp.sum(jax.nn.one_hot(top_i, N_EXPERTS, dtype=F32) * top_w[..., None], axis=1).astype(x.dtype)
    y = jnp.zeros_like(xt)
    for e in range(N_EXPERTS):
        h = jax.nn.silu(xt @ w1[e]) * (xt @ w3[e])
        y = y + gate[:, e:e + 1] * (h @ w2[e])
    return y.reshape(B, T, D)


def trunk(x, offset, past_k, past_v, past_ik, ret_s, conv_s, params):
    (w_in, b_gate, w_att_o, ret_gn_g, w_ret_o, conv_dw, conv_dw_b, conv_ln_g, conv_ln_b, w_conv_o, w_out,
     ln1_g, ln1_b, w_router, b_router, moe_w1, moe_w3, moe_w2, ln2_g, ln2_b) = params
    ks, vs, iks, rs, cs = [], [], [], [], []
    for l in range(DEPTH):
        mix, (k_l, v_l, ik_l, r_l, c_l) = token_mixer(
            x, offset, past_k[l], past_v[l], past_ik[l], ret_s[l], conv_s[l],
            w_in[l], b_gate[l], w_att_o[l], ret_gn_g[l], w_ret_o[l], conv_dw[l], conv_dw_b[l],
            conv_ln_g[l], conv_ln_b[l], w_conv_o[l], w_out[l])
        x = layer_norm(DN_ALPHA * x + mix, ln1_g[l], ln1_b[l])
        x = layer_norm(DN_ALPHA * x + moe_ffn(x, w_router, b_router, moe_w1[l], moe_w3[l], moe_w2[l]), ln2_g[l], ln2_b[l])
        ks.append(k_l); vs.append(v_l); iks.append(ik_l); rs.append(r_l); cs.append(c_l)
    return x, jnp.stack(ks), jnp.stack(vs), jnp.stack(iks), jnp.stack(rs), jnp.stack(cs)


def setup_inputs(seed: int = 0) -> dict:
    key = jax.random.key(seed)
    k = jax.random.split(key, 32)

    def nrm(kk, shape, scale):
        return jax.random.normal(kk, shape, F32) * scale

    D = D_MODEL
    return {
        "x_prompt": nrm(k[0], (BATCH, SEQ, D), 1.0),
        "x_sample": nrm(k[1], (DEC_BATCH, DEC_SEQ, D), 1.0),
        "cache_k": nrm(k[2], (DEPTH, DEC_BATCH, PAST_LEN, KV_HEADS, ATT_HD), 1.0),
        "cache_v": nrm(k[3], (DEPTH, DEC_BATCH, PAST_LEN, KV_HEADS, ATT_HD), 1.0),
        "cache_idx_k": nrm(k[4], (DEPTH, DEC_BATCH, PAST_LEN, IDX_DIM), 1.0),
        "state_ret": nrm(k[5], (DEPTH, DEC_BATCH, RET_HEADS, RET_DK, RET_DV), 1.0),
        "state_conv": nrm(k[6], (DEPTH, DEC_BATCH, CONV_W - 1, CONV_CH), 0.5),
        "w_in": nrm(k[7], (DEPTH, D, IN_WIDTH), D ** -0.5),
        "b_gate": nrm(k[8], (DEPTH, 3 * D), 0.02),
        "w_att_o": nrm(k[9], (DEPTH, ATT_HEADS * ATT_HD, D), (ATT_HEADS * ATT_HD) ** -0.5),
        "ret_gn_g": 1.0 + nrm(k[10], (DEPTH, RET_HEADS * RET_DV), 0.02),
        "w_ret_o": nrm(k[11], (DEPTH, RET_HEADS * RET_DV, D), (RET_HEADS * RET_DV) ** -0.5),
        "conv_dw": nrm(k[12], (DEPTH, CONV_W, CONV_CH), CONV_W ** -0.5),
        "conv_dw_b": nrm(k[13], (DEPTH, CONV_CH), 0.02),
        "conv_ln_g": 1.0 + nrm(k[14], (DEPTH, CONV_CH), 0.02),
        "conv_ln_b": nrm(k[15], (DEPTH, CONV_CH), 0.02),
        "w_conv_o": nrm(k[16], (DEPTH, CONV_CH, D), CONV_CH ** -0.5),
        "w_out": nrm(k[17], (DEPTH, D, D), D ** -0.5 * DN_BETA),
        "ln1_g": 1.0 + nrm(k[18], (DEPTH, D), 0.02),
        "ln1_b": nrm(k[19], (DEPTH, D), 0.02),
        "w_router": nrm(k[20], (D, N_EXPERTS), D ** -0.5),
        "b_router": nrm(k[21], (N_EXPERTS,), 0.01),
        "moe_w1": nrm(k[22], (DEPTH, N_EXPERTS, D, D_EXPERT), D ** -0.5),
        "moe_w3": nrm(k[23], (DEPTH, N_EXPERTS, D, D_EXPERT), D ** -0.5),
        "moe_w2": nrm(k[24], (DEPTH, N_EXPERTS, D_EXPERT, D), D_EXPERT ** -0.5 * DN_BETA),
        "ln2_g": 1.0 + nrm(k[25], (DEPTH, D), 0.02),
        "ln2_b": nrm(k[26], (DEPTH, D), 0.02),
    }


def reference(x_prompt, x_sample, cache_k, cache_v, cache_idx_k, state_ret, state_conv,
              w_in, b_gate, w_att_o, ret_gn_g, w_ret_o, conv_dw, conv_dw_b, conv_ln_g, conv_ln_b, w_conv_o, w_out,
              ln1_g, ln1_b, w_router, b_router, moe_w1, moe_w3, moe_w2, ln2_g, ln2_b):
    params = (w_in, b_gate, w_att_o, ret_gn_g, w_ret_o, conv_dw, conv_dw_b, conv_ln_g, conv_ln_b, w_conv_o, w_out,
              ln1_g, ln1_b, w_router, b_router, moe_w1, moe_w3, moe_w2, ln2_g, ln2_b)
    B = x_prompt.shape[0]
    dt = x_prompt.dtype
    empty_k = jnp.zeros((DEPTH, B, 0, KV_HEADS, ATT_HD), dt)
    empty_ik = jnp.zeros((DEPTH, B, 0, IDX_DIM), dt)
    zero_ret = jnp.zeros((DEPTH, B, RET_HEADS, RET_DK, RET_DV), dt)
    zero_conv = jnp.zeros((DEPTH, B, CONV_W - 1, CONV_CH), dt)
    y_prompt, k_p, v_p, ik_p, ret_p, conv_p = trunk(x_prompt, 0, empty_k, empty_k, empty_ik, zero_ret, zero_conv, params)
    y_sample, k_s, v_s, ik_s, ret_s, conv_s = trunk(x_sample, PAST_LEN, cache_k, cache_v, cache_idx_k, state_ret, state_conv, params)
    return (y_prompt, y_sample, k_p, v_p, ik_p, ret_p, conv_p, k_s, v_s, ik_s, ret_s, conv_s)
```

```python
import functools

import jax
import jax.numpy as jnp
from jax import lax
from jax.experimental import pallas as pl
from jax.experimental.pallas import tpu as pltpu

F32 = jnp.float32
BF16 = jnp.bfloat16
I32 = jnp.int32

CHUNK = 64
TOPK_MAX = 256
ATT_HEADS, ATT_HD, KV_HEADS = 8, 128, 2
KV_GROUP = ATT_HEADS // KV_HEADS
IDX_HEADS, IDX_DIM = 8, 64
RET_HEADS, RET_DK, RET_DV = 8, 128, 256
ROPE_BASE = 10000.0
CONV_CH, CONV_W = 1024, 31
N_EXPERTS, N_GROUPS = 16, 4
EXP_PER_GROUP = N_EXPERTS // N_GROUPS
LN_EPS = 1e-5

LANES = 128
SUBLANES = 8

INT_MIN = -2 ** 31
NEG_BIG = -1e30

QA_OFF = 0
QR_OFF = QA_OFF + ATT_HEADS * ATT_HD
KR_OFF = QR_OFF + RET_HEADS * RET_DK
VR_OFF = KR_OFF + RET_HEADS * RET_DK
GR_OFF = VR_OFF + RET_HEADS * RET_DV
CIN_OFF = GR_OFF + RET_HEADS * RET_DV
GATE_OFF = CIN_OFF + 2 * CONV_CH
KA_OFF = 0
VA_OFF = KA_OFF + KV_HEADS * ATT_HD
QI_OFF = VA_OFF + KV_HEADS * ATT_HD
KI_OFF = QI_OFF + IDX_HEADS * IDX_DIM
WI_OFF = KI_OFF + IDX_DIM
F32_USED = WI_OFF + IDX_HEADS
F32_WIDTH = -(-F32_USED // LANES) * LANES

VMEM_LIMIT = 56 * 1024 * 1024


def _cp(*sem):
    return pltpu.CompilerParams(dimension_semantics=sem, vmem_limit_bytes=VMEM_LIMIT)


def _pow2_tile(n, pref):
    t = pref
    while n % t:
        t //= 2
    return t


def _layer_norm(x, g, b):
    mu = jnp.mean(x, axis=-1, keepdims=True)
    xc = x - mu
    var = jnp.mean(xc * xc, axis=-1, keepdims=True)
    return xc * lax.rsqrt(var + LN_EPS) * g + b


def _mm_kernel(x_ref, w_ref, o_ref):
    o_ref[...] = jnp.dot(x_ref[...], w_ref[...], preferred_element_type=F32).astype(o_ref.dtype)


def _matmul(x, w, out_dtype, tm, tn):
    m, k = x.shape
    n = w.shape[1]
    return pl.pallas_call(
        _mm_kernel,
        grid=(n // tn, m // tm),
        in_specs=[pl.BlockSpec((tm, k), lambda j, i: (i, 0)),
                  pl.BlockSpec((k, tn), lambda j, i: (0, j))],
        out_specs=pl.BlockSpec((tm, tn), lambda j, i: (i, j)),
        out_shape=jax.ShapeDtypeStruct((m, n), out_dtype),
        compiler_params=_cp("arbitrary", "arbitrary"),
        name="in_proj",
    )(x, w)


def _attn_kernel(q_ref, k_ref, v_ref, qi_ref, ki_ref, wi_ref, o_ref,
                 key_ref, bias_ref, lg_ref, mx_ref, den_ref, acc_ref, m_ref,
                 *, tq, ck, nck, offset, l_true, k_sel, tie_bits):
    j = pl.program_id(1)
    nsub = ck // LANES
    row = lax.broadcasted_iota(I32, (tq, LANES), 0)
    lane = lax.broadcasted_iota(I32, (tq, LANES), 1)
    pos = offset + j * tq + row
    limit = jnp.minimum((pos // CHUNK + 1) * CHUNK, l_true)
    last_pos = offset + (j + 1) * tq - 1
    max_limit = jnp.minimum((last_pos // CHUNK + 1) * CHUNK, l_true)
    n_act = (max_limit + ck - 1) // ck

    qi = qi_ref[...] * (IDX_DIM ** -0.5)
    qis = [qi[:, h * IDX_DIM:(h + 1) * IDX_DIM] for h in range(IDX_HEADS)]
    wi = wi_ref[:, IDX_DIM:IDX_DIM + IDX_HEADS] * (IDX_HEADS ** -0.5)
    wis = [wi[:, h:h + 1] for h in range(IDX_HEADS)]

    def score_body(c, carry):
        kic = ki_ref[pl.ds(pl.multiple_of(c * ck, ck), ck), :][:, :IDX_DIM]
        for u in range(nsub):
            kiu = kic[u * LANES:(u + 1) * LANES]
            s = jnp.zeros((tq, LANES), F32)
            for h in range(IDX_HEADS):
                d = lax.dot_general(qis[h], kiu, (((1,), (1,)), ((), ())), preferred_element_type=F32)
                s = s + jnp.maximum(d, 0.0) * wis[h]
            bits = lax.bitcast_convert_type(s, I32)
            bits = jnp.where(bits == INT_MIN, 0, bits)
            key = jnp.where(bits >= 0, bits, bits ^ 0x7FFFFFFF)
            kpos = c * ck + u * LANES + lane
            key_ref[c, :, u * LANES:(u + 1) * LANES] = jnp.where(kpos < limit, key, INT_MIN)
        return carry

    lax.fori_loop(0, n_act, score_body, 0)

    ones = jnp.ones((LANES, LANES), BF16)

    def count(pred):
        def body(c, acc):
            for u in range(nsub):
                p = pred(key_ref[c, :, u * LANES:(u + 1) * LANES], c * ck + u * LANES + lane)
                acc = acc + jnp.where(p, 1.0, 0.0)
            return acc
        acc = lax.fori_loop(0, n_act, body, jnp.zeros((tq, LANES), F32))
        return jnp.dot(acc.astype(BF16), ones, preferred_element_type=F32)

    def bisect(i, tu):
        cand_u = tu | (jnp.int32(1) << (31 - i))
        cand_s = cand_u ^ INT_MIN
        cnt = count(lambda key, idx: key >= cand_s)
        return jnp.where(cnt >= k_sel, cand_u, tu)

    tu = lax.fori_loop(0, 32, bisect, jnp.zeros((tq, LANES), I32))
    ts = tu ^ INT_MIN
    c_gt = count(lambda key, idx: key > ts)
    c_ge = count(lambda key, idx: key >= ts)
    want = k_sel - c_gt

    m_ref[...] = jnp.full((tq, LANES), 2 ** 30, I32)

    @pl.when(jnp.max(c_ge) > k_sel)
    def _():
        def tie_bisect(i, m):
            cand = m | (jnp.int32(1) << (tie_bits - 1 - i))
            cnt = count(lambda key, idx: (key == ts) & (idx < cand))
            return jnp.where(cnt < want, cand, m)
        m_ref[...] = lax.fori_loop(0, tie_bits, tie_bisect, jnp.zeros((tq, LANES), I32))

    m_last = m_ref[...]

    def bias_body(c, carry):
        for u in range(nsub):
            key = key_ref[c, :, u * LANES:(u + 1) * LANES]
            idx = c * ck + u * LANES + lane
            sel = ((key > ts) | ((key == ts) & (idx <= m_last))) & (key > INT_MIN)
            bias_ref[c, :, u * LANES:(u + 1) * LANES] = jnp.where(sel, 0.0, NEG_BIG)
        return carry

    lax.fori_loop(0, n_act, bias_body, 0)

    scale = ATT_HD ** -0.5
    for g in range(KV_HEADS):
        qs = jnp.concatenate(
            [q_ref[:, (g * KV_GROUP + r) * ATT_HD:(g * KV_GROUP + r + 1) * ATT_HD] for r in range(KV_GROUP)], axis=0)
        mx_ref[...] = jnp.full(mx_ref.shape, NEG_BIG, F32)

        def logit_body(c, carry):
            kc = k_ref[pl.ds(pl.multiple_of(c * ck, ck), ck), g * ATT_HD:(g + 1) * ATT_HD].astype(BF16)
            lg = lax.dot_general(qs, kc, (((1,), (1,)), ((), ())), preferred_element_type=F32) * scale
            b = bias_ref[c]
            lg = lg + jnp.concatenate([b] * KV_GROUP, axis=0)
            lg_ref[c] = lg
            m = mx_ref[...]
            for u in range(nsub):
                m = jnp.maximum(m, lg[:, u * LANES:(u + 1) * LANES])
            mx_ref[...] = m
            return carry

        lax.fori_loop(0, n_act, logit_body, 0)
        m_row = jnp.max(mx_ref[...], axis=-1, keepdims=True)
        den_ref[...] = jnp.zeros(den_ref.shape, F32)
        acc_ref[...] = jnp.zeros(acc_ref.shape, F32)

        def pv_body(c, carry):
            p = jnp.exp(lg_ref[c] - m_row)
            d = den_ref[...]
            for u in range(nsub):
                d = d + p[:, u * LANES:(u + 1) * LANES]
            den_ref[...] = d
            vc = v_ref[pl.ds(pl.multiple_of(c * ck, ck), ck), g * ATT_HD:(g + 1) * ATT_HD].astype(BF16)
            acc_ref[...] += jnp.dot(p.astype(BF16), vc, preferred_element_type=F32)
            return carry

        lax.fori_loop(0, n_act, pv_body, 0)
        o = acc_ref[...] / jnp.sum(den_ref[...], axis=-1, keepdims=True)
        for r in range(KV_GROUP):
            h = g * KV_GROUP + r
            o_ref[:, h * ATT_HD:(h + 1) * ATT_HD] = o[r * tq:(r + 1) * tq].astype(o_ref.dtype)


def _attention(q_arr, q_map, k_arr, k_map, v_arr, v_map, qi_arr, qi_map, ki_arr, ki_map, wi_arr, wi_map,
               *, nb, t, l_pad, l_true, offset, tq):
    ck = 2 * LANES
    nck = l_pad // ck
    assert l_pad % ck == 0 and t % tq == 0 and nck * (ck // LANES) < 256
    nq = t // tq
    k_sel = min(TOPK_MAX, l_true // 4)
    tie_bits = max(1, (l_pad - 1).bit_length())
    kern = functools.partial(_attn_kernel, tq=tq, ck=ck, nck=nck, offset=offset, l_true=l_true,
                             k_sel=k_sel, tie_bits=tie_bits)
    hq = ATT_HEADS * ATT_HD
    return pl.pallas_call(
        kern,
        grid=(nb, nq),
        in_specs=[pl.BlockSpec(*q_map(tq)), pl.BlockSpec(*k_map(l_pad)), pl.BlockSpec(*v_map(l_pad)),
                  pl.BlockSpec(*qi_map(tq)), pl.BlockSpec(*ki_map(l_pad)), pl.BlockSpec(*wi_map(tq))],
        out_specs=pl.BlockSpec((tq, hq), lambda b, j: (b * nq + j, 0)),
        out_shape=jax.ShapeDtypeStruct((nb * t, hq), BF16),
        scratch_shapes=[pltpu.VMEM((nck, tq, ck), I32),
                        pltpu.VMEM((nck, tq, ck), F32),
                        pltpu.VMEM((nck, KV_GROUP * tq, ck), F32),
                        pltpu.VMEM((KV_GROUP * tq, LANES), F32),
                        pltpu.VMEM((KV_GROUP * tq, LANES), F32),
                        pltpu.VMEM((KV_GROUP * tq, ATT_HD), F32),
                        pltpu.VMEM((tq, LANES), I32)],
        compiler_params=_cp("arbitrary", "arbitrary"),
        name="dsa_attention",
    )(q_arr, k_arr, v_arr, qi_arr, ki_arr, wi_arr)


def _ret_kernel(q_ref, k_ref, v_ref, g_ref, cs_ref, sn_ref, dec_ref, xi_ref, zt_ref, gp_ref, gn_ref, s0_ref,
                y_ref, sf_ref, s_ref):
    c = pl.program_id(2)

    @pl.when(c == 0)
    def _():
        s_ref[...] = s0_ref[...]

    cs = cs_ref[...]
    sn = sn_ref[...]

    def rot(x):
        xf = x.astype(F32)
        return xf * cs + pltpu.roll(xf, RET_DK // 2, 1) * sn

    q = rot(q_ref[...])
    k = rot(k_ref[...]) * (RET_DK ** -0.5)
    qb = q.astype(BF16)
    kb = k.astype(BF16)
    v = v_ref[...]
    inner = lax.dot_general(qb, kb, (((1,), (1,)), ((), ())), preferred_element_type=F32) * dec_ref[...]
    s = s_ref[...]
    o = jnp.dot(inner.astype(BF16), v, preferred_element_type=F32)
    o = o + jnp.dot(qb, s.astype(BF16), preferred_element_type=F32) * xi_ref[...]
    kz = (k * zt_ref[...]).T.astype(BF16)
    s_new = s * gp_ref[...] + jnp.dot(kz, v, preferred_element_type=F32)
    s_ref[...] = s_new
    sf_ref[...] = s_new

    mu = jnp.mean(o, axis=-1, keepdims=True)
    oc = o - mu
    var = jnp.mean(oc * oc, axis=-1, keepdims=True)
    yn = oc * lax.rsqrt(var + LN_EPS) * gn_ref[...]
    gg = g_ref[...].astype(F32)
    y_ref[...] = (gg * jax.nn.sigmoid(gg) * yn).astype(y_ref.dtype)


def _retention(p_bf, row0, nb, t, cl, offset, s0, gn_g):
    nc = t // cl
    h_, dk, dv = RET_HEADS, RET_DK, RET_DV
    half = dk // 2
    pos = (offset + jnp.arange(t, dtype=I32)).astype(F32)
    inv = 1.0 / (ROPE_BASE ** (jnp.arange(half, dtype=F32) / half))
    ang = pos[:, None] * inv[None, :]
    cos, sin = jnp.cos(ang), jnp.sin(ang)
    cs2 = jnp.concatenate([cos, cos], axis=-1)
    sn2 = jnp.concatenate([-sin, sin], axis=-1)
    log_g = jnp.log1p(-jnp.exp2(-5.0 - jnp.arange(h_, dtype=F32)))
    n = jnp.arange(cl, dtype=F32)
    diff = n[:, None] - n[None, :]
    decay = jnp.where(diff >= 0, jnp.exp(log_g[:, None, None] * jnp.maximum(diff, 0.0)), 0.0)
    xi = jnp.exp(log_g[:, None] * (n + 1.0))[..., None]
    zeta = jnp.exp(log_g[:, None] * (cl - 1.0 - n))[..., None]
    gpow = jnp.broadcast_to(jnp.exp(log_g * cl)[:, None, None], (h_, 1, dv))
    rb = row0 // cl
    return pl.pallas_call(
        _ret_kernel,
        grid=(nb, h_, nc),
        in_specs=[pl.BlockSpec((cl, dk), lambda b, h, c: (rb + b * nc + c, QR_OFF // dk + h)),
                  pl.BlockSpec((cl, dk), lambda b, h, c: (rb + b * nc + c, KR_OFF // dk + h)),
                  pl.BlockSpec((cl, dv), lambda b, h, c: (rb + b * nc + c, VR_OFF // dv + h)),
                  pl.BlockSpec((cl, dv), lambda b, h, c: (rb + b * nc + c, GR_OFF // dv + h)),
                  pl.BlockSpec((cl, dk), lambda b, h, c: (c, 0)),
                  pl.BlockSpec((cl, dk), lambda b, h, c: (c, 0)),
                  pl.BlockSpec((None, cl, cl), lambda b, h, c: (h, 0, 0)),
                  pl.BlockSpec((None, cl, 1), lambda b, h, c: (h, 0, 0)),
                  pl.BlockSpec((None, cl, 1), lambda b, h, c: (h, 0, 0)),
                  pl.BlockSpec((None, 1, dv), lambda b, h, c: (h, 0, 0)),
                  pl.BlockSpec((1, dv), lambda b, h, c: (0, h)),
                  pl.BlockSpec((None, None, dk, dv), lambda b, h, c: (b, h, 0, 0))],
        out_specs=[pl.BlockSpec((cl, dv), lambda b, h, c: (b * nc + c, h)),
                   pl.BlockSpec((None, None, dk, dv), lambda b, h, c: (b, h, 0, 0))],
        out_shape=[jax.ShapeDtypeStruct((nb * t, h_ * dv), BF16),
                   jax.ShapeDtypeStruct((nb, h_, dk, dv), F32)],
        scratch_shapes=[pltpu.VMEM((dk, dv), F32)],
        compiler_params=_cp("arbitrary", "arbitrary", "arbitrary"),
        name="retention",
    )(p_bf, p_bf, p_bf, p_bf, cs2, sn2, decay, xi, zeta, gpow, gn_g.reshape(1, -1), s0)


CONV_HALO = 32
CONV_ROWS = 16


def _conv_kernel(ca_ref, cb_ref, st_ref, w_ref, b_ref, lg_ref, lb_ref, y_ref, tail_ref, sh_ref, *, tt):
    t = pl.program_id(1)
    ext = tt + CONV_HALO

    @pl.when(t == 0)
    def _():
        sh_ref[0, 0:CONV_HALO] = st_ref[...]

    @pl.when(t > 0)
    def _():
        sh_ref[0, 0:CONV_HALO] = sh_ref[0, tt:ext]

    ca = ca_ref[...].astype(F32)
    cb = cb_ref[...].astype(F32)
    sh_ref[0, CONV_HALO:ext] = ca * jax.nn.sigmoid(cb)
    tail_ref[...] = sh_ref[0, tt:ext]
    for s in range(1, SUBLANES):
        sh_ref[s, 0:ext - SUBLANES] = sh_ref[0, s:ext - SUBLANES + s]

    lead = CONV_HALO - (CONV_W - 1)

    def body(i, carry):
        r0 = pl.multiple_of(i * CONV_ROWS, CONV_ROWS)
        acc = jnp.broadcast_to(b_ref[...], (CONV_ROWS, CONV_CH))
        for k in range(CONV_W):
            a, s = divmod(k + lead, SUBLANES)
            acc = acc + w_ref[k:k + 1, :] * sh_ref[s, pl.ds(r0 + a * SUBLANES, CONV_ROWS), :]
        hn = _layer_norm(acc, lg_ref[...], lb_ref[...])
        y_ref[pl.ds(r0, CONV_ROWS), :] = (hn * jax.nn.sigmoid(hn)).astype(y_ref.dtype)
        return carry

    lax.fori_loop(0, tt // CONV_ROWS, body, 0)


def _conv_module(p_bf, row0, nb, t, tt, state, dw, dw_b, ln_g, ln_b):
    nt = t // tt
    lead = CONV_HALO - (CONV_W - 1)
    st = jnp.pad(state, ((0, 0), (lead, 0), (0, 0)))
    rb = row0 // tt
    kern = functools.partial(_conv_kernel, tt=tt)
    y, tail = pl.pallas_call(
        kern,
        grid=(nb, nt),
        in_specs=[pl.BlockSpec((tt, CONV_CH), lambda b, i: (rb + b * nt + i, CIN_OFF // CONV_CH)),
                  pl.BlockSpec((tt, CONV_CH), lambda b, i: (rb + b * nt + i, CIN_OFF // CONV_CH + 1)),
                  pl.BlockSpec((None, CONV_HALO, CONV_CH), lambda b, i: (b, 0, 0)),
                  pl.BlockSpec((CONV_W, CONV_CH), lambda b, i: (0, 0)),
                  pl.BlockSpec((1, CONV_CH), lambda b, i: (0, 0)),
                  pl.BlockSpec((1, CONV_CH), lambda b, i: (0, 0)),
                  pl.BlockSpec((1, CONV_CH), lambda b, i: (0, 0))],
        out_specs=[pl.BlockSpec((tt, CONV_CH), lambda b, i: (b * nt + i, 0)),
                   pl.BlockSpec((None, CONV_HALO, CONV_CH), lambda b, i: (b, 0, 0))],
        out_shape=[jax.ShapeDtypeStruct((nb * t, CONV_CH), BF16),
                   jax.ShapeDtypeStruct((nb, CONV_HALO, CONV_CH), F32)],
        scratch_shapes=[pltpu.VMEM((SUBLANES, tt + CONV_HALO, CONV_CH), F32)],
        compiler_params=_cp("arbitrary", "arbitrary"),
        name="conv_module",
    )(p_bf, p_bf, st, dw, dw_b.reshape(1, -1), ln_g.reshape(1, -1), ln_b.reshape(1, -1))
    return y, tail[:, lead:, :]


def _merge_kernel(a_ref, r_ref, c_ref, ga_ref, gb_ref, gc_ref, ba_ref, bb_ref, bc_ref, wa_ref, wr_ref, wc_ref, z_ref):
    ya = jnp.dot(a_ref[...], wa_ref[...], preferred_element_type=F32)
    yb = jnp.dot(r_ref[...], wr_ref[...], preferred_element_type=F32)
    yc = jnp.dot(c_ref[...], wc_ref[...], preferred_element_type=F32)
    ga = jax.nn.sigmoid(ga_ref[...].astype(F32) + ba_ref[...])
    gb = jax.nn.sigmoid(gb_ref[...].astype(F32) + bb_ref[...])
    gc = jax.nn.sigmoid(gc_ref[...].astype(F32) + bc_ref[...])
    z_ref[...] = (ga * ya + gb * yb + gc * yc).astype(z_ref.dtype)


def _merge(att, ret, cnv, p_bf, b_gate, wa, wr, wc, tm, tn):
    m = att.shape[0]
    d = wa.shape[1]
    nj = d // tn
    g0 = GATE_OFF // tn
    bg = b_gate.reshape(1, -1)
    return pl.pallas_call(
        _merge_kernel,
        grid=(nj, m // tm),
        in_specs=[pl.BlockSpec((tm, att.shape[1]), lambda j, i: (i, 0)),
                  pl.BlockSpec((tm, ret.shape[1]), lambda j, i: (i, 0)),
                  pl.BlockSpec((tm, cnv.shape[1]), lambda j, i: (i, 0)),
                  pl.BlockSpec((tm, tn), lambda j, i: (i, g0 + j)),
                  pl.BlockSpec((tm, tn), lambda j, i: (i, g0 + nj + j)),
                  pl.BlockSpec((tm, tn), lambda j, i: (i, g0 + 2 * nj + j)),
                  pl.BlockSpec((1, tn), lambda j, i: (0, j)),
                  pl.BlockSpec((1, tn), lambda j, i: (0, nj + j)),
                  pl.BlockSpec((1, tn), lambda j, i: (0, 2 * nj + j)),
                  pl.BlockSpec((wa.shape[0], tn), lambda j, i: (0, j)),
                  pl.BlockSpec((wr.shape[0], tn), lambda j, i: (0, j)),
                  pl.BlockSpec((wc.shape[0], tn), lambda j, i: (0, j))],
        out_specs=pl.BlockSpec((tm, tn), lambda j, i: (i, j)),
        out_shape=jax.ShapeDtypeStruct((m, d), BF16),
        compiler_params=_cp("arbitrary", "arbitrary"),
        name="branch_merge",
    )(att, ret, cnv, p_bf, p_bf, p_bf, bg, bg, bg, wa, wr, wc)


def _outproj_ln_kernel(z_ref, w_ref, x_ref, g_ref, b_ref, of_ref, ob_ref, *, alpha):
    mix = jnp.dot(z_ref[...], w_ref[...], preferred_element_type=F32)
    y = _layer_norm(alpha * x_ref[...] + mix, g_ref[...], b_ref[...])
    of_ref[...] = y
    ob_ref[...] = y.astype(BF16)


def _outproj_ln(z, w_out, x, g, b, alpha, tm):
    m, d = x.shape
    kern = functools.partial(_outproj_ln_kernel, alpha=alpha)
    return pl.pallas_call(
        kern,
        grid=(m // tm,),
        in_specs=[pl.BlockSpec((tm, d), lambda i: (i, 0)),
                  pl.BlockSpec((d, d), lambda i: (0, 0)),
                  pl.BlockSpec((tm, d), lambda i: (i, 0)),
                  pl.BlockSpec((1, d), lambda i: (0, 0)),
                  pl.BlockSpec((1, d), lambda i: (0, 0))],
        out_specs=[pl.BlockSpec((tm, d), lambda i: (i, 0)), pl.BlockSpec((tm, d), lambda i: (i, 0))],
        out_shape=[jax.ShapeDtypeStruct((m, d), F32), jax.ShapeDtypeStruct((m, d), BF16)],
        compiler_params=_cp("arbitrary"),
        name="out_proj_ln1",
    )(z, w_out, x, g.reshape(1, -1), b.reshape(1, -1))


def _router_kernel(x_ref, w_ref, b_ref, gate_ref):
    tm = x_ref.shape[0]
    logits = jnp.dot(x_ref[...], w_ref[...], precision=lax.Precision.HIGHEST, preferred_element_type=F32) + b_ref[...]
    lane = lax.broadcasted_iota(I32, (tm, LANES), 1)
    valid = lane < N_EXPERTS
    lm = jnp.where(valid, logits, NEG_BIG)
    e = jnp.where(valid, jnp.exp(lm - jnp.max(lm, axis=-1, keepdims=True)), 0.0)
    aff = e / jnp.sum(e, axis=-1, keepdims=True)

    def top2(vals):
        v1 = jnp.max(vals, axis=-1, keepdims=True)
        i1 = jnp.min(jnp.where(vals == v1, lane, LANES), axis=-1, keepdims=True)
        rest = jnp.where(lane == i1, -2.0, vals)
        v2 = jnp.max(rest, axis=-1, keepdims=True)
        i2 = jnp.min(jnp.where(rest == v2, lane, LANES), axis=-1, keepdims=True)
        return v1, i1, v2, i2

    grp = lane // EXP_PER_GROUP
    best = jnp.zeros((tm, 1), I32)
    best_score = None
    for g in range(N_GROUPS):
        v1, _, v2, _ = top2(jnp.where(grp == g, aff, -1.0))
        score = v1 + v2
        if g == 0:
            best_score = score
        else:
            better = score > best_score
            best = jnp.where(better, g, best)
            best_score = jnp.where(better, score, best_score)
    v1, i1, v2, i2 = top2(jnp.where(grp == best, aff, -1.0))
    tot = v1 + v2
    gate_ref[...] = jnp.where(lane == i1, v1 / tot, 0.0) + jnp.where(lane == i2, v2 / tot, 0.0)


def _router(x, w_router, b_router, tm):
    m, d = x.shape
    wr = jnp.pad(w_router, ((0, 0), (0, LANES - N_EXPERTS)))
    br = jnp.pad(b_router, (0, LANES - N_EXPERTS)).reshape(1, -1)
    return pl.pallas_call(
        _router_kernel,
        grid=(m // tm,),
        in_specs=[pl.BlockSpec((tm, d), lambda i: (i, 0)),
                  pl.BlockSpec((d, LANES), lambda i: (0, 0)),
                  pl.BlockSpec((1, LANES), lambda i: (0, 0))],
        out_specs=pl.BlockSpec((tm, LANES), lambda i: (i, 0)),
        out_shape=jax.ShapeDtypeStruct((m, LANES), F32),
        compiler_params=_cp("arbitrary"),
        name="router",
    )(x, wr, br)


def _moe_kernel(xb_ref, w1_ref, w3_ref, w2_ref, gate_ref, x_ref, g_ref, b_ref, of_ref, ob_ref, acc_ref, *, alpha):
    e = pl.program_id(1)

    @pl.when(e == 0)
    def _():
        acc_ref[...] = jnp.zeros(acc_ref.shape, F32)

    xb = xb_ref[...]
    h1 = jnp.dot(xb, w1_ref[...], preferred_element_type=F32)
    h3 = jnp.dot(xb, w3_ref[...], preferred_element_type=F32)
    lane = lax.broadcasted_iota(I32, gate_ref.shape, 1)
    ge = jnp.sum(jnp.where(lane == e, gate_ref[...], 0.0), axis=-1, keepdims=True)
    h = h1 * jax.nn.sigmoid(h1) * h3 * ge
    acc_ref[...] += jnp.dot(h.astype(BF16), w2_ref[...], preferred_element_type=F32)

    @pl.when(e == pl.num_programs(1) - 1)
    def _():
        y = _layer_norm(alpha * x_ref[...] + acc_ref[...], g_ref[...], b_ref[...])
        of_ref[...] = y
        ob_ref[...] = y.astype(BF16)


def _moe_ln(xb, w1, w3, w2, gate, x, g, b, alpha, tm):
    m, d = x.shape
    ne, _, de = w1.shape
    kern = functools.partial(_moe_kernel, alpha=alpha)
    return pl.pallas_call(
        kern,
        grid=(m // tm, ne),
        in_specs=[pl.BlockSpec((tm, d), lambda i, e: (i, 0)),
                  pl.BlockSpec((None, d, de), lambda i, e: (e, 0, 0)),
                  pl.BlockSpec((None, d, de), lambda i, e: (e, 0, 0)),
                  pl.BlockSpec((None, de, d), lambda i, e: (e, 0, 0)),
                  pl.BlockSpec((tm, LANES), lambda i, e: (i, 0)),
                  pl.BlockSpec((tm, d), lambda i, e: (i, 0)),
                  pl.BlockSpec((1, d), lambda i, e: (0, 0)),
                  pl.BlockSpec((1, d), lambda i, e: (0, 0))],
        out_specs=[pl.BlockSpec((tm, d), lambda i, e: (i, 0)), pl.BlockSpec((tm, d), lambda i, e: (i, 0))],
        out_shape=[jax.ShapeDtypeStruct((m, d), F32), jax.ShapeDtypeStruct((m, d), BF16)],
        scratch_shapes=[pltpu.VMEM((tm, d), F32)],
        compiler_params=_cp("arbitrary", "arbitrary"),
        name="moe_ln2",
    )(xb, w1, w3, w2, gate, x, g.reshape(1, -1), b.reshape(1, -1))


def kernel(x_prompt, x_sample, cache_k, cache_v, cache_idx_k, state_ret, state_conv, w_in, b_gate, w_att_o, ret_gn_g,
           w_ret_o, conv_dw, conv_dw_b, conv_ln_g, conv_ln_b, w_conv_o, w_out, ln1_g, ln1_b, w_router, b_router,
           moe_w1, moe_w3, moe_w2, ln2_g, ln2_b):
    nbp, tp, d = x_prompt.shape
    nbs, ts, _ = x_sample.shape
    depth = w_in.shape[0]
    past = cache_k.shape[2]
    np_, ns = nbp * tp, nbs * ts
    n = np_ + ns
    alpha = (2 * depth) ** 0.25
    kvw = KV_HEADS * ATT_HD

    tm = _pow2_tile(n, 512)
    tq_p = _pow2_tile(tp, 128)
    cl_p = _pow2_tile(tp, 256)
    tt_p = _pow2_tile(tp, 256)
    ls_true = past + ts
    ls_pad = -(-ls_true // (2 * LANES)) * (2 * LANES)
    assert tp % (2 * LANES) == 0 and np_ % max(tq_p, cl_p, tt_p, ts) == 0

    x = jnp.concatenate([x_prompt.reshape(np_, d), x_sample.reshape(ns, d)], axis=0)
    xb = x.astype(BF16)

    c1 = ATT_HEADS * ATT_HD
    c2 = c1 + F32_USED
    ks, vs, iks, rps, cps, kss, vss, ikss, rss, css = ([] for _ in range(10))
    zero_ret = jnp.zeros((nbp, RET_HEADS, RET_DK, RET_DV), F32)
    zero_conv = jnp.zeros((nbp, CONV_W - 1, CONV_CH), F32)

    for l in range(depth):
        w_l = w_in[l]
        w_bf = jnp.concatenate([w_l[:, :c1], w_l[:, c2:]], axis=1).astype(BF16)
        w_f = jnp.pad(w_l[:, c1:c2], ((0, 0), (0, F32_WIDTH - F32_USED))).astype(BF16)
        p_bf = _matmul(xb, w_bf, BF16, tm, 1024)
        p_f = _matmul(xb, w_f, F32, tm, F32_WIDTH)

        k_new = p_f[:, KA_OFF:KA_OFF + kvw]
        v_new = p_f[:, VA_OFF:VA_OFF + kvw]
        ik_new = p_f[:, KI_OFF:KI_OFF + IDX_DIM]
        ks.append(k_new[:np_].reshape(nbp, tp, KV_HEADS, ATT_HD))
        vs.append(v_new[:np_].reshape(nbp, tp, KV_HEADS, ATT_HD))
        iks.append(ik_new[:np_].reshape(nbp, tp, IDX_DIM))
        kss.append(k_new[np_:].reshape(nbs, ts, KV_HEADS, ATT_HD))
        vss.append(v_new[np_:].reshape(nbs, ts, KV_HEADS, ATT_HD))
        ikss.append(ik_new[np_:].reshape(nbs, ts, IDX_DIM))

        nq = tp // tq_p
        hq = ATT_HEADS * ATT_HD
        att_p = _attention(
            p_bf, lambda tq: ((tq, hq), lambda b, j: (b * nq + j, 0)),
            p_f, lambda lp: ((lp, kvw), lambda b, j: (b, KA_OFF // kvw)),
            p_f, lambda lp: ((lp, kvw), lambda b, j: (b, VA_OFF // kvw)),
            p_f, lambda tq: ((tq, IDX_HEADS * IDX_DIM), lambda b, j: (b * nq + j, QI_OFF // (IDX_HEADS * IDX_DIM))),
            p_f, lambda lp: ((lp, LANES), lambda b, j: (b, KI_OFF // LANES)),
            p_f, lambda tq: ((tq, LANES), lambda b, j: (b * nq + j, KI_OFF // LANES)),
            nb=nbp, t=tp, l_pad=tp, l_true=tp, offset=0, tq=tq_p)

        padk = ls_pad - ls_true
        k_all = jnp.concatenate([cache_k[l].reshape(nbs, past, kvw), k_new[np_:].reshape(nbs, ts, kvw),
                                 jnp.zeros((nbs, padk, kvw), F32)], axis=1)
        v_all = jnp.concatenate([cache_v[l].reshape(nbs, past, kvw), v_new[np_:].reshape(nbs, ts, kvw),
                                 jnp.zeros((nbs, padk, kvw), F32)], axis=1)
        ik_all = jnp.concatenate([cache_idx_k[l], ik_new[np_:].reshape(nbs, ts, IDX_DIM),
                                  jnp.zeros((nbs, padk, IDX_DIM), F32)], axis=1)
        ik_all = jnp.pad(ik_all, ((0, 0), (0, 0), (0, LANES - IDX_DIM)))
        sb = np_ // ts
        att_s = _attention(
            p_bf, lambda tq: ((tq, hq), lambda b, j: (sb + b, 0)),
            k_all, lambda lp: ((None, lp, kvw), lambda b, j: (b, 0, 0)),
            v_all, lambda lp: ((None, lp, kvw), lambda b, j: (b, 0, 0)),
            p_f, lambda tq: ((tq, IDX_HEADS * IDX_DIM), lambda b, j: (sb + b, QI_OFF // (IDX_HEADS * IDX_DIM))),
            ik_all, lambda lp: ((None, lp, LANES), lambda b, j: (b, 0, 0)),
            p_f, lambda tq: ((tq, LANES), lambda b, j: (sb + b, KI_OFF // LANES)),
            nb=nbs, t=ts, l_pad=ls_pad, l_true=ls_true, offset=past, tq=ts)
        att = jnp.concatenate([att_p, att_s], axis=0)

        ret_p, rs_p = _retention(p_bf, 0, nbp, tp, cl_p, 0, zero_ret, ret_gn_g[l])
        ret_s, rs_s = _retention(p_bf, np_, nbs, ts, ts, past, state_ret[l], ret_gn_g[l])
        ret = jnp.concatenate([ret_p, ret_s], axis=0)
        rps.append(rs_p)
        rss.append(rs_s)

        cnv_p, cs_p = _conv_module(p_bf, 0, nbp, tp, tt_p, zero_conv, conv_dw[l], conv_dw_b[l], conv_ln_g[l], conv_ln_b[l])
        cnv_s, cs_s = _conv_module(p_bf, np_, nbs, ts, ts, state_conv[l], conv_dw[l], conv_dw_b[l], conv_ln_g[l], conv_ln_b[l])
        cnv = jnp.concatenate([cnv_p, cnv_s], axis=0)
        cps.append(cs_p)
        css.append(cs_s)

        z = _merge(att, ret, cnv, p_bf, b_gate[l], w_att_o[l].astype(BF16), w_ret_o[l].astype(BF16),
                   w_conv_o[l].astype(BF16), tm, 512)
        x, xb = _outproj_ln(z, w_out[l].astype(BF16), x, ln1_g[l], ln1_b[l], alpha, _pow2_tile(n, 256))
        gate = _router(x, w_router, b_router, _pow2_tile(n, 256))
        x, xb = _moe_ln(xb, moe_w1[l].astype(BF16), moe_w3[l].astype(BF16), moe_w2[l].astype(BF16), gate, x,
                        ln2_g[l], ln2_b[l], alpha, tm)

    y_prompt = x[:np_].reshape(nbp, tp, d)
    y_sample = x[np_:].reshape(nbs, ts, d)
    st = jnp.stack
    return (y_prompt, y_sample, st(ks), st(vs), st(iks), st(rps), st(cps), st(kss), st(vss), st(ikss), st(rss), st(css))
```

```python
import functools

import jax
import jax.numpy as jnp
from jax import lax
from jax.experimental import pallas as pl
from jax.experimental.pallas import tpu as pltpu

F32 = jnp.float32
BF16 = jnp.bfloat16
I32 = jnp.int32

CHUNK = 64
TOPK_MAX = 256
ATT_HEADS, ATT_HD, KV_HEADS = 8, 128, 2
KV_GROUP = ATT_HEADS // KV_HEADS
IDX_HEADS, IDX_DIM = 8, 64
RET_HEADS, RET_DK, RET_DV = 8, 128, 256
ROPE_BASE = 10000.0
CONV_CH, CONV_W = 1024, 31
N_EXPERTS, N_GROUPS = 16, 4
EXP_PER_GROUP = N_EXPERTS // N_GROUPS
LN_EPS = 1e-5

LANES = 128
SUBLANES = 8

INT_MIN = -2 ** 31
NEG_BIG = -1e30

QA_OFF = 0
QR_OFF = QA_OFF + ATT_HEADS * ATT_HD
KR_OFF = QR_OFF + RET_HEADS * RET_DK
VR_OFF = KR_OFF + RET_HEADS * RET_DK
GR_OFF = VR_OFF + RET_HEADS * RET_DV
CIN_OFF = GR_OFF + RET_HEADS * RET_DV
GATE_OFF = CIN_OFF + 2 * CONV_CH
KA_OFF = 0
VA_OFF = KA_OFF + KV_HEADS * ATT_HD
QI_OFF = VA_OFF + KV_HEADS * ATT_HD
KI_OFF = QI_OFF + IDX_HEADS * IDX_DIM
WI_OFF = KI_OFF + IDX_DIM
F32_USED = WI_OFF + IDX_HEADS
F32_WIDTH = -(-F32_USED // LANES) * LANES

VMEM_LIMIT = 56 * 1024 * 1024


def _cp(*sem):
    return pltpu.CompilerParams(dimension_semantics=sem, vmem_limit_bytes=VMEM_LIMIT)


def _pow2_tile(n, pref):
    t = pref
    while n % t:
        t //= 2
    return t


def _layer_norm(x, g, b):
    mu = jnp.mean(x, axis=-1, keepdims=True)
    xc = x - mu
    var = jnp.mean(xc * xc, axis=-1, keepdims=True)
    return xc * lax.rsqrt(var + LN_EPS) * g + b


def _mm_kernel(x_ref, w_ref, o_ref):
    o_ref[...] = jnp.dot(x_ref[...], w_ref[...], preferred_element_type=F32).astype(o_ref.dtype)


def _matmul(x, w, out_dtype, tm, tn):
    m, k = x.shape
    n = w.shape[1]
    return pl.pallas_call(
        _mm_kernel,
        grid=(n // tn, m // tm),
        in_specs=[pl.BlockSpec((tm, k), lambda j, i: (i, 0)),
                  pl.BlockSpec((k, tn), lambda j, i: (0, j))],
        out_specs=pl.BlockSpec((tm, tn), lambda j, i: (i, j)),
        out_shape=jax.ShapeDtypeStruct((m, n), out_dtype),
        compiler_params=_cp("arbitrary", "arbitrary"),
        name="in_proj",
    )(x, w)


def _attn_kernel(q_ref, k_ref, v_ref, qi_ref, ki_ref, wi_ref, o_ref,
                 key_ref, lg_ref, mx_ref, acc_ref, m_ref,
                 *, tq, nq, ck, nck, offset, l_true, k_sel, tie_bits):
    j = pl.program_id(1)
    nsub = ck // LANES
    qlane = lax.broadcasted_iota(I32, (1, tq), 1)
    krow = lax.broadcasted_iota(I32, (ck, tq), 0)
    pos = offset + j * tq + qlane
    limit = jnp.minimum((pos // CHUNK + 1) * CHUNK, l_true)
    if nq == 1:
        n_act = (min(((offset + tq - 1) // CHUNK + 1) * CHUNK, l_true) + ck - 1) // ck
    else:
        last_pos = offset + (j + 1) * tq - 1
        max_limit = jnp.minimum((last_pos // CHUNK + 1) * CHUNK, l_true)
        n_act = (max_limit + ck - 1) // ck

    qi = (qi_ref[...] * (IDX_DIM ** -0.5)).astype(BF16)
    qis = [qi[:, h * IDX_DIM:(h + 1) * IDX_DIM] for h in range(IDX_HEADS)]
    stacked = tq % LANES == 0
    if stacked:
        qstack = jnp.concatenate(qis, axis=0)
    wt = wi_ref[...] * (IDX_HEADS ** -0.5)
    wrows = [wt[h:h + 1, :] for h in range(IDX_HEADS)]
    nt = (((1,), (1,)), ((), ()))

    def score_body(c, carry):
        for u in range(nsub):
            r0 = pl.multiple_of(c * ck + u * LANES, LANES)
            kiu = ki_ref[pl.ds(r0, LANES), :][:, :IDX_DIM].astype(BF16)
            if stacked:
                d_all = lax.dot_general(kiu, qstack, nt, preferred_element_type=F32)
                ds = [d_all[:, h * tq:(h + 1) * tq] for h in range(IDX_HEADS)]
            else:
                ds = [lax.dot_general(kiu, qis[h], nt, preferred_element_type=F32) for h in range(IDX_HEADS)]
            s = jnp.zeros((LANES, tq), F32)
            for h in range(IDX_HEADS):
                s = s + jnp.maximum(ds[h], 0.0) * wrows[h]
            bits = lax.bitcast_convert_type(s, I32)
            bits = jnp.where(bits == INT_MIN, 0, bits)
            key = jnp.where(bits >= 0, bits, bits ^ 0x7FFFFFFF)
            kpos = r0 + lax.broadcasted_iota(I32, (LANES, tq), 0)
            key_ref[c, u * LANES:(u + 1) * LANES, :] = jnp.where(kpos < limit, key, INT_MIN)
        return carry

    lax.fori_loop(0, n_act, score_body, 0)

    def count(pred):
        def body(c, acc):
            p = jnp.where(pred(key_ref[c], c * ck + krow), 1.0, 0.0)
            parts = [p[r * SUBLANES:(r + 1) * SUBLANES] for r in range(ck // SUBLANES)]
            while len(parts) > 1:
                parts = [a + b for a, b in zip(parts[0::2], parts[1::2])]
            return acc + parts[0]
        acc = lax.fori_loop(0, n_act, body, jnp.zeros((SUBLANES, tq), F32))
        return jnp.sum(acc, axis=0, keepdims=True)

    def bisect(i, tu):
        cand_u = tu | (jnp.int32(1) << (31 - i))
        cand_s = cand_u ^ INT_MIN
        cnt = count(lambda key, idx: key >= cand_s)
        return jnp.where(cnt >= k_sel, cand_u, tu)

    tu = lax.fori_loop(0, 32, bisect, jnp.zeros((1, tq), I32))
    ts = tu ^ INT_MIN
    c_gt = count(lambda key, idx: key > ts)
    c_ge = count(lambda key, idx: key >= ts)
    want = k_sel - c_gt

    eye = jnp.where(lax.broadcasted_iota(I32, (tq, tq), 0) == lax.broadcasted_iota(I32, (tq, tq), 1), 1.0, 0.0).astype(BF16)
    ts_adm = jnp.maximum(ts, INT_MIN + 1)
    m_ref[...] = jnp.full((1, tq), 2 ** 30, I32)

    @pl.when(jnp.max(c_ge) > k_sel)
    def _():
        def tie_bisect(i, m):
            cand = m | (jnp.int32(1) << (tie_bits - 1 - i))
            cnt = count(lambda key, idx: (key == ts) & (idx < cand))
            return jnp.where(cnt < want, cand, m)
        m_ref[...] = lax.fori_loop(0, tie_bits, tie_bisect, jnp.zeros((1, tq), I32))

    m_last = m_ref[...]

    qscale = (ATT_HD ** -0.5) * 1.4426950408889634
    groups = range(KV_HEADS)
    qss = []
    for g in groups:
        qs = jnp.concatenate(
            [q_ref[:, (g * KV_GROUP + r) * ATT_HD:(g * KV_GROUP + r + 1) * ATT_HD] for r in range(KV_GROUP)], axis=0)
        qss.append((qs.astype(F32) * qscale).astype(BF16))
    mx_ref[...] = jnp.full(mx_ref.shape, NEG_BIG, F32)

    def logit_body(c, carry):
        r0 = pl.multiple_of(c * ck, ck)
        key = key_ref[c]
        sel = (key > ts_adm) | ((key == ts_adm) & (c * ck + krow <= m_last))
        sel_t = jnp.where(sel, 1.0, 0.0).astype(BF16)
        sel_q = lax.dot_general(eye, sel_t, nt, preferred_element_type=F32)
        b = (sel_q - 1.0) * (-NEG_BIG)
        b4 = jnp.concatenate([b] * KV_GROUP, axis=0)
        for g in groups:
            kc = k_ref[pl.ds(r0, ck), g * ATT_HD:(g + 1) * ATT_HD].astype(BF16)
            lg = lax.dot_general(qss[g], kc, nt, preferred_element_type=F32) + b4
            lg_ref[g, c] = lg
            m = mx_ref[g]
            for u in range(nsub):
                m = jnp.maximum(m, lg[:, u * LANES:(u + 1) * LANES])
            mx_ref[g] = m
        return carry

    lax.fori_loop(0, n_act, logit_body, 0)
    m_rows = [jnp.max(mx_ref[g], axis=-1, keepdims=True) for g in groups]
    acc_ref[...] = jnp.zeros(acc_ref.shape, F32)
    ones_cols = jnp.ones((ck, ATT_HD), BF16)

    def pv_body(c, carry):
        r0 = pl.multiple_of(c * ck, ck)
        for g in groups:
            p = jnp.exp2(lg_ref[g, c] - m_rows[g]).astype(BF16)
            vc = v_ref[pl.ds(r0, ck), g * ATT_HD:(g + 1) * ATT_HD].astype(BF16)
            acc_ref[g] += jnp.dot(p, jnp.concatenate([vc, ones_cols], axis=1), preferred_element_type=F32)
        return carry

    lax.fori_loop(0, n_act, pv_body, 0)
    for g in groups:
        a = acc_ref[g]
        o = a[:, :ATT_HD] / a[:, ATT_HD:]
        for r in range(KV_GROUP):
            h = g * KV_GROUP + r
            o_ref[:, h * ATT_HD:(h + 1) * ATT_HD] = o[r * tq:(r + 1) * tq].astype(o_ref.dtype)


def _attention(q_arr, q_map, k_arr, k_map, v_arr, v_map, qi_arr, qi_map, ki_arr, ki_map, wi_arr, wi_map,
               *, nb, t, l_pad, l_true, offset, tq):
    ck = 2 * LANES
    nck = l_pad // ck
    assert l_pad % ck == 0 and t % tq == 0 and nck * (ck // LANES) < 256
    nq = t // tq
    k_sel = min(TOPK_MAX, l_true // 4)
    tie_bits = max(1, (l_pad - 1).bit_length())
    kern = functools.partial(_attn_kernel, tq=tq, nq=nq, ck=ck, nck=nck, offset=offset, l_true=l_true,
                             k_sel=k_sel, tie_bits=tie_bits)
    hq = ATT_HEADS * ATT_HD
    return pl.pallas_call(
        kern,
        grid=(nb, nq),
        in_specs=[pl.BlockSpec(*q_map(tq)), pl.BlockSpec(*k_map(l_pad)), pl.BlockSpec(*v_map(l_pad)),
                  pl.BlockSpec(*qi_map(tq)), pl.BlockSpec(*ki_map(l_pad)), pl.BlockSpec(*wi_map(tq))],
        out_specs=pl.BlockSpec((tq, hq), lambda b, j: (b * nq + j, 0)),
        out_shape=jax.ShapeDtypeStruct((nb * t, hq), BF16),
        scratch_shapes=[pltpu.VMEM((nck, ck, tq), I32),
                        pltpu.VMEM((KV_HEADS, nck, KV_GROUP * tq, ck), F32),
                        pltpu.VMEM((KV_HEADS, KV_GROUP * tq, LANES), F32),
                        pltpu.VMEM((KV_HEADS, KV_GROUP * tq, 2 * ATT_HD), F32),
                        pltpu.VMEM((1, tq), I32)],
        compiler_params=_cp("arbitrary", "arbitrary"),
        name="dsa_attention",
    )(q_arr, k_arr, v_arr, qi_arr, ki_arr, wi_arr)


def _ret_kernel(q_ref, k_ref, v_ref, g_ref, cs_ref, sn_ref, dec_ref, xi_ref, zt_ref, gp_ref, gn_ref, s0_ref,
                y_ref, sf_ref, s_ref):
    c = pl.program_id(2)

    @pl.when(c == 0)
    def _():
        s_ref[...] = s0_ref[...]

    cs = cs_ref[...]
    sn = sn_ref[...]

    def rot(x):
        xf = x.astype(F32)
        return xf * cs + pltpu.roll(xf, RET_DK // 2, 1) * sn

    q = rot(q_ref[...])
    k = rot(k_ref[...]) * (RET_DK ** -0.5)
    qb = q.astype(BF16)
    kb = k.astype(BF16)
    v = v_ref[...]
    inner = lax.dot_general(qb, kb, (((1,), (1,)), ((), ())), preferred_element_type=F32) * dec_ref[...]
    s = s_ref[...]
    o = jnp.dot(inner.astype(BF16), v, preferred_element_type=F32)
    o = o + jnp.dot(qb, s.astype(BF16), preferred_element_type=F32) * xi_ref[...]
    kz = (k * zt_ref[...]).T.astype(BF16)
    s_new = s * gp_ref[...] + jnp.dot(kz, v, preferred_element_type=F32)
    s_ref[...] = s_new
    sf_ref[...] = s_new

    mu = jnp.mean(o, axis=-1, keepdims=True)
    oc = o - mu
    var = jnp.mean(oc * oc, axis=-1, keepdims=True)
    yn = oc * lax.rsqrt(var + LN_EPS) * gn_ref[...]
    gg = g_ref[...].astype(F32)
    y_ref[...] = (gg * jax.nn.sigmoid(gg) * yn).astype(y_ref.dtype)


def _retention(p_bf, row0, nb, t, cl, offset, s0, gn_g):
    nc = t // cl
    h_, dk, dv = RET_HEADS, RET_DK, RET_DV
    half = dk // 2
    pos = (offset + jnp.arange(t, dtype=I32)).astype(F32)
    inv = 1.0 / (ROPE_BASE ** (jnp.arange(half, dtype=F32) / half))
    ang = pos[:, None] * inv[None, :]
    cos, sin = jnp.cos(ang), jnp.sin(ang)
    cs2 = jnp.concatenate([cos, cos], axis=-1)
    sn2 = jnp.concatenate([-sin, sin], axis=-1)
    log_g = jnp.log1p(-jnp.exp2(-5.0 - jnp.arange(h_, dtype=F32)))
    n = jnp.arange(cl, dtype=F32)
    diff = n[:, None] - n[None, :]
    decay = jnp.where(diff >= 0, jnp.exp(log_g[:, None, None] * jnp.maximum(diff, 0.0)), 0.0)
    xi = jnp.exp(log_g[:, None] * (n + 1.0))[..., None]
    zeta = jnp.exp(log_g[:, None] * (cl - 1.0 - n))[..., None]
    gpow = jnp.broadcast_to(jnp.exp(log_g * cl)[:, None, None], (h_, 1, dv))
    rb = row0 // cl
    return pl.pallas_call(
        _ret_kernel,
        grid=(nb, h_, nc),
        in_specs=[pl.BlockSpec((cl, dk), lambda b, h, c: (rb + b * nc + c, QR_OFF // dk + h)),
                  pl.BlockSpec((cl, dk), lambda b, h, c: (rb + b * nc + c, KR_OFF // dk + h)),
                  pl.BlockSpec((cl, dv), lambda b, h, c: (rb + b * nc + c, VR_OFF // dv + h)),
                  pl.BlockSpec((cl, dv), lambda b, h, c: (rb + b * nc + c, GR_OFF // dv + h)),
                  pl.BlockSpec((cl, dk), lambda b, h, c: (c, 0)),
                  pl.BlockSpec((cl, dk), lambda b, h, c: (c, 0)),
                  pl.BlockSpec((None, cl, cl), lambda b, h, c: (h, 0, 0)),
                  pl.BlockSpec((None, cl, 1), lambda b, h, c: (h, 0, 0)),
                  pl.BlockSpec((None, cl, 1), lambda b, h, c: (h, 0, 0)),
                  pl.BlockSpec((None, 1, dv), lambda b, h, c: (h, 0, 0)),
                  pl.BlockSpec((1, dv), lambda b, h, c: (0, h)),
                  pl.BlockSpec((None, None, dk, dv), lambda b, h, c: (b, h, 0, 0))],
        out_specs=[pl.BlockSpec((cl, dv), lambda b, h, c: (b * nc + c, h)),
                   pl.BlockSpec((None, None, dk, dv), lambda b, h, c: (b, h, 0, 0))],
        out_shape=[jax.ShapeDtypeStruct((nb * t, h_ * dv), BF16),
                   jax.ShapeDtypeStruct((nb, h_, dk, dv), F32)],
        scratch_shapes=[pltpu.VMEM((dk, dv), F32)],
        compiler_params=_cp("arbitrary", "arbitrary", "arbitrary"),
        name="retention",
    )(p_bf, p_bf, p_bf, p_bf, cs2, sn2, decay, xi, zeta, gpow, gn_g.reshape(1, -1), s0)


CONV_HALO = 32
CONV_ROWS = 16


def _conv_kernel(ca_ref, cb_ref, st_ref, w_ref, b_ref, lg_ref, lb_ref, y_ref, tail_ref, sh_ref, *, tt):
    t = pl.program_id(1)
    ext = tt + CONV_HALO

    @pl.when(t == 0)
    def _():
        sh_ref[0, 0:CONV_HALO] = st_ref[...]

    @pl.when(t > 0)
    def _():
        sh_ref[0, 0:CONV_HALO] = sh_ref[0, tt:ext]

    ca = ca_ref[...].astype(F32)
    cb = cb_ref[...].astype(F32)
    sh_ref[0, CONV_HALO:ext] = ca * jax.nn.sigmoid(cb)
    tail_ref[...] = sh_ref[0, tt:ext]
    for s in range(1, SUBLANES):
        sh_ref[s, 0:ext - SUBLANES] = sh_ref[0, s:ext - SUBLANES + s]

    lead = CONV_HALO - (CONV_W - 1)

    def body(i, carry):
        r0 = pl.multiple_of(i * CONV_ROWS, CONV_ROWS)
        acc = jnp.broadcast_to(b_ref[...], (CONV_ROWS, CONV_CH))
        for k in range(CONV_W):
            a, s = divmod(k + lead, SUBLANES)
            acc = acc + w_ref[k:k + 1, :] * sh_ref[s, pl.ds(r0 + a * SUBLANES, CONV_ROWS), :]
        hn = _layer_norm(acc, lg_ref[...], lb_ref[...])
        y_ref[pl.ds(r0, CONV_ROWS), :] = (hn * jax.nn.sigmoid(hn)).astype(y_ref.dtype)
        return carry

    lax.fori_loop(0, tt // CONV_ROWS, body, 0)


def _conv_module(p_bf, row0, nb, t, tt, state, dw, dw_b, ln_g, ln_b):
    nt = t // tt
    lead = CONV_HALO - (CONV_W - 1)
    st = jnp.pad(state, ((0, 0), (lead, 0), (0, 0)))
    rb = row0 // tt
    kern = functools.partial(_conv_kernel, tt=tt)
    y, tail = pl.pallas_call(
        kern,
        grid=(nb, nt),
        in_specs=[pl.BlockSpec((tt, CONV_CH), lambda b, i: (rb + b * nt + i, CIN_OFF // CONV_CH)),
                  pl.BlockSpec((tt, CONV_CH), lambda b, i: (rb + b * nt + i, CIN_OFF // CONV_CH + 1)),
                  pl.BlockSpec((None, CONV_HALO, CONV_CH), lambda b, i: (b, 0, 0)),
                  pl.BlockSpec((CONV_W, CONV_CH), lambda b, i: (0, 0)),
                  pl.BlockSpec((1, CONV_CH), lambda b, i: (0, 0)),
                  pl.BlockSpec((1, CONV_CH), lambda b, i: (0, 0)),
                  pl.BlockSpec((1, CONV_CH), lambda b, i: (0, 0))],
        out_specs=[pl.BlockSpec((tt, CONV_CH), lambda b, i: (b * nt + i, 0)),
                   pl.BlockSpec((None, CONV_HALO, CONV_CH), lambda b, i: (b, 0, 0))],
        out_shape=[jax.ShapeDtypeStruct((nb * t, CONV_CH), BF16),
                   jax.ShapeDtypeStruct((nb, CONV_HALO, CONV_CH), F32)],
        scratch_shapes=[pltpu.VMEM((SUBLANES, tt + CONV_HALO, CONV_CH), F32)],
        compiler_params=_cp("arbitrary", "arbitrary"),
        name="conv_module",
    )(p_bf, p_bf, st, dw, dw_b.reshape(1, -1), ln_g.reshape(1, -1), ln_b.reshape(1, -1))
    return y, tail[:, lead:, :]


def _merge_kernel(a_ref, r_ref, c_ref, ga_ref, gb_ref, gc_ref, ba_ref, bb_ref, bc_ref, wa_ref, wr_ref, wc_ref, z_ref):
    ya = jnp.dot(a_ref[...], wa_ref[...], preferred_element_type=F32)
    yb = jnp.dot(r_ref[...], wr_ref[...], preferred_element_type=F32)
    yc = jnp.dot(c_ref[...], wc_ref[...], preferred_element_type=F32)
    ga = jax.nn.sigmoid(ga_ref[...].astype(F32) + ba_ref[...])
    gb = jax.nn.sigmoid(gb_ref[...].astype(F32) + bb_ref[...])
    gc = jax.nn.sigmoid(gc_ref[...].astype(F32) + bc_ref[...])
    z_ref[...] = (ga * ya + gb * yb + gc * yc).astype(z_ref.dtype)


def _merge(att, ret, cnv, p_bf, b_gate, wa, wr, wc, tm, tn):
    m = att.shape[0]
    d = wa.shape[1]
    nj = d // tn
    g0 = GATE_OFF // tn
    bg = b_gate.reshape(1, -1)
    return pl.pallas_call(
        _merge_kernel,
        grid=(nj, m // tm),
        in_specs=[pl.BlockSpec((tm, att.shape[1]), lambda j, i: (i, 0)),
                  pl.BlockSpec((tm, ret.shape[1]), lambda j, i: (i, 0)),
                  pl.BlockSpec((tm, cnv.shape[1]), lambda j, i: (i, 0)),
                  pl.BlockSpec((tm, tn), lambda j, i: (i, g0 + j)),
                  pl.BlockSpec((tm, tn), lambda j, i: (i, g0 + nj + j)),
                  pl.BlockSpec((tm, tn), lambda j, i: (i, g0 + 2 * nj + j)),
                  pl.BlockSpec((1, tn), lambda j, i: (0, j)),
                  pl.BlockSpec((1, tn), lambda j, i: (0, nj + j)),
                  pl.BlockSpec((1, tn), lambda j, i: (0, 2 * nj + j)),
                  pl.BlockSpec((wa.shape[0], tn), lambda j, i: (0, j)),
                  pl.BlockSpec((wr.shape[0], tn), lambda j, i: (0, j)),
                  pl.BlockSpec((wc.shape[0], tn), lambda j, i: (0, j))],
        out_specs=pl.BlockSpec((tm, tn), lambda j, i: (i, j)),
        out_shape=jax.ShapeDtypeStruct((m, d), BF16),
        compiler_params=_cp("arbitrary", "arbitrary"),
        name="branch_merge",
    )(att, ret, cnv, p_bf, p_bf, p_bf, bg, bg, bg, wa, wr, wc)


def _outproj_ln_kernel(z_ref, w_ref, x_ref, g_ref, b_ref, of_ref, ob_ref, *, alpha):
    mix = jnp.dot(z_ref[...], w_ref[...], preferred_element_type=F32)
    y = _layer_norm(alpha * x_ref[...] + mix, g_ref[...], b_ref[...])
    of_ref[...] = y
    ob_ref[...] = y.astype(BF16)


def _outproj_ln(z, w_out, x, g, b, alpha, tm):
    m, d = x.shape
    kern = functools.partial(_outproj_ln_kernel, alpha=alpha)
    return pl.pallas_call(
        kern,
        grid=(m // tm,),
        in_specs=[pl.BlockSpec((tm, d), lambda i: (i, 0)),
                  pl.BlockSpec((d, d), lambda i: (0, 0)),
                  pl.BlockSpec((tm, d), lambda i: (i, 0)),
                  pl.BlockSpec((1, d), lambda i: (0, 0)),
                  pl.BlockSpec((1, d), lambda i: (0, 0))],
        out_specs=[pl.BlockSpec((tm, d), lambda i: (i, 0)), pl.BlockSpec((tm, d), lambda i: (i, 0))],
        out_shape=[jax.ShapeDtypeStruct((m, d), F32), jax.ShapeDtypeStruct((m, d), BF16)],
        compiler_params=_cp("arbitrary"),
        name="out_proj_ln1",
    )(z, w_out, x, g.reshape(1, -1), b.reshape(1, -1))


def _router_kernel(x_ref, w_ref, b_ref, gate_ref):
    tm = x_ref.shape[0]
    logits = jnp.dot(x_ref[...], w_ref[...], precision=lax.Precision.HIGHEST, preferred_element_type=F32) + b_ref[...]
    lane = lax.broadcasted_iota(I32, (tm, LANES), 1)
    valid = lane < N_EXPERTS
    lm = jnp.where(valid, logits, NEG_BIG)
    e = jnp.where(valid, jnp.exp(lm - jnp.max(lm, axis=-1, keepdims=True)), 0.0)
    aff = e / jnp.sum(e, axis=-1, keepdims=True)

    def top2(vals):
        v1 = jnp.max(vals, axis=-1, keepdims=True)
        i1 = jnp.min(jnp.where(vals == v1, lane, LANES), axis=-1, keepdims=True)
        rest = jnp.where(lane == i1, -2.0, vals)
        v2 = jnp.max(rest, axis=-1, keepdims=True)
        i2 = jnp.min(jnp.where(rest == v2, lane, LANES), axis=-1, keepdims=True)
        return v1, i1, v2, i2

    grp = lane // EXP_PER_GROUP
    best = jnp.zeros((tm, 1), I32)
    best_score = None
    for g in range(N_GROUPS):
        v1, _, v2, _ = top2(jnp.where(grp == g, aff, -1.0))
        score = v1 + v2
        if g == 0:
            best_score = score
        else:
            better = score > best_score
            best = jnp.where(better, g, best)
            best_score = jnp.where(better, score, best_score)
    v1, i1, v2, i2 = top2(jnp.where(grp == best, aff, -1.0))
    tot = v1 + v2
    gate_ref[...] = (jnp.where(lane == 0, i1.astype(F32), 0.0) + jnp.where(lane == 1, i2.astype(F32), 0.0)
                     + jnp.where(lane == 2, v1 / tot, 0.0) + jnp.where(lane == 3, v2 / tot, 0.0))


def _router(x, w_router, b_router, tm):
    m, d = x.shape
    wr = jnp.pad(w_router, ((0, 0), (0, LANES - N_EXPERTS)))
    br = jnp.pad(b_router, (0, LANES - N_EXPERTS)).reshape(1, -1)
    return pl.pallas_call(
        _router_kernel,
        grid=(m // tm,),
        in_specs=[pl.BlockSpec((tm, d), lambda i: (i, 0)),
                  pl.BlockSpec((d, LANES), lambda i: (0, 0)),
                  pl.BlockSpec((1, LANES), lambda i: (0, 0))],
        out_specs=pl.BlockSpec((tm, LANES), lambda i: (i, 0)),
        out_shape=jax.ShapeDtypeStruct((m, LANES), F32),
        compiler_params=_cp("arbitrary"),
        name="router",
    )(x, wr, br)


MOE_TILE = 256


def _gather_rows(idx_ref, src_hbm, dst_ref, sem, n):
    for r in range(n):
        pltpu.make_async_copy(src_hbm.at[pl.ds(idx_ref[0, 0, r], 1)], dst_ref.at[pl.ds(r, 1)], sem).start()


def _wait_rows(src_hbm, dst_ref, sem, n):
    pltpu.make_async_copy(src_hbm.at[pl.ds(0, n)], dst_ref, sem).wait()


def _moe_expert_kernel(te_ref, cur_ref, nxt_ref, x_hbm, w1_ref, w3_ref, w2_ref, rw_ref, o_ref,
                       xg_ref, sem, w1b_ref, w3b_ref, w2b_ref):
    t = pl.program_id(0)
    nt = pl.num_programs(0)
    tile = xg_ref.shape[1]
    slot = t % 2

    @pl.when(t == 0)
    def _():
        _gather_rows(cur_ref, x_hbm, xg_ref.at[0], sem.at[0], tile)

    _gather_rows(nxt_ref, x_hbm, xg_ref.at[1 - slot], sem.at[1 - slot], tile)

    @pl.when((t == 0) | (te_ref[t] != te_ref[jnp.maximum(t - 1, 0)]))
    def _():
        w1b_ref[...] = w1_ref[...].astype(BF16)
        w3b_ref[...] = w3_ref[...].astype(BF16)
        w2b_ref[...] = w2_ref[...].astype(BF16)

    _wait_rows(x_hbm, xg_ref.at[slot], sem.at[slot], tile)
    xb = xg_ref[slot].astype(BF16)
    h1 = jnp.dot(xb, w1b_ref[...], preferred_element_type=F32)
    h3 = jnp.dot(xb, w3b_ref[...], preferred_element_type=F32)
    h = h1 * jax.nn.sigmoid(h1) * h3 * rw_ref[...]
    o_ref[...] = jnp.dot(h.astype(BF16), w2b_ref[...], preferred_element_type=F32)

    @pl.when(t == nt - 1)
    def _():
        _wait_rows(x_hbm, xg_ref.at[1 - slot], sem.at[1 - slot], tile)


def _moe_experts(x, w1, w3, w2, tile_expert, src_tiles, row_w):
    n, d = x.shape
    _, _, de = w1.shape
    nt = src_tiles.shape[0] - 1
    tile = src_tiles.shape[2]
    grid_spec = pltpu.PrefetchScalarGridSpec(
        num_scalar_prefetch=1,
        grid=(nt,),
        in_specs=[pl.BlockSpec((1, 1, tile), lambda t, te: (t, 0, 0), memory_space=pltpu.SMEM),
                  pl.BlockSpec((1, 1, tile), lambda t, te: (t + 1, 0, 0), memory_space=pltpu.SMEM),
                  pl.BlockSpec(memory_space=pl.ANY),
                  pl.BlockSpec((None, d, de), lambda t, te: (te[t], 0, 0)),
                  pl.BlockSpec((None, d, de), lambda t, te: (te[t], 0, 0)),
                  pl.BlockSpec((None, de, d), lambda t, te: (te[t], 0, 0)),
                  pl.BlockSpec((tile, 1), lambda t, te: (t, 0))],
        out_specs=pl.BlockSpec((tile, d), lambda t, te: (t, 0)),
        scratch_shapes=[pltpu.VMEM((2, tile, d), F32),
                        pltpu.SemaphoreType.DMA((2,)),
                        pltpu.VMEM((d, de), BF16),
                        pltpu.VMEM((d, de), BF16),
                        pltpu.VMEM((de, d), BF16)],
    )
    return pl.pallas_call(
        _moe_expert_kernel,
        grid_spec=grid_spec,
        out_shape=jax.ShapeDtypeStruct((nt * tile, d), F32),
        compiler_params=_cp("arbitrary"),
        name="moe_experts",
    )(tile_expert, src_tiles, src_tiles, x, w1, w3, w2, row_w)


def _moe_combine_kernel(cur_ref, nxt_ref, y_hbm, x_ref, g_ref, b_ref, of_ref, ob_ref, yg_ref, sem, *, alpha):
    t = pl.program_id(0)
    nt = pl.num_programs(0)
    tm = x_ref.shape[0]
    slot = t % 2

    @pl.when(t == 0)
    def _():
        _gather_rows(cur_ref, y_hbm, yg_ref.at[0], sem.at[0], 2 * tm)

    _gather_rows(nxt_ref, y_hbm, yg_ref.at[1 - slot], sem.at[1 - slot], 2 * tm)
    _wait_rows(y_hbm, yg_ref.at[slot], sem.at[slot], 2 * tm)
    moe = yg_ref[slot, 0:tm] + yg_ref[slot, tm:2 * tm]
    y = _layer_norm(alpha * x_ref[...] + moe, g_ref[...], b_ref[...])
    of_ref[...] = y
    ob_ref[...] = y.astype(BF16)

    @pl.when(t == nt - 1)
    def _():
        _wait_rows(y_hbm, yg_ref.at[1 - slot], sem.at[1 - slot], 2 * tm)


def _moe_combine_ln(y_sorted, pos_tiles, x, g, b, alpha, tm):
    m, d = x.shape
    kern = functools.partial(_moe_combine_kernel, alpha=alpha)
    return pl.pallas_call(
        kern,
        grid=(m // tm,),
        in_specs=[pl.BlockSpec((1, 1, 2 * tm), lambda t: (t, 0, 0), memory_space=pltpu.SMEM),
                  pl.BlockSpec((1, 1, 2 * tm), lambda t: (t + 1, 0, 0), memory_space=pltpu.SMEM),
                  pl.BlockSpec(memory_space=pl.ANY),
                  pl.BlockSpec((tm, d), lambda t: (t, 0)),
                  pl.BlockSpec((1, d), lambda t: (0, 0)),
                  pl.BlockSpec((1, d), lambda t: (0, 0))],
        out_specs=[pl.BlockSpec((tm, d), lambda t: (t, 0)), pl.BlockSpec((tm, d), lambda t: (t, 0))],
        out_shape=[jax.ShapeDtypeStruct((m, d), F32), jax.ShapeDtypeStruct((m, d), BF16)],
        scratch_shapes=[pltpu.VMEM((2, 2 * tm, d), F32), pltpu.SemaphoreType.DMA((2,))],
        compiler_params=_cp("arbitrary"),
        name="moe_combine_ln2",
    )(pos_tiles, pos_tiles, y_sorted, x, g.reshape(1, -1), b.reshape(1, -1))


def _moe_layout(route, tile, tm):
    n = route.shape[0]
    e_flat = route[:, 0:2].astype(I32).reshape(-1)
    w_flat = route[:, 2:4].reshape(-1)
    npair = 2 * n
    nt = -(-npair // tile) + N_EXPERTS
    onehot = (e_flat[:, None] == jnp.arange(N_EXPERTS, dtype=I32)[None, :]).astype(I32)
    csum = jnp.cumsum(onehot, axis=0)
    counts = csum[-1]
    rank = jnp.sum((csum - onehot) * onehot, axis=1)
    padded = (counts + tile - 1) // tile * tile
    ends = jnp.cumsum(padded)
    starts = ends - padded
    pos = starts[e_flat] + rank
    rows = nt * tile
    src = jnp.zeros((rows,), I32).at[pos].set(jnp.arange(npair, dtype=I32) // 2, unique_indices=True)
    row_w = jnp.zeros((rows,), F32).at[pos].set(w_flat, unique_indices=True)
    tile_expert = jnp.minimum(jnp.searchsorted(ends, jnp.arange(nt, dtype=I32) * tile, side='right'),
                              N_EXPERTS - 1).astype(I32)
    src_tiles = jnp.concatenate([src, jnp.zeros((tile,), I32)]).reshape(nt + 1, 1, tile)
    pos_t = pos.reshape(n // tm, tm, 2).transpose(0, 2, 1).reshape(n // tm, 1, 2 * tm)
    pos_tiles = jnp.concatenate([pos_t, jnp.zeros((1, 1, 2 * tm), I32)], axis=0)
    return tile_expert, src_tiles, row_w.reshape(rows, 1), pos_tiles


def kernel(x_prompt, x_sample, cache_k, cache_v, cache_idx_k, state_ret, state_conv, w_in, b_gate, w_att_o, ret_gn_g,
           w_ret_o, conv_dw, conv_dw_b, conv_ln_g, conv_ln_b, w_conv_o, w_out, ln1_g, ln1_b, w_router, b_router,
           moe_w1, moe_w3, moe_w2, ln2_g, ln2_b):
    nbp, tp, d = x_prompt.shape
    nbs, ts, _ = x_sample.shape
    depth = w_in.shape[0]
    past = cache_k.shape[2]
    np_, ns = nbp * tp, nbs * ts
    n = np_ + ns
    alpha = (2 * depth) ** 0.25
    kvw = KV_HEADS * ATT_HD

    tm = _pow2_tile(n, 512)
    tq_p = _pow2_tile(tp, 256)
    cl_p = _pow2_tile(tp, 256)
    tt_p = _pow2_tile(tp, 256)
    ls_true = past + ts
    ls_pad = -(-ls_true // (2 * LANES)) * (2 * LANES)
    assert tp % (2 * LANES) == 0 and np_ % max(tq_p, cl_p, tt_p, ts) == 0

    x = jnp.concatenate([x_prompt.reshape(np_, d), x_sample.reshape(ns, d)], axis=0)
    xb = x.astype(BF16)

    c1 = ATT_HEADS * ATT_HD
    c2 = c1 + F32_USED
    ks, vs, iks, rps, cps, kss, vss, ikss, rss, css = ([] for _ in range(10))
    zero_ret = jnp.zeros((nbp, RET_HEADS, RET_DK, RET_DV), F32)
    zero_conv = jnp.zeros((nbp, CONV_W - 1, CONV_CH), F32)

    for l in range(depth):
        w_l = w_in[l]
        w_bf = jnp.concatenate([w_l[:, :c1], w_l[:, c2:]], axis=1).astype(BF16)
        w_f = jnp.pad(w_l[:, c1:c2], ((0, 0), (0, F32_WIDTH - F32_USED))).astype(BF16)
        p_bf = _matmul(xb, w_bf, BF16, tm, 1024)
        p_f = _matmul(xb, w_f, F32, tm, F32_WIDTH)

        k_new = p_f[:, KA_OFF:KA_OFF + kvw]
        v_new = p_f[:, VA_OFF:VA_OFF + kvw]
        ik_new = p_f[:, KI_OFF:KI_OFF + IDX_DIM]
        wi_t = p_f[:, WI_OFF:WI_OFF + IDX_HEADS].T
        ks.append(k_new[:np_].reshape(nbp, tp, KV_HEADS, ATT_HD))
        vs.append(v_new[:np_].reshape(nbp, tp, KV_HEADS, ATT_HD))
        iks.append(ik_new[:np_].reshape(nbp, tp, IDX_DIM))
        kss.append(k_new[np_:].reshape(nbs, ts, KV_HEADS, ATT_HD))
        vss.append(v_new[np_:].reshape(nbs, ts, KV_HEADS, ATT_HD))
        ikss.append(ik_new[np_:].reshape(nbs, ts, IDX_DIM))

        nq = tp // tq_p
        hq = ATT_HEADS * ATT_HD
        att_p = _attention(
            p_bf, lambda tq: ((tq, hq), lambda b, j: (b * nq + j, 0)),
            p_f, lambda lp: ((lp, kvw), lambda b, j: (b, KA_OFF // kvw)),
            p_f, lambda lp: ((lp, kvw), lambda b, j: (b, VA_OFF // kvw)),
            p_f, lambda tq: ((tq, IDX_HEADS * IDX_DIM), lambda b, j: (b * nq + j, QI_OFF // (IDX_HEADS * IDX_DIM))),
            p_f, lambda lp: ((lp, LANES), lambda b, j: (b, KI_OFF // LANES)),
            wi_t[:, :np_], lambda tq: ((IDX_HEADS, tq), lambda b, j: (0, b * nq + j)),
            nb=nbp, t=tp, l_pad=tp, l_true=tp, offset=0, tq=tq_p)

        padk = ls_pad - ls_true
        k_all = jnp.concatenate([cache_k[l].reshape(nbs, past, kvw), k_new[np_:].reshape(nbs, ts, kvw),
                                 jnp.zeros((nbs, padk, kvw), F32)], axis=1)
        v_all = jnp.concatenate([cache_v[l].reshape(nbs, past, kvw), v_new[np_:].reshape(nbs, ts, kvw),
                                 jnp.zeros((nbs, padk, kvw), F32)], axis=1)
        ik_all = jnp.concatenate([cache_idx_k[l], ik_new[np_:].reshape(nbs, ts, IDX_DIM),
                                  jnp.zeros((nbs, padk, IDX_DIM), F32)], axis=1)
        ik_all = jnp.pad(ik_all, ((0, 0), (0, 0), (0, LANES - IDX_DIM)))
        sb = np_ // ts
        att_s = _attention(
            p_bf, lambda tq: ((tq, hq), lambda b, j: (sb + b, 0)),
            k_all, lambda lp: ((None, lp, kvw), lambda b, j: (b, 0, 0)),
            v_all, lambda lp: ((None, lp, kvw), lambda b, j: (b, 0, 0)),
            p_f, lambda tq: ((tq, IDX_HEADS * IDX_DIM), lambda b, j: (sb + b, QI_OFF // (IDX_HEADS * IDX_DIM))),
            ik_all, lambda lp: ((None, lp, LANES), lambda b, j: (b, 0, 0)),
            wi_t[:, np_:].reshape(IDX_HEADS, nbs, ts).transpose(1, 0, 2),
            lambda tq: ((None, IDX_HEADS, tq), lambda b, j: (b, 0, 0)),
            nb=nbs, t=ts, l_pad=ls_pad, l_true=ls_true, offset=past, tq=ts)
        att = jnp.concatenate([att_p, att_s], axis=0)

        ret_p, rs_p = _retention(p_bf, 0, nbp, tp, cl_p, 0, zero_ret, ret_gn_g[l])
        ret_s, rs_s = _retention(p_bf, np_, nbs, ts, ts, past, state_ret[l], ret_gn_g[l])
        ret = jnp.concatenate([ret_p, ret_s], axis=0)
        rps.append(rs_p)
        rss.append(rs_s)

        cnv_p, cs_p = _conv_module(p_bf, 0, nbp, tp, tt_p, zero_conv, conv_dw[l], conv_dw_b[l], conv_ln_g[l], conv_ln_b[l])
        cnv_s, cs_s = _conv_module(p_bf, np_, nbs, ts, ts, state_conv[l], conv_dw[l], conv_dw_b[l], conv_ln_g[l], conv_ln_b[l])
        cnv = jnp.concatenate([cnv_p, cnv_s], axis=0)
        cps.append(cs_p)
        css.append(cs_s)

        z = _merge(att, ret, cnv, p_bf, b_gate[l], w_att_o[l].astype(BF16), w_ret_o[l].astype(BF16),
                   w_conv_o[l].astype(BF16), tm, 512)
        x, xb = _outproj_ln(z, w_out[l].astype(BF16), x, ln1_g[l], ln1_b[l], alpha, _pow2_tile(n, 256))
        route = _router(x, w_router, b_router, _pow2_tile(n, 256))
        tm_c = _pow2_tile(n, 256)
        tile_expert, src_tiles, row_w, pos_tiles = _moe_layout(route, MOE_TILE, tm_c)
        y_sorted = _moe_experts(x, moe_w1[l], moe_w3[l], moe_w2[l], tile_expert, src_tiles, row_w)
        x, xb = _moe_combine_ln(y_sorted, pos_tiles, x, ln2_g[l], ln2_b[l], alpha, tm_c)

    y_prompt = x[:np_].reshape(nbp, tp, d)
    y_sample = x[np_:].reshape(nbs, ts, d)
    st = jnp.stack
    return (y_prompt, y_sample, st(ks), st(vs), st(iks), st(rps), st(cps), st(kss), st(vss), st(ikss), st(rss), st(css))
```

```python
import functools

import jax
import jax.numpy as jnp
from jax import lax
from jax.experimental import pallas as pl
from jax.experimental.pallas import tpu as pltpu

F32 = jnp.float32
BF16 = jnp.bfloat16
I32 = jnp.int32

CHUNK = 64
TOPK_MAX = 256
ATT_HEADS, ATT_HD, KV_HEADS = 8, 128, 2
KV_GROUP = ATT_HEADS // KV_HEADS
IDX_HEADS, IDX_DIM = 8, 64
RET_HEADS, RET_DK, RET_DV = 8, 128, 256
ROPE_BASE = 10000.0
CONV_CH, CONV_W = 1024, 31
N_EXPERTS, N_GROUPS = 16, 4
EXP_PER_GROUP = N_EXPERTS // N_GROUPS
LN_EPS = 1e-5

LANES = 128
SUBLANES = 8

INT_MIN = -2 ** 31
NEG_BIG = -1e30

QR_OFF = 0
KR_OFF = QR_OFF + RET_HEADS * RET_DK
VR_OFF = KR_OFF + RET_HEADS * RET_DK
GR_OFF = VR_OFF + RET_HEADS * RET_DV
CIN_OFF = GR_OFF + RET_HEADS * RET_DV
GATE_OFF = CIN_OFF + 2 * CONV_CH
KA_OFF = 0
VA_OFF = KA_OFF + KV_HEADS * ATT_HD
QI_OFF = VA_OFF + KV_HEADS * ATT_HD
KI_OFF = QI_OFF + IDX_HEADS * IDX_DIM
WI_OFF = KI_OFF + IDX_DIM
F32_USED = WI_OFF + IDX_HEADS
F32_WIDTH = -(-F32_USED // LANES) * LANES

VMEM_LIMIT = 56 * 1024 * 1024


def _cp(*sem):
    return pltpu.CompilerParams(dimension_semantics=sem, vmem_limit_bytes=VMEM_LIMIT)


def _pow2_tile(n, pref):
    t = pref
    while n % t:
        t //= 2
    return t


def _layer_norm(x, g, b):
    mu = jnp.mean(x, axis=-1, keepdims=True)
    xc = x - mu
    var = jnp.mean(xc * xc, axis=-1, keepdims=True)
    return xc * lax.rsqrt(var + LN_EPS) * g + b


def _mm_kernel(x_ref, w_ref, o_ref):
    o_ref[...] = jnp.dot(x_ref[...], w_ref[...], preferred_element_type=F32).astype(o_ref.dtype)


def _matmul(x, w, out_dtype, tm, tn):
    m, k = x.shape
    n = w.shape[1]
    return pl.pallas_call(
        _mm_kernel,
        grid=(n // tn, m // tm),
        in_specs=[pl.BlockSpec((tm, k), lambda j, i: (i, 0)),
                  pl.BlockSpec((k, tn), lambda j, i: (0, j))],
        out_specs=pl.BlockSpec((tm, tn), lambda j, i: (i, j)),
        out_shape=jax.ShapeDtypeStruct((m, n), out_dtype),
        compiler_params=_cp("arbitrary", "arbitrary"),
        name="in_proj",
    )(x, w)


def _attn_kernel(q_ref, k_ref, v_ref, qi_ref, ki_ref, wi_ref, o_ref,
                 key_ref, lg_ref, mx_ref, acc_ref, m_ref,
                 *, tq, nq, ck, nck, offset, l_true, k_sel, tie_bits):
    j = pl.program_id(1)
    nsub = ck // LANES
    qlane = lax.broadcasted_iota(I32, (1, tq), 1)
    krow = lax.broadcasted_iota(I32, (ck, tq), 0)
    pos = offset + j * tq + qlane
    limit = jnp.minimum((pos // CHUNK + 1) * CHUNK, l_true)
    if nq == 1:
        n_act = (min(((offset + tq - 1) // CHUNK + 1) * CHUNK, l_true) + ck - 1) // ck
    else:
        last_pos = offset + (j + 1) * tq - 1
        max_limit = jnp.minimum((last_pos // CHUNK + 1) * CHUNK, l_true)
        n_act = (max_limit + ck - 1) // ck

    qi = (qi_ref[...] * (IDX_DIM ** -0.5)).astype(BF16)
    qis = [qi[:, h * IDX_DIM:(h + 1) * IDX_DIM] for h in range(IDX_HEADS)]
    stacked = tq % LANES == 0
    if stacked:
        qstack = jnp.concatenate(qis, axis=0)
    wt = wi_ref[...] * (IDX_HEADS ** -0.5)
    wrows = [wt[h:h + 1, :] for h in range(IDX_HEADS)]
    nt = (((1,), (1,)), ((), ()))

    def score_body(c, carry):
        for u in range(nsub):
            r0 = pl.multiple_of(c * ck + u * LANES, LANES)
            kiu = ki_ref[pl.ds(r0, LANES), :][:, :IDX_DIM].astype(BF16)
            if stacked:
                d_all = lax.dot_general(kiu, qstack, nt, preferred_element_type=F32)
                ds = [d_all[:, h * tq:(h + 1) * tq] for h in range(IDX_HEADS)]
            else:
                ds = [lax.dot_general(kiu, qis[h], nt, preferred_element_type=F32) for h in range(IDX_HEADS)]
            s = jnp.zeros((LANES, tq), F32)
            for h in range(IDX_HEADS):
                s = s + jnp.maximum(ds[h], 0.0) * wrows[h]
            bits = lax.bitcast_convert_type(s, I32)
            bits = jnp.where(bits == INT_MIN, 0, bits)
            key = jnp.where(bits >= 0, bits, bits ^ 0x7FFFFFFF)
            kpos = r0 + lax.broadcasted_iota(I32, (LANES, tq), 0)
            key_ref[c, u * LANES:(u + 1) * LANES, :] = jnp.where(kpos < limit, key, INT_MIN)
        return carry

    lax.fori_loop(0, n_act, score_body, 0)

    def count(pred):
        def body(c, acc):
            p = jnp.where(pred(key_ref[c], c * ck + krow), 1.0, 0.0)
            parts = [p[r * SUBLANES:(r + 1) * SUBLANES] for r in range(ck // SUBLANES)]
            while len(parts) > 1:
                parts = [a + b for a, b in zip(parts[0::2], parts[1::2])]
            return acc + parts[0]
        acc = lax.fori_loop(0, n_act, body, jnp.zeros((SUBLANES, tq), F32))
        return jnp.sum(acc, axis=0, keepdims=True)

    def bisect(i, tu):
        cand_u = tu | (jnp.int32(1) << (31 - i))
        cand_s = cand_u ^ INT_MIN
        cnt = count(lambda key, idx: key >= cand_s)
        return jnp.where(cnt >= k_sel, cand_u, tu)

    tu = lax.fori_loop(0, 32, bisect, jnp.zeros((1, tq), I32))
    ts = tu ^ INT_MIN
    c_gt = count(lambda key, idx: key > ts)
    c_ge = count(lambda key, idx: key >= ts)
    want = k_sel - c_gt

    eye = jnp.where(lax.broadcasted_iota(I32, (tq, tq), 0) == lax.broadcasted_iota(I32, (tq, tq), 1), 1.0, 0.0).astype(BF16)
    ts_adm = jnp.maximum(ts, INT_MIN + 1)
    m_ref[...] = jnp.full((1, tq), 2 ** 30, I32)

    @pl.when(jnp.max(c_ge) > k_sel)
    def _():
        def tie_bisect(i, m):
            cand = m | (jnp.int32(1) << (tie_bits - 1 - i))
            cnt = count(lambda key, idx: (key == ts) & (idx < cand))
            return jnp.where(cnt < want, cand, m)
        m_ref[...] = lax.fori_loop(0, tie_bits, tie_bisect, jnp.zeros((1, tq), I32))

    m_last = m_ref[...]

    qscale = (ATT_HD ** -0.5) * 1.4426950408889634
    groups = range(KV_HEADS)
    qss = []
    for g in groups:
        qs = jnp.concatenate(
            [q_ref[:, (g * KV_GROUP + r) * ATT_HD:(g * KV_GROUP + r + 1) * ATT_HD] for r in range(KV_GROUP)], axis=0)
        qss.append((qs.astype(F32) * qscale).astype(BF16))
    mx_ref[...] = jnp.full(mx_ref.shape, NEG_BIG, F32)

    def logit_body(c, carry):
        r0 = pl.multiple_of(c * ck, ck)
        key = key_ref[c]
        sel = (key > ts_adm) | ((key == ts_adm) & (c * ck + krow <= m_last))
        sel_t = jnp.where(sel, 1.0, 0.0).astype(BF16)
        sel_q = lax.dot_general(eye, sel_t, nt, preferred_element_type=F32)
        b = (sel_q - 1.0) * (-NEG_BIG)
        b4 = jnp.concatenate([b] * KV_GROUP, axis=0)
        for g in groups:
            kc = k_ref[pl.ds(r0, ck), g * ATT_HD:(g + 1) * ATT_HD].astype(BF16)
            lg = lax.dot_general(qss[g], kc, nt, preferred_element_type=F32) + b4
            lg_ref[g, c] = lg
            m = mx_ref[g]
            for u in range(nsub):
                m = jnp.maximum(m, lg[:, u * LANES:(u + 1) * LANES])
            mx_ref[g] = m
        return carry

    lax.fori_loop(0, n_act, logit_body, 0)
    m_rows = [jnp.max(mx_ref[g], axis=-1, keepdims=True) for g in groups]
    acc_ref[...] = jnp.zeros(acc_ref.shape, F32)
    ones_cols = jnp.ones((ck, ATT_HD), BF16)

    def pv_body(c, carry):
        r0 = pl.multiple_of(c * ck, ck)
        for g in groups:
            p = jnp.exp2(lg_ref[g, c] - m_rows[g]).astype(BF16)
            vc = v_ref[pl.ds(r0, ck), g * ATT_HD:(g + 1) * ATT_HD].astype(BF16)
            acc_ref[g] += jnp.dot(p, jnp.concatenate([vc, ones_cols], axis=1), preferred_element_type=F32)
        return carry

    lax.fori_loop(0, n_act, pv_body, 0)
    for g in groups:
        a = acc_ref[g]
        o = a[:, :ATT_HD] / a[:, ATT_HD:]
        for r in range(KV_GROUP):
            h = g * KV_GROUP + r
            o_ref[:, h * ATT_HD:(h + 1) * ATT_HD] = o[r * tq:(r + 1) * tq].astype(o_ref.dtype)


def _attention(q_arr, q_map, k_arr, k_map, v_arr, v_map, qi_arr, qi_map, ki_arr, ki_map, wi_arr, wi_map,
               *, nb, t, l_pad, l_true, offset, tq):
    ck = 2 * LANES
    nck = l_pad // ck
    assert l_pad % ck == 0 and t % tq == 0 and nck * (ck // LANES) < 256
    nq = t // tq
    k_sel = min(TOPK_MAX, l_true // 4)
    tie_bits = max(1, (l_pad - 1).bit_length())
    kern = functools.partial(_attn_kernel, tq=tq, nq=nq, ck=ck, nck=nck, offset=offset, l_true=l_true,
                             k_sel=k_sel, tie_bits=tie_bits)
    hq = ATT_HEADS * ATT_HD
    return pl.pallas_call(
        kern,
        grid=(nb, nq),
        in_specs=[pl.BlockSpec(*q_map(tq)), pl.BlockSpec(*k_map(l_pad)), pl.BlockSpec(*v_map(l_pad)),
                  pl.BlockSpec(*qi_map(tq)), pl.BlockSpec(*ki_map(l_pad)), pl.BlockSpec(*wi_map(tq))],
        out_specs=pl.BlockSpec((tq, hq), lambda b, j: (b * nq + j, 0)),
        out_shape=jax.ShapeDtypeStruct((nb * t, hq), BF16),
        scratch_shapes=[pltpu.VMEM((nck, ck, tq), I32),
                        pltpu.VMEM((KV_HEADS, nck, KV_GROUP * tq, ck), F32),
                        pltpu.VMEM((KV_HEADS, KV_GROUP * tq, LANES), F32),
                        pltpu.VMEM((KV_HEADS, KV_GROUP * tq, 2 * ATT_HD), F32),
                        pltpu.VMEM((1, tq), I32)],
        compiler_params=_cp("arbitrary", "arbitrary"),
        name="dsa_attention",
    )(q_arr, k_arr, v_arr, qi_arr, ki_arr, wi_arr)


def _order_key(s):
    bits = lax.bitcast_convert_type(s, I32)
    bits = jnp.where(bits == INT_MIN, 0, bits)
    return jnp.where(bits >= 0, bits, bits ^ 0x7FFFFFFF)


def _sample_select_kernel(kic_ref, new_ref, qi_ref, wi_ref, mask_ref, key_ref, keyn_ref, m_ref,
                          *, ts, gs, past, ck, k_sel, tie_bits):
    nck = past // ck
    nsub = ck // LANES
    l_true = past + ts
    lane = lax.broadcasted_iota(I32, (1, LANES), 1)
    lane_s = lane // ts
    pos = past + lane % ts
    limit = jnp.minimum((pos // CHUNK + 1) * CHUNK, l_true)
    nt = (((1,), (1,)), ((), ()))

    qi = (qi_ref[...] * (IDX_DIM ** -0.5)).astype(BF16)
    qstack = jnp.concatenate([qi[:, h * IDX_DIM:(h + 1) * IDX_DIM] for h in range(IDX_HEADS)], axis=0)
    wt = wi_ref[...] * (IDX_HEADS ** -0.5)
    wrows = [wt[h:h + 1, :] for h in range(IDX_HEADS)]

    def own_stream(d, rows):
        out = d[(gs - 1) * rows:gs * rows]
        for s in range(gs - 2, -1, -1):
            out = jnp.where(lane_s == s, d[s * rows:(s + 1) * rows], out)
        return out

    def scores(kmat, rows):
        d_all = lax.dot_general(kmat, qstack, nt, preferred_element_type=F32)
        s = jnp.zeros((rows, LANES), F32)
        for h in range(IDX_HEADS):
            s = s + jnp.maximum(own_stream(d_all[:, h * LANES:(h + 1) * LANES], rows), 0.0) * wrows[h]
        return _order_key(s)

    def score_body(c, carry):
        for u in range(nsub):
            r0 = pl.multiple_of(c * ck + u * LANES, LANES)
            kmat = jnp.concatenate([kic_ref[s, pl.ds(r0, LANES), :] for s in range(gs)], axis=0).astype(BF16)
            kpos = r0 + lax.broadcasted_iota(I32, (LANES, LANES), 0)
            key_ref[c, u * LANES:(u + 1) * LANES, :] = jnp.where(kpos < limit, scores(kmat, LANES), INT_MIN)
        return carry

    lax.fori_loop(0, nck, score_body, 0)
    kn = new_ref[:, :IDX_DIM].astype(BF16)
    npos = past + lax.broadcasted_iota(I32, (ts, LANES), 0)
    keyn_ref[...] = jnp.where(npos < limit, scores(kn, ts), INT_MIN)

    krow = lax.broadcasted_iota(I32, (ck, LANES), 0)

    def tree(p, rows):
        parts = [p[r * SUBLANES:(r + 1) * SUBLANES] for r in range(rows // SUBLANES)]
        while len(parts) > 1:
            parts = [a + b for a, b in zip(parts[0::2], parts[1::2])]
        return parts[0]

    def count(pred):
        def body(c, acc):
            p = jnp.where(pred(key_ref[c], c * ck + krow), 1.0, 0.0)
            return acc + tree(p, ck)
        acc = lax.fori_loop(0, nck, body, jnp.zeros((SUBLANES, LANES), F32))
        acc = acc + tree(jnp.where(pred(keyn_ref[...], npos), 1.0, 0.0), ts)
        return jnp.sum(acc, axis=0, keepdims=True)

    def bisect(i, tu):
        cand_u = tu | (jnp.int32(1) << (31 - i))
        cand_s = cand_u ^ INT_MIN
        cnt = count(lambda key, idx: key >= cand_s)
        return jnp.where(cnt >= k_sel, cand_u, tu)

    tu = lax.fori_loop(0, 32, bisect, jnp.zeros((1, LANES), I32))
    ts_ = tu ^ INT_MIN
    c_gt = count(lambda key, idx: key > ts_)
    c_ge = count(lambda key, idx: key >= ts_)
    want = k_sel - c_gt
    m_ref[...] = jnp.full((1, LANES), 2 ** 30, I32)

    @pl.when(jnp.max(c_ge) > k_sel)
    def _():
        def tie_bisect(i, m):
            cand = m | (jnp.int32(1) << (tie_bits - 1 - i))
            cnt = count(lambda key, idx: (key == ts_) & (idx < cand))
            return jnp.where(cnt < want, cand, m)
        m_ref[...] = lax.fori_loop(0, tie_bits, tie_bisect, jnp.zeros((1, LANES), I32))

    m_last = m_ref[...]
    ts_adm = jnp.maximum(ts_, INT_MIN + 1)
    eye = jnp.where(lax.broadcasted_iota(I32, (LANES, LANES), 0) == lax.broadcasted_iota(I32, (LANES, LANES), 1),
                    1.0, 0.0).astype(BF16)

    def selected(key, idx):
        sel = (key > ts_adm) | ((key == ts_adm) & (idx <= m_last))
        return jnp.where(sel, 1.0, 0.0).astype(BF16)

    for c in range(nck):
        sel_q = lax.dot_general(eye, selected(key_ref[c], c * ck + krow), nt, preferred_element_type=F32)
        mask_ref[:, c * ck:(c + 1) * ck] = sel_q.astype(BF16)
    sel_n = lax.dot_general(eye, selected(keyn_ref[...], npos), nt, preferred_element_type=F32)
    mask_ref[:, past:past + LANES] = jnp.concatenate(
        [sel_n, jnp.zeros((LANES, LANES - ts), F32)], axis=1).astype(BF16)


def _sample_select(cache_ik, p_f, wi_t, np_, nbs, ts, past):
    gs = LANES // ts
    ck = 2 * LANES
    assert LANES % ts == 0 and nbs % gs == 0 and past % ck == 0 and np_ % LANES == 0 and ts % SUBLANES == 0
    l_true = past + ts
    k_sel = min(TOPK_MAX, l_true // 4)
    kern = functools.partial(_sample_select_kernel, ts=ts, gs=gs, past=past, ck=ck, k_sel=k_sel,
                             tie_bits=max(1, (l_true - 1).bit_length()))
    rb = np_ // LANES
    return pl.pallas_call(
        kern,
        grid=(nbs // gs,),
        in_specs=[pl.BlockSpec((gs, past, IDX_DIM), lambda g: (g, 0, 0)),
                  pl.BlockSpec((LANES, LANES), lambda g: (rb + g, KI_OFF // LANES)),
                  pl.BlockSpec((LANES, IDX_HEADS * IDX_DIM), lambda g: (rb + g, QI_OFF // (IDX_HEADS * IDX_DIM))),
                  pl.BlockSpec((IDX_HEADS, LANES), lambda g: (0, rb + g))],
        out_specs=pl.BlockSpec((LANES, past + LANES), lambda g: (g, 0)),
        out_shape=jax.ShapeDtypeStruct((nbs * ts, past + LANES), BF16),
        scratch_shapes=[pltpu.VMEM((past // ck, ck, LANES), I32),
                        pltpu.VMEM((ts, LANES), I32),
                        pltpu.VMEM((1, LANES), I32)],
        compiler_params=_cp("arbitrary"),
        name="sample_select",
    )(cache_ik, p_f, p_f, wi_t)


def _sample_attend_kernel(q_ref, kc_ref, vc_ref, kn_ref, vn_ref, mask_ref, o_ref, lg_ref, *, ts, past, ca):
    nt = (((1,), (1,)), ((), ()))
    qscale = (ATT_HD ** -0.5) * 1.4426950408889634
    nca = past // ca
    rows = KV_GROUP * ts

    def bias_of(m):
        b = (m.astype(F32) - 1.0) * (-NEG_BIG)
        return jnp.concatenate([b] * KV_GROUP, axis=0)

    bias_n = bias_of(mask_ref[:, past:past + ts])
    for g in range(KV_HEADS):
        qs = jnp.concatenate(
            [q_ref[:, (g * KV_GROUP + r) * ATT_HD:(g * KV_GROUP + r + 1) * ATT_HD] for r in range(KV_GROUP)], axis=0)
        qs = (qs.astype(F32) * qscale).astype(BF16)
        cols = slice(g * ATT_HD, (g + 1) * ATT_HD)
        lg_n = lax.dot_general(qs, kn_ref[:, cols].astype(BF16), nt, preferred_element_type=F32) + bias_n
        m_run = jnp.full((rows, LANES), NEG_BIG, F32)
        for c in range(nca):
            kc = kc_ref[c * ca:(c + 1) * ca, cols].astype(BF16)
            lg = lax.dot_general(qs, kc, nt, preferred_element_type=F32) + bias_of(mask_ref[:, c * ca:(c + 1) * ca])
            lg_ref[g, :, c * ca:(c + 1) * ca] = lg
            for u in range(ca // LANES):
                m_run = jnp.maximum(m_run, lg[:, u * LANES:(u + 1) * LANES])
        m_row = jnp.maximum(jnp.max(m_run, axis=-1, keepdims=True), jnp.max(lg_n, axis=-1, keepdims=True))
        ones_n = jnp.ones((ts, ATT_HD), BF16)
        acc = jnp.dot(jnp.exp2(lg_n - m_row).astype(BF16),
                      jnp.concatenate([vn_ref[:, cols].astype(BF16), ones_n], axis=1), preferred_element_type=F32)
        ones_c = jnp.ones((ca, ATT_HD), BF16)
        for c in range(nca):
            p = jnp.exp2(lg_ref[g, :, c * ca:(c + 1) * ca] - m_row).astype(BF16)
            vc = vc_ref[c * ca:(c + 1) * ca, cols].astype(BF16)
            acc = acc + jnp.dot(p, jnp.concatenate([vc, ones_c], axis=1), preferred_element_type=F32)
        o = acc[:, :ATT_HD] / acc[:, ATT_HD:]
        for r in range(KV_GROUP):
            h = g * KV_GROUP + r
            o_ref[:, h * ATT_HD:(h + 1) * ATT_HD] = o[r * ts:(r + 1) * ts].astype(o_ref.dtype)


def _sample_attend(p_bf, qa_blk, cache_k, cache_v, p_f, mask, np_, nbs, ts, past):
    kvw = KV_HEADS * ATT_HD
    hq = ATT_HEADS * ATT_HD
    ca = 4 * LANES
    assert past % ca == 0
    rb = np_ // ts
    kern = functools.partial(_sample_attend_kernel, ts=ts, past=past, ca=ca)
    return pl.pallas_call(
        kern,
        grid=(nbs,),
        in_specs=[pl.BlockSpec((ts, hq), lambda b: (rb + b, qa_blk)),
                  pl.BlockSpec((None, past, kvw), lambda b: (b, 0, 0)),
                  pl.BlockSpec((None, past, kvw), lambda b: (b, 0, 0)),
                  pl.BlockSpec((ts, kvw), lambda b: (rb + b, KA_OFF // kvw)),
                  pl.BlockSpec((ts, kvw), lambda b: (rb + b, VA_OFF // kvw)),
                  pl.BlockSpec((ts, past + LANES), lambda b: (b, 0))],
        out_specs=pl.BlockSpec((ts, hq), lambda b: (b, 0)),
        out_shape=jax.ShapeDtypeStruct((nbs * ts, hq), BF16),
        scratch_shapes=[pltpu.VMEM((KV_HEADS, KV_GROUP * ts, past), F32)],
        compiler_params=_cp("arbitrary"),
        name="sample_attend",
    )(p_bf, cache_k, cache_v, p_f, p_f, mask)


def _ret_kernel(q_ref, k_ref, v_ref, g_ref, cs_ref, sn_ref, dec_ref, xi_ref, zt_ref, gp_ref, gn_ref, s0_ref,
                y_ref, sf_ref, s_ref):
    c = pl.program_id(1)
    dk, dv = RET_DK, RET_DV

    @pl.when(c == 0)
    def _():
        s_ref[...] = s0_ref[...]

    cs = cs_ref[...]
    sn = sn_ref[...]

    def rot(x):
        xf = x.astype(F32)
        return xf * cs + pltpu.roll(xf, dk // 2, 1) * sn

    for h in range(RET_HEADS):
        q = rot(q_ref[:, h * dk:(h + 1) * dk])
        k = rot(k_ref[:, h * dk:(h + 1) * dk]) * (dk ** -0.5)
        qb = q.astype(BF16)
        kb = k.astype(BF16)
        v = v_ref[:, h * dv:(h + 1) * dv]
        inner = lax.dot_general(qb, kb, (((1,), (1,)), ((), ())), preferred_element_type=F32) * dec_ref[h]
        s = s_ref[h]
        o = jnp.dot(inner.astype(BF16), v, preferred_element_type=F32)
        o = o + jnp.dot(qb, s.astype(BF16), preferred_element_type=F32) * xi_ref[h]
        kz = (k * zt_ref[h]).T.astype(BF16)
        s_new = s * gp_ref[h] + jnp.dot(kz, v, preferred_element_type=F32)
        s_ref[h] = s_new
        sf_ref[h] = s_new

        mu = jnp.mean(o, axis=-1, keepdims=True)
        oc = o - mu
        var = jnp.mean(oc * oc, axis=-1, keepdims=True)
        yn = oc * lax.rsqrt(var + LN_EPS) * gn_ref[:, h * dv:(h + 1) * dv]
        gg = g_ref[:, h * dv:(h + 1) * dv].astype(F32)
        y_ref[:, h * dv:(h + 1) * dv] = (gg * jax.nn.sigmoid(gg) * yn).astype(y_ref.dtype)


def _retention(p_bf, row0, nb, t, cl, offset, s0, gn_g):
    nc = t // cl
    h_, dk, dv = RET_HEADS, RET_DK, RET_DV
    half = dk // 2
    pos = (offset + jnp.arange(t, dtype=I32)).astype(F32)
    inv = 1.0 / (ROPE_BASE ** (jnp.arange(half, dtype=F32) / half))
    ang = pos[:, None] * inv[None, :]
    cos, sin = jnp.cos(ang), jnp.sin(ang)
    cs2 = jnp.concatenate([cos, cos], axis=-1)
    sn2 = jnp.concatenate([-sin, sin], axis=-1)
    log_g = jnp.log1p(-jnp.exp2(-5.0 - jnp.arange(h_, dtype=F32)))
    n = jnp.arange(cl, dtype=F32)
    diff = n[:, None] - n[None, :]
    decay = jnp.where(diff >= 0, jnp.exp(log_g[:, None, None] * jnp.maximum(diff, 0.0)), 0.0)
    xi = jnp.exp(log_g[:, None] * (n + 1.0))[..., None]
    zeta = jnp.exp(log_g[:, None] * (cl - 1.0 - n))[..., None]
    gpow = jnp.broadcast_to(jnp.exp(log_g * cl)[:, None, None], (h_, 1, dv))
    rb = row0 // cl
    return pl.pallas_call(
        _ret_kernel,
        grid=(nb, nc),
        in_specs=[pl.BlockSpec((cl, h_ * dk), lambda b, c: (rb + b * nc + c, QR_OFF // (h_ * dk))),
                  pl.BlockSpec((cl, h_ * dk), lambda b, c: (rb + b * nc + c, KR_OFF // (h_ * dk))),
                  pl.BlockSpec((cl, h_ * dv), lambda b, c: (rb + b * nc + c, VR_OFF // (h_ * dv))),
                  pl.BlockSpec((cl, h_ * dv), lambda b, c: (rb + b * nc + c, GR_OFF // (h_ * dv))),
                  pl.BlockSpec((cl, dk), lambda b, c: (c, 0)),
                  pl.BlockSpec((cl, dk), lambda b, c: (c, 0)),
                  pl.BlockSpec((h_, cl, cl), lambda b, c: (0, 0, 0)),
                  pl.BlockSpec((h_, cl, 1), lambda b, c: (0, 0, 0)),
                  pl.BlockSpec((h_, cl, 1), lambda b, c: (0, 0, 0)),
                  pl.BlockSpec((h_, 1, dv), lambda b, c: (0, 0, 0)),
                  pl.BlockSpec((1, h_ * dv), lambda b, c: (0, 0)),
                  pl.BlockSpec((None, h_, dk, dv), lambda b, c: (b, 0, 0, 0))],
        out_specs=[pl.BlockSpec((cl, h_ * dv), lambda b, c: (b * nc + c, 0)),
                   pl.BlockSpec((None, h_, dk, dv), lambda b, c: (b, 0, 0, 0))],
        out_shape=[jax.ShapeDtypeStruct((nb * t, h_ * dv), BF16),
                   jax.ShapeDtypeStruct((nb, h_, dk, dv), F32)],
        scratch_shapes=[pltpu.VMEM((h_, dk, dv), F32)],
        compiler_params=_cp("arbitrary", "arbitrary"),
        name="retention",
    )(p_bf, p_bf, p_bf, p_bf, cs2, sn2, decay, xi, zeta, gpow, gn_g.reshape(1, -1), s0)


CONV_HALO = 32
CONV_ROWS = 16


def _conv_kernel(ca_ref, cb_ref, st_ref, w_ref, b_ref, lg_ref, lb_ref, y_ref, tail_ref, sh_ref, *, tt):
    t = pl.program_id(1)
    ext = tt + CONV_HALO

    @pl.when(t == 0)
    def _():
        sh_ref[0, 0:CONV_HALO] = st_ref[...]

    @pl.when(t > 0)
    def _():
        sh_ref[0, 0:CONV_HALO] = sh_ref[0, tt:ext]

    ca = ca_ref[...].astype(F32)
    cb = cb_ref[...].astype(F32)
    sh_ref[0, CONV_HALO:ext] = ca * jax.nn.sigmoid(cb)
    tail_ref[...] = sh_ref[0, tt:ext]
    for s in range(1, SUBLANES):
        sh_ref[s, 0:ext - SUBLANES] = sh_ref[0, s:ext - SUBLANES + s]

    lead = CONV_HALO - (CONV_W - 1)

    def body(i, carry):
        r0 = pl.multiple_of(i * CONV_ROWS, CONV_ROWS)
        acc = jnp.broadcast_to(b_ref[...], (CONV_ROWS, CONV_CH))
        for k in range(CONV_W):
            a, s = divmod(k + lead, SUBLANES)
            acc = acc + w_ref[k:k + 1, :] * sh_ref[s, pl.ds(r0 + a * SUBLANES, CONV_ROWS), :]
        hn = _layer_norm(acc, lg_ref[...], lb_ref[...])
        y_ref[pl.ds(r0, CONV_ROWS), :] = (hn * jax.nn.sigmoid(hn)).astype(y_ref.dtype)
        return carry

    lax.fori_loop(0, tt // CONV_ROWS, body, 0)


def _conv_module(p_bf, row0, nb, t, tt, state, dw, dw_b, ln_g, ln_b):
    nt = t // tt
    lead = CONV_HALO - (CONV_W - 1)
    st = jnp.pad(state, ((0, 0), (lead, 0), (0, 0)))
    rb = row0 // tt
    kern = functools.partial(_conv_kernel, tt=tt)
    y, tail = pl.pallas_call(
        kern,
        grid=(nb, nt),
        in_specs=[pl.BlockSpec((tt, CONV_CH), lambda b, i: (rb + b * nt + i, CIN_OFF // CONV_CH)),
                  pl.BlockSpec((tt, CONV_CH), lambda b, i: (rb + b * nt + i, CIN_OFF // CONV_CH + 1)),
                  pl.BlockSpec((None, CONV_HALO, CONV_CH), lambda b, i: (b, 0, 0)),
                  pl.BlockSpec((CONV_W, CONV_CH), lambda b, i: (0, 0)),
                  pl.BlockSpec((1, CONV_CH), lambda b, i: (0, 0)),
                  pl.BlockSpec((1, CONV_CH), lambda b, i: (0, 0)),
                  pl.BlockSpec((1, CONV_CH), lambda b, i: (0, 0))],
        out_specs=[pl.BlockSpec((tt, CONV_CH), lambda b, i: (b * nt + i, 0)),
                   pl.BlockSpec((None, CONV_HALO, CONV_CH), lambda b, i: (b, 0, 0))],
        out_shape=[jax.ShapeDtypeStruct((nb * t, CONV_CH), BF16),
                   jax.ShapeDtypeStruct((nb, CONV_HALO, CONV_CH), F32)],
        scratch_shapes=[pltpu.VMEM((SUBLANES, tt + CONV_HALO, CONV_CH), F32)],
        compiler_params=_cp("arbitrary", "arbitrary"),
        name="conv_module",
    )(p_bf, p_bf, st, dw, dw_b.reshape(1, -1), ln_g.reshape(1, -1), ln_b.reshape(1, -1))
    return y, tail[:, lead:, :]


def _merge_kernel(a_ref, r_ref, c_ref, ga_ref, gb_ref, gc_ref, ba_ref, bb_ref, bc_ref, wa_ref, wr_ref, wc_ref, z_ref):
    ya = jnp.dot(a_ref[...], wa_ref[...], preferred_element_type=F32)
    yb = jnp.dot(r_ref[...], wr_ref[...], preferred_element_type=F32)
    yc = jnp.dot(c_ref[...], wc_ref[...], preferred_element_type=F32)
    ga = jax.nn.sigmoid(ga_ref[...].astype(F32) + ba_ref[...])
    gb = jax.nn.sigmoid(gb_ref[...].astype(F32) + bb_ref[...])
    gc = jax.nn.sigmoid(gc_ref[...].astype(F32) + bc_ref[...])
    z_ref[...] = (ga * ya + gb * yb + gc * yc).astype(z_ref.dtype)


def _merge(att, ret, cnv, p_bf, b_gate, wa, wr, wc, tm, tn):
    m = att.shape[0]
    d = wa.shape[1]
    nj = d // tn
    g0 = GATE_OFF // tn
    bg = b_gate.reshape(1, -1)
    return pl.pallas_call(
        _merge_kernel,
        grid=(nj, m // tm),
        in_specs=[pl.BlockSpec((tm, att.shape[1]), lambda j, i: (i, 0)),
                  pl.BlockSpec((tm, ret.shape[1]), lambda j, i: (i, 0)),
                  pl.BlockSpec((tm, cnv.shape[1]), lambda j, i: (i, 0)),
                  pl.BlockSpec((tm, tn), lambda j, i: (i, g0 + j)),
                  pl.BlockSpec((tm, tn), lambda j, i: (i, g0 + nj + j)),
                  pl.BlockSpec((tm, tn), lambda j, i: (i, g0 + 2 * nj + j)),
                  pl.BlockSpec((1, tn), lambda j, i: (0, j)),
                  pl.BlockSpec((1, tn), lambda j, i: (0, nj + j)),
                  pl.BlockSpec((1, tn), lambda j, i: (0, 2 * nj + j)),
                  pl.BlockSpec((wa.shape[0], tn), lambda j, i: (0, j)),
                  pl.BlockSpec((wr.shape[0], tn), lambda j, i: (0, j)),
                  pl.BlockSpec((wc.shape[0], tn), lambda j, i: (0, j))],
        out_specs=pl.BlockSpec((tm, tn), lambda j, i: (i, j)),
        out_shape=jax.ShapeDtypeStruct((m, d), BF16),
        compiler_params=_cp("arbitrary", "arbitrary"),
        name="branch_merge",
    )(att, ret, cnv, p_bf, p_bf, p_bf, bg, bg, bg, wa, wr, wc)


def _outproj_ln_kernel(z_ref, w_ref, x_ref, g_ref, b_ref, of_ref, ob_ref, *, alpha):
    mix = jnp.dot(z_ref[...], w_ref[...], preferred_element_type=F32)
    y = _layer_norm(alpha * x_ref[...] + mix, g_ref[...], b_ref[...])
    of_ref[...] = y
    ob_ref[...] = y.astype(BF16)


def _outproj_ln(z, w_out, x, g, b, alpha, tm):
    m, d = x.shape
    kern = functools.partial(_outproj_ln_kernel, alpha=alpha)
    return pl.pallas_call(
        kern,
        grid=(m // tm,),
        in_specs=[pl.BlockSpec((tm, d), lambda i: (i, 0)),
                  pl.BlockSpec((d, d), lambda i: (0, 0)),
                  pl.BlockSpec((tm, d), lambda i: (i, 0)),
                  pl.BlockSpec((1, d), lambda i: (0, 0)),
                  pl.BlockSpec((1, d), lambda i: (0, 0))],
        out_specs=[pl.BlockSpec((tm, d), lambda i: (i, 0)), pl.BlockSpec((tm, d), lambda i: (i, 0))],
        out_shape=[jax.ShapeDtypeStruct((m, d), F32), jax.ShapeDtypeStruct((m, d), BF16)],
        compiler_params=_cp("arbitrary"),
        name="out_proj_ln1",
    )(z, w_out, x, g.reshape(1, -1), b.reshape(1, -1))


def _router_kernel(x_ref, w_ref, b_ref, gate_ref):
    tm = x_ref.shape[0]
    logits = jnp.dot(x_ref[...], w_ref[...], precision=lax.Precision.HIGHEST, preferred_element_type=F32) + b_ref[...]
    lane = lax.broadcasted_iota(I32, (tm, LANES), 1)
    valid = lane < N_EXPERTS
    lm = jnp.where(valid, logits, NEG_BIG)
    e = jnp.where(valid, jnp.exp(lm - jnp.max(lm, axis=-1, keepdims=True)), 0.0)
    aff = e / jnp.sum(e, axis=-1, keepdims=True)

    def top2(vals):
        v1 = jnp.max(vals, axis=-1, keepdims=True)
        i1 = jnp.min(jnp.where(vals == v1, lane, LANES), axis=-1, keepdims=True)
        rest = jnp.where(lane == i1, -2.0, vals)
        v2 = jnp.max(rest, axis=-1, keepdims=True)
        i2 = jnp.min(jnp.where(rest == v2, lane, LANES), axis=-1, keepdims=True)
        return v1, i1, v2, i2

    grp = lane // EXP_PER_GROUP
    best = jnp.zeros((tm, 1), I32)
    best_score = None
    for g in range(N_GROUPS):
        v1, _, v2, _ = top2(jnp.where(grp == g, aff, -1.0))
        score = v1 + v2
        if g == 0:
            best_score = score
        else:
            better = score > best_score
            best = jnp.where(better, g, best)
            best_score = jnp.where(better, score, best_score)
    v1, i1, v2, i2 = top2(jnp.where(grp == best, aff, -1.0))
    tot = v1 + v2
    gate_ref[...] = (jnp.where(lane == 0, i1.astype(F32), 0.0) + jnp.where(lane == 1, i2.astype(F32), 0.0)
                     + jnp.where(lane == 2, v1 / tot, 0.0) + jnp.where(lane == 3, v2 / tot, 0.0))


def _router(x, w_router, b_router, tm):
    m, d = x.shape
    wr = jnp.pad(w_router, ((0, 0), (0, LANES - N_EXPERTS)))
    br = jnp.pad(b_router, (0, LANES - N_EXPERTS)).reshape(1, -1)
    return pl.pallas_call(
        _router_kernel,
        grid=(m // tm,),
        in_specs=[pl.BlockSpec((tm, d), lambda i: (i, 0)),
                  pl.BlockSpec((d, LANES), lambda i: (0, 0)),
                  pl.BlockSpec((1, LANES), lambda i: (0, 0))],
        out_specs=pl.BlockSpec((tm, LANES), lambda i: (i, 0)),
        out_shape=jax.ShapeDtypeStruct((m, LANES), F32),
        compiler_params=_cp("arbitrary"),
        name="router",
    )(x, wr, br)


MOE_TILE = 512


def _gather_rows(idx_ref, src_hbm, dst_ref, sem, n):
    for r in range(n):
        pltpu.make_async_copy(src_hbm.at[pl.ds(idx_ref[0, 0, r], 1)], dst_ref.at[pl.ds(r, 1)], sem).start()


def _wait_rows(src_hbm, dst_ref, sem, n):
    pltpu.make_async_copy(src_hbm.at[pl.ds(0, n)], dst_ref, sem).wait()


def _moe_expert_kernel(te_ref, cur_ref, nxt_ref, x_hbm, w1_ref, w3_ref, w2_ref, o_ref,
                       xg_ref, sem, w1b_ref, w3b_ref, w2b_ref):
    t = pl.program_id(0)
    nt = pl.num_programs(0)
    tile = xg_ref.shape[1]
    slot = t % 2

    @pl.when(t == 0)
    def _():
        _gather_rows(cur_ref, x_hbm, xg_ref.at[0], sem.at[0], tile)

    _gather_rows(nxt_ref, x_hbm, xg_ref.at[1 - slot], sem.at[1 - slot], tile)

    @pl.when((t == 0) | (te_ref[t] != te_ref[jnp.maximum(t - 1, 0)]))
    def _():
        w1b_ref[...] = w1_ref[...].astype(BF16)
        w3b_ref[...] = w3_ref[...].astype(BF16)
        w2b_ref[...] = w2_ref[...].astype(BF16)

    _wait_rows(x_hbm, xg_ref.at[slot], sem.at[slot], tile)
    xb = xg_ref[slot].astype(BF16)
    h1 = jnp.dot(xb, w1b_ref[...], preferred_element_type=F32)
    h3 = jnp.dot(xb, w3b_ref[...], preferred_element_type=F32)
    h = h1 * jax.nn.sigmoid(h1) * h3
    o_ref[...] = jnp.dot(h.astype(BF16), w2b_ref[...], preferred_element_type=F32)

    @pl.when(t == nt - 1)
    def _():
        _wait_rows(x_hbm, xg_ref.at[1 - slot], sem.at[1 - slot], tile)


def _moe_experts(x, w1, w3, w2, tile_expert, src_tiles):
    n, d = x.shape
    _, _, de = w1.shape
    nt = src_tiles.shape[0] - 1
    tile = src_tiles.shape[2]
    grid_spec = pltpu.PrefetchScalarGridSpec(
        num_scalar_prefetch=1,
        grid=(nt,),
        in_specs=[pl.BlockSpec((1, 1, tile), lambda t, te: (t, 0, 0), memory_space=pltpu.SMEM),
                  pl.BlockSpec((1, 1, tile), lambda t, te: (t + 1, 0, 0), memory_space=pltpu.SMEM),
                  pl.BlockSpec(memory_space=pl.ANY),
                  pl.BlockSpec((None, d, de), lambda t, te: (te[t], 0, 0)),
                  pl.BlockSpec((None, d, de), lambda t, te: (te[t], 0, 0)),
                  pl.BlockSpec((None, de, d), lambda t, te: (te[t], 0, 0))],
        out_specs=pl.BlockSpec((tile, d), lambda t, te: (t, 0)),
        scratch_shapes=[pltpu.VMEM((2, tile, d), F32),
                        pltpu.SemaphoreType.DMA((2,)),
                        pltpu.VMEM((d, de), BF16),
                        pltpu.VMEM((d, de), BF16),
                        pltpu.VMEM((de, d), BF16)],
    )
    return pl.pallas_call(
        _moe_expert_kernel,
        grid_spec=grid_spec,
        out_shape=jax.ShapeDtypeStruct((nt * tile, d), F32),
        compiler_params=_cp("arbitrary"),
        name="moe_experts",
    )(tile_expert, src_tiles, src_tiles, x, w1, w3, w2)


def _moe_combine_kernel(cur_ref, nxt_ref, y_hbm, rt_ref, x_ref, g_ref, b_ref, of_ref, ob_ref, yg_ref, sem, *, alpha):
    t = pl.program_id(0)
    nt = pl.num_programs(0)
    tm = x_ref.shape[0]
    slot = t % 2

    @pl.when(t == 0)
    def _():
        _gather_rows(cur_ref, y_hbm, yg_ref.at[0], sem.at[0], 2 * tm)

    _gather_rows(nxt_ref, y_hbm, yg_ref.at[1 - slot], sem.at[1 - slot], 2 * tm)
    _wait_rows(y_hbm, yg_ref.at[slot], sem.at[slot], 2 * tm)
    rt = rt_ref[...]
    moe = rt[:, 2:3] * yg_ref[slot, 0:tm] + rt[:, 3:4] * yg_ref[slot, tm:2 * tm]
    y = _layer_norm(alpha * x_ref[...] + moe, g_ref[...], b_ref[...])
    of_ref[...] = y
    ob_ref[...] = y.astype(BF16)

    @pl.when(t == nt - 1)
    def _():
        _wait_rows(y_hbm, yg_ref.at[1 - slot], sem.at[1 - slot], 2 * tm)


def _moe_combine_ln(y_sorted, pos_tiles, route, x, g, b, alpha, tm):
    m, d = x.shape
    kern = functools.partial(_moe_combine_kernel, alpha=alpha)
    return pl.pallas_call(
        kern,
        grid=(m // tm,),
        in_specs=[pl.BlockSpec((1, 1, 2 * tm), lambda t: (t, 0, 0), memory_space=pltpu.SMEM),
                  pl.BlockSpec((1, 1, 2 * tm), lambda t: (t + 1, 0, 0), memory_space=pltpu.SMEM),
                  pl.BlockSpec(memory_space=pl.ANY),
                  pl.BlockSpec((tm, LANES), lambda t: (t, 0)),
                  pl.BlockSpec((tm, d), lambda t: (t, 0)),
                  pl.BlockSpec((1, d), lambda t: (0, 0)),
                  pl.BlockSpec((1, d), lambda t: (0, 0))],
        out_specs=[pl.BlockSpec((tm, d), lambda t: (t, 0)), pl.BlockSpec((tm, d), lambda t: (t, 0))],
        out_shape=[jax.ShapeDtypeStruct((m, d), F32), jax.ShapeDtypeStruct((m, d), BF16)],
        scratch_shapes=[pltpu.VMEM((2, 2 * tm, d), F32), pltpu.SemaphoreType.DMA((2,))],
        compiler_params=_cp("arbitrary"),
        name="moe_combine_ln2",
    )(pos_tiles, pos_tiles, y_sorted, route, x, g.reshape(1, -1), b.reshape(1, -1))


def _moe_layout(route, tile, tm):
    n = route.shape[0]
    e_flat = route[:, 0:2].astype(I32).reshape(-1)
    npair = 2 * n
    nt = -(-npair // tile) + N_EXPERTS
    onehot = (e_flat[:, None] == jnp.arange(N_EXPERTS, dtype=I32)[None, :]).astype(I32)
    csum = jnp.cumsum(onehot, axis=0)
    counts = csum[-1]
    rank = jnp.sum((csum - onehot) * onehot, axis=1)
    padded = (counts + tile - 1) // tile * tile
    ends = jnp.cumsum(padded)
    starts = ends - padded
    pos = starts[e_flat] + rank
    rows = nt * tile
    src = jnp.zeros((rows,), I32).at[pos].set(jnp.arange(npair, dtype=I32) // 2, unique_indices=True)
    tile_expert = jnp.minimum(jnp.searchsorted(ends, jnp.arange(nt, dtype=I32) * tile, side='right'),
                              N_EXPERTS - 1).astype(I32)
    src_tiles = jnp.concatenate([src, jnp.zeros((tile,), I32)]).reshape(nt + 1, 1, tile)
    pos_t = pos.reshape(n // tm, tm, 2).transpose(0, 2, 1).reshape(n // tm, 1, 2 * tm)
    pos_tiles = jnp.concatenate([pos_t, jnp.zeros((1, 1, 2 * tm), I32)], axis=0)
    return tile_expert, src_tiles, pos_tiles


def kernel(x_prompt, x_sample, cache_k, cache_v, cache_idx_k, state_ret, state_conv, w_in, b_gate, w_att_o, ret_gn_g,
           w_ret_o, conv_dw, conv_dw_b, conv_ln_g, conv_ln_b, w_conv_o, w_out, ln1_g, ln1_b, w_router, b_router,
           moe_w1, moe_w3, moe_w2, ln2_g, ln2_b):
    nbp, tp, d = x_prompt.shape
    nbs, ts, _ = x_sample.shape
    depth = w_in.shape[0]
    past = cache_k.shape[2]
    np_, ns = nbp * tp, nbs * ts
    n = np_ + ns
    alpha = (2 * depth) ** 0.25
    kvw = KV_HEADS * ATT_HD

    tm = _pow2_tile(n, 512)
    tq_p = _pow2_tile(tp, 256)
    cl_p = _pow2_tile(tp, 256)
    tt_p = _pow2_tile(tp, 256)
    ls_true = past + ts
    ls_pad = -(-ls_true // (2 * LANES)) * (2 * LANES)
    assert tp % (2 * LANES) == 0 and np_ % max(tq_p, cl_p, tt_p, ts) == 0

    x = jnp.concatenate([x_prompt.reshape(np_, d), x_sample.reshape(ns, d)], axis=0)
    xb = x.astype(BF16)

    c1 = ATT_HEADS * ATT_HD
    c2 = c1 + F32_USED
    ks, vs, iks, rps, cps, kss, vss, ikss, rss, css = ([] for _ in range(10))
    zero_ret = jnp.zeros((nbp, RET_HEADS, RET_DK, RET_DV), F32)
    zero_conv = jnp.zeros((nbp, CONV_W - 1, CONV_CH), F32)

    for l in range(depth):
        w_l = w_in[l]
        w_bf = jnp.concatenate([w_l[:, c2:], w_l[:, :c1]], axis=1).astype(BF16)
        w_f = jnp.pad(w_l[:, c1:c2], ((0, 0), (0, F32_WIDTH - F32_USED))).astype(BF16)
        p_bf = _matmul(xb, w_bf, BF16, tm, 1024)
        p_f = _matmul(xb, w_f, F32, tm, F32_WIDTH)

        k_new = p_f[:, KA_OFF:KA_OFF + kvw]
        v_new = p_f[:, VA_OFF:VA_OFF + kvw]
        ik_new = p_f[:, KI_OFF:KI_OFF + IDX_DIM]
        wi_t = p_f[:, WI_OFF:WI_OFF + IDX_HEADS].T
        ks.append(k_new[:np_].reshape(nbp, tp, KV_HEADS, ATT_HD))
        vs.append(v_new[:np_].reshape(nbp, tp, KV_HEADS, ATT_HD))
        iks.append(ik_new[:np_].reshape(nbp, tp, IDX_DIM))
        kss.append(k_new[np_:].reshape(nbs, ts, KV_HEADS, ATT_HD))
        vss.append(v_new[np_:].reshape(nbs, ts, KV_HEADS, ATT_HD))
        ikss.append(ik_new[np_:].reshape(nbs, ts, IDX_DIM))

        nq = tp // tq_p
        hq = ATT_HEADS * ATT_HD
        qa_blk = (GATE_OFF + 3 * d) // hq
        att_p = _attention(
            p_bf, lambda tq: ((tq, hq), lambda b, j: (b * nq + j, qa_blk)),
            p_f, lambda lp: ((lp, kvw), lambda b, j: (b, KA_OFF // kvw)),
            p_f, lambda lp: ((lp, kvw), lambda b, j: (b, VA_OFF // kvw)),
            p_f, lambda tq: ((tq, IDX_HEADS * IDX_DIM), lambda b, j: (b * nq + j, QI_OFF // (IDX_HEADS * IDX_DIM))),
            p_f, lambda lp: ((lp, LANES), lambda b, j: (b, KI_OFF // LANES)),
            wi_t[:, :np_], lambda tq: ((IDX_HEADS, tq), lambda b, j: (0, b * nq + j)),
            nb=nbp, t=tp, l_pad=tp, l_true=tp, offset=0, tq=tq_p)

        mask_s = _sample_select(cache_idx_k[l], p_f, wi_t, np_, nbs, ts, past)
        att_s = _sample_attend(p_bf, qa_blk, cache_k[l].reshape(nbs, past, kvw), cache_v[l].reshape(nbs, past, kvw),
                               p_f, mask_s, np_, nbs, ts, past)
        att = jnp.concatenate([att_p, att_s], axis=0)

        ret_p, rs_p = _retention(p_bf, 0, nbp, tp, cl_p, 0, zero_ret, ret_gn_g[l])
        ret_s, rs_s = _retention(p_bf, np_, nbs, ts, ts, past, state_ret[l], ret_gn_g[l])
        ret = jnp.concatenate([ret_p, ret_s], axis=0)
        rps.append(rs_p)
        rss.append(rs_s)

        cnv_p, cs_p = _conv_module(p_bf, 0, nbp, tp, tt_p, zero_conv, conv_dw[l], conv_dw_b[l], conv_ln_g[l], conv_ln_b[l])
        cnv_s, cs_s = _conv_module(p_bf, np_, nbs, ts, ts, state_conv[l], conv_dw[l], conv_dw_b[l], conv_ln_g[l], conv_ln_b[l])
        cnv = jnp.concatenate([cnv_p, cnv_s], axis=0)
        cps.append(cs_p)
        css.append(cs_s)

        z = _merge(att, ret, cnv, p_bf, b_gate[l], w_att_o[l].astype(BF16), w_ret_o[l].astype(BF16),
                   w_conv_o[l].astype(BF16), tm, 512)
        x, xb = _outproj_ln(z, w_out[l].astype(BF16), x, ln1_g[l], ln1_b[l], alpha, _pow2_tile(n, 256))
        route = _router(x, w_router, b_router, _pow2_tile(n, 256))
        tm_c = _pow2_tile(n, 256)
        tile_expert, src_tiles, pos_tiles = _moe_layout(route, MOE_TILE, tm_c)
        y_sorted = _moe_experts(x, moe_w1[l], moe_w3[l], moe_w2[l], tile_expert, src_tiles)
        x, xb = _moe_combine_ln(y_sorted, pos_tiles, route, x, ln2_g[l], ln2_b[l], alpha, tm_c)

    y_prompt = x[:np_].reshape(nbp, tp, d)
    y_sample = x[np_:].reshape(nbs, ts, d)
    st = jnp.stack
    return (y_prompt, y_sample, st(ks), st(vs), st(iks), st(rps), st(cps), st(kss), st(vss), st(ikss), st(rss), st(css))
```

```python
import functools
import math

import jax
import jax.numpy as jnp
from jax import lax
from jax.experimental import pallas as pl
from jax.experimental.pallas import tpu as pltpu

F32 = jnp.float32
BF16 = jnp.bfloat16
I32 = jnp.int32

CHUNK = 64
TOPK_MAX = 256
ATT_HEADS, ATT_HD, KV_HEADS = 8, 128, 2
KV_GROUP = ATT_HEADS // KV_HEADS
IDX_HEADS, IDX_DIM = 8, 64
RET_HEADS, RET_DK, RET_DV = 8, 128, 256
ROPE_BASE = 10000.0
CONV_CH, CONV_W = 1024, 31
N_EXPERTS, N_GROUPS = 16, 4
EXP_PER_GROUP = N_EXPERTS // N_GROUPS
LN_EPS = 1e-5

LANES = 128
SUBLANES = 8

INT_MIN = -2 ** 31
NEG_BIG = -1e30

QR_OFF = 0
KR_OFF = QR_OFF + RET_HEADS * RET_DK
VR_OFF = KR_OFF + RET_HEADS * RET_DK
GR_OFF = VR_OFF + RET_HEADS * RET_DV
CIN_OFF = GR_OFF + RET_HEADS * RET_DV
GATE_OFF = CIN_OFF + 2 * CONV_CH
KA_OFF = 0
VA_OFF = KA_OFF + KV_HEADS * ATT_HD
QI_OFF = VA_OFF + KV_HEADS * ATT_HD
KI_OFF = QI_OFF + IDX_HEADS * IDX_DIM
WI_OFF = KI_OFF + IDX_DIM
F32_USED = WI_OFF + IDX_HEADS
F32_WIDTH = -(-F32_USED // LANES) * LANES

VMEM_LIMIT = 56 * 1024 * 1024


def _cp(*sem):
    return pltpu.CompilerParams(dimension_semantics=sem, vmem_limit_bytes=VMEM_LIMIT)


def _pow2_tile(n, pref):
    t = pref
    while n % t:
        t //= 2
    return t


def _layer_norm(x, g, b):
    mu = jnp.mean(x, axis=-1, keepdims=True)
    xc = x - mu
    var = jnp.mean(xc * xc, axis=-1, keepdims=True)
    return xc * lax.rsqrt(var + LN_EPS) * g + b


def _mm_kernel(x_ref, w_ref, o_ref):
    o_ref[...] = jnp.dot(x_ref[...], w_ref[...], preferred_element_type=F32).astype(o_ref.dtype)


def _matmul(x, w, out_dtype, tm, tn):
    m, k = x.shape
    n = w.shape[1]
    return pl.pallas_call(
        _mm_kernel,
        grid=(n // tn, m // tm),
        in_specs=[pl.BlockSpec((tm, k), lambda j, i: (i, 0)),
                  pl.BlockSpec((k, tn), lambda j, i: (0, j))],
        out_specs=pl.BlockSpec((tm, tn), lambda j, i: (i, j)),
        out_shape=jax.ShapeDtypeStruct((m, n), out_dtype),
        compiler_params=_cp("arbitrary", "arbitrary"),
        name="in_proj",
    )(x, w)


def _attn_kernel(q_ref, k_ref, v_ref, qi_ref, ki_ref, wi_ref, o_ref,
                 key_ref, lg_ref, mx_ref, acc_ref, m_ref,
                 *, tq, nq, ck, nck, offset, l_true, k_sel, tie_bits):
    j = pl.program_id(1)
    nsub = ck // LANES
    qlane = lax.broadcasted_iota(I32, (1, tq), 1)
    krow = lax.broadcasted_iota(I32, (ck, tq), 0)
    pos = offset + j * tq + qlane
    limit = jnp.minimum((pos // CHUNK + 1) * CHUNK, l_true)
    if nq == 1:
        n_act = (min(((offset + tq - 1) // CHUNK + 1) * CHUNK, l_true) + ck - 1) // ck
    else:
        last_pos = offset + (j + 1) * tq - 1
        max_limit = jnp.minimum((last_pos // CHUNK + 1) * CHUNK, l_true)
        n_act = (max_limit + ck - 1) // ck

    qi = (qi_ref[...] * (IDX_DIM ** -0.5)).astype(BF16)
    qis = [qi[:, h * IDX_DIM:(h + 1) * IDX_DIM] for h in range(IDX_HEADS)]
    stacked = tq % LANES == 0
    if stacked:
        qstack = jnp.concatenate(qis, axis=0)
    wt = wi_ref[...] * (IDX_HEADS ** -0.5)
    wrows = [wt[h:h + 1, :] for h in range(IDX_HEADS)]
    nt = (((1,), (1,)), ((), ()))

    def score_body(c, carry):
        for u in range(nsub):
            r0 = pl.multiple_of(c * ck + u * LANES, LANES)
            kiu = ki_ref[pl.ds(r0, LANES), :][:, :IDX_DIM].astype(BF16)
            if stacked:
                d_all = lax.dot_general(kiu, qstack, nt, preferred_element_type=F32)
                ds = [d_all[:, h * tq:(h + 1) * tq] for h in range(IDX_HEADS)]
            else:
                ds = [lax.dot_general(kiu, qis[h], nt, preferred_element_type=F32) for h in range(IDX_HEADS)]
            s = jnp.zeros((LANES, tq), F32)
            for h in range(IDX_HEADS):
                s = s + jnp.maximum(ds[h], 0.0) * wrows[h]
            bits = lax.bitcast_convert_type(s, I32)
            bits = jnp.where(bits == INT_MIN, 0, bits)
            key = jnp.where(bits >= 0, bits, bits ^ 0x7FFFFFFF)
            kpos = r0 + lax.broadcasted_iota(I32, (LANES, tq), 0)
            key_ref[c, u * LANES:(u + 1) * LANES, :] = jnp.where(kpos < limit, key, INT_MIN)
        return carry

    lax.fori_loop(0, n_act, score_body, 0)

    def count(pred):
        def body(c, acc):
            p = jnp.where(pred(key_ref[c], c * ck + krow), 1.0, 0.0)
            parts = [p[r * SUBLANES:(r + 1) * SUBLANES] for r in range(ck // SUBLANES)]
            while len(parts) > 1:
                parts = [a + b for a, b in zip(parts[0::2], parts[1::2])]
            return acc + parts[0]
        acc = lax.fori_loop(0, n_act, body, jnp.zeros((SUBLANES, tq), F32))
        return jnp.sum(acc, axis=0, keepdims=True)

    def bisect(i, tu):
        cand_u = tu | (jnp.int32(1) << (31 - i))
        cand_s = cand_u ^ INT_MIN
        cnt = count(lambda key, idx: key >= cand_s)
        return jnp.where(cnt >= k_sel, cand_u, tu)

    tu = lax.fori_loop(0, 32, bisect, jnp.zeros((1, tq), I32))
    ts = tu ^ INT_MIN
    c_gt = count(lambda key, idx: key > ts)
    c_ge = count(lambda key, idx: key >= ts)
    want = k_sel - c_gt

    eye = jnp.where(lax.broadcasted_iota(I32, (tq, tq), 0) == lax.broadcasted_iota(I32, (tq, tq), 1), 1.0, 0.0).astype(BF16)
    ts_adm = jnp.maximum(ts, INT_MIN + 1)
    m_ref[...] = jnp.full((1, tq), 2 ** 30, I32)

    @pl.when(jnp.max(c_ge) > k_sel)
    def _():
        def tie_bisect(i, m):
            cand = m | (jnp.int32(1) << (tie_bits - 1 - i))
            cnt = count(lambda key, idx: (key == ts) & (idx < cand))
            return jnp.where(cnt < want, cand, m)
        m_ref[...] = lax.fori_loop(0, tie_bits, tie_bisect, jnp.zeros((1, tq), I32))

    m_last = m_ref[...]

    qscale = (ATT_HD ** -0.5) * 1.4426950408889634
    groups = range(KV_HEADS)
    qss = []
    for g in groups:
        qs = jnp.concatenate(
            [q_ref[:, (g * KV_GROUP + r) * ATT_HD:(g * KV_GROUP + r + 1) * ATT_HD] for r in range(KV_GROUP)], axis=0)
        qss.append((qs.astype(F32) * qscale).astype(BF16))
    mx_ref[...] = jnp.full(mx_ref.shape, NEG_BIG, F32)

    def logit_body(c, carry):
        r0 = pl.multiple_of(c * ck, ck)
        key = key_ref[c]
        sel = (key > ts_adm) | ((key == ts_adm) & (c * ck + krow <= m_last))
        sel_t = jnp.where(sel, 1.0, 0.0).astype(BF16)
        sel_q = lax.dot_general(eye, sel_t, nt, preferred_element_type=F32)
        b = (sel_q - 1.0) * (-NEG_BIG)
        b4 = jnp.concatenate([b] * KV_GROUP, axis=0)
        for g in groups:
            kc = k_ref[pl.ds(r0, ck), g * ATT_HD:(g + 1) * ATT_HD].astype(BF16)
            lg = lax.dot_general(qss[g], kc, nt, preferred_element_type=F32) + b4
            lg_ref[g, c] = lg
            m = mx_ref[g]
            for u in range(nsub):
                m = jnp.maximum(m, lg[:, u * LANES:(u + 1) * LANES])
            mx_ref[g] = m
        return carry

    lax.fori_loop(0, n_act, logit_body, 0)
    m_rows = [jnp.max(mx_ref[g], axis=-1, keepdims=True) for g in groups]
    acc_ref[...] = jnp.zeros(acc_ref.shape, F32)
    ones_cols = jnp.ones((ck, ATT_HD), BF16)

    def pv_body(c, carry):
        r0 = pl.multiple_of(c * ck, ck)
        for g in groups:
            p = jnp.exp2(lg_ref[g, c] - m_rows[g]).astype(BF16)
            vc = v_ref[pl.ds(r0, ck), g * ATT_HD:(g + 1) * ATT_HD].astype(BF16)
            acc_ref[g] += jnp.dot(p, jnp.concatenate([vc, ones_cols], axis=1), preferred_element_type=F32)
        return carry

    lax.fori_loop(0, n_act, pv_body, 0)
    for g in groups:
        a = acc_ref[g]
        o = a[:, :ATT_HD] / a[:, ATT_HD:]
        for r in range(KV_GROUP):
            h = g * KV_GROUP + r
            o_ref[:, h * ATT_HD:(h + 1) * ATT_HD] = o[r * tq:(r + 1) * tq].astype(o_ref.dtype)


def _attention(q_arr, q_map, k_arr, k_map, v_arr, v_map, qi_arr, qi_map, ki_arr, ki_map, wi_arr, wi_map,
               *, nb, t, l_pad, l_true, offset, tq):
    ck = 2 * LANES
    nck = l_pad // ck
    assert l_pad % ck == 0 and t % tq == 0 and nck * (ck // LANES) < 256
    nq = t // tq
    k_sel = min(TOPK_MAX, l_true // 4)
    tie_bits = max(1, (l_pad - 1).bit_length())
    kern = functools.partial(_attn_kernel, tq=tq, nq=nq, ck=ck, nck=nck, offset=offset, l_true=l_true,
                             k_sel=k_sel, tie_bits=tie_bits)
    hq = ATT_HEADS * ATT_HD
    return pl.pallas_call(
        kern,
        grid=(nb, nq),
        in_specs=[pl.BlockSpec(*q_map(tq)), pl.BlockSpec(*k_map(l_pad)), pl.BlockSpec(*v_map(l_pad)),
                  pl.BlockSpec(*qi_map(tq)), pl.BlockSpec(*ki_map(l_pad)), pl.BlockSpec(*wi_map(tq))],
        out_specs=pl.BlockSpec((tq, hq), lambda b, j: (b * nq + j, 0)),
        out_shape=jax.ShapeDtypeStruct((nb * t, hq), BF16),
        scratch_shapes=[pltpu.VMEM((nck, ck, tq), I32),
                        pltpu.VMEM((KV_HEADS, nck, KV_GROUP * tq, ck), F32),
                        pltpu.VMEM((KV_HEADS, KV_GROUP * tq, LANES), F32),
                        pltpu.VMEM((KV_HEADS, KV_GROUP * tq, 2 * ATT_HD), F32),
                        pltpu.VMEM((1, tq), I32)],
        compiler_params=_cp("arbitrary", "arbitrary"),
        name="dsa_attention",
    )(q_arr, k_arr, v_arr, qi_arr, ki_arr, wi_arr)


def _order_key(s):
    bits = lax.bitcast_convert_type(s, I32)
    bits = jnp.where(bits == INT_MIN, 0, bits)
    return jnp.where(bits >= 0, bits, bits ^ 0x7FFFFFFF)


def _sample_select_kernel(kic_ref, new_ref, qi_ref, wi_ref, mask_ref, key_ref, keyn_ref, m_ref,
                          *, ts, gs, past, ck, k_sel, tie_bits):
    nck = past // ck
    nsub = ck // LANES
    l_true = past + ts
    lane = lax.broadcasted_iota(I32, (1, LANES), 1)
    lane_s = lane // ts
    pos = past + lane % ts
    limit = jnp.minimum((pos // CHUNK + 1) * CHUNK, l_true)
    nt = (((1,), (1,)), ((), ()))

    qi = (qi_ref[...] * (IDX_DIM ** -0.5)).astype(BF16)
    qstack = jnp.concatenate([qi[:, h * IDX_DIM:(h + 1) * IDX_DIM] for h in range(IDX_HEADS)], axis=0)
    wt = wi_ref[...] * (IDX_HEADS ** -0.5)
    wrows = [wt[h:h + 1, :] for h in range(IDX_HEADS)]

    def own_stream(d, rows):
        out = d[(gs - 1) * rows:gs * rows]
        for s in range(gs - 2, -1, -1):
            out = jnp.where(lane_s == s, d[s * rows:(s + 1) * rows], out)
        return out

    def scores(kmat, rows):
        d_all = lax.dot_general(kmat, qstack, nt, preferred_element_type=F32)
        s = jnp.zeros((rows, LANES), F32)
        for h in range(IDX_HEADS):
            s = s + jnp.maximum(own_stream(d_all[:, h * LANES:(h + 1) * LANES], rows), 0.0) * wrows[h]
        return _order_key(s)

    def score_body(c, carry):
        for u in range(nsub):
            r0 = pl.multiple_of(c * ck + u * LANES, LANES)
            kmat = jnp.concatenate([kic_ref[s, pl.ds(r0, LANES), :] for s in range(gs)], axis=0).astype(BF16)
            kpos = r0 + lax.broadcasted_iota(I32, (LANES, LANES), 0)
            key_ref[c, u * LANES:(u + 1) * LANES, :] = jnp.where(kpos < limit, scores(kmat, LANES), INT_MIN)
        return carry

    lax.fori_loop(0, nck, score_body, 0)
    kn = new_ref[:, :IDX_DIM].astype(BF16)
    npos = past + lax.broadcasted_iota(I32, (ts, LANES), 0)
    keyn_ref[...] = jnp.where(npos < limit, scores(kn, ts), INT_MIN)

    krow = lax.broadcasted_iota(I32, (ck, LANES), 0)

    def tree(p, rows):
        parts = [p[r * SUBLANES:(r + 1) * SUBLANES] for r in range(rows // SUBLANES)]
        while len(parts) > 1:
            parts = [a + b for a, b in zip(parts[0::2], parts[1::2])]
        return parts[0]

    def count(pred):
        def body(c, acc):
            p = jnp.where(pred(key_ref[c], c * ck + krow), 1.0, 0.0)
            return acc + tree(p, ck)
        acc = lax.fori_loop(0, nck, body, jnp.zeros((SUBLANES, LANES), F32))
        acc = acc + tree(jnp.where(pred(keyn_ref[...], npos), 1.0, 0.0), ts)
        return jnp.sum(acc, axis=0, keepdims=True)

    def bisect(i, tu):
        cand_u = tu | (jnp.int32(1) << (31 - i))
        cand_s = cand_u ^ INT_MIN
        cnt = count(lambda key, idx: key >= cand_s)
        return jnp.where(cnt >= k_sel, cand_u, tu)

    tu = lax.fori_loop(0, 32, bisect, jnp.zeros((1, LANES), I32))
    ts_ = tu ^ INT_MIN
    c_gt = count(lambda key, idx: key > ts_)
    c_ge = count(lambda key, idx: key >= ts_)
    want = k_sel - c_gt
    m_ref[...] = jnp.full((1, LANES), 2 ** 30, I32)

    @pl.when(jnp.max(c_ge) > k_sel)
    def _():
        def tie_bisect(i, m):
            cand = m | (jnp.int32(1) << (tie_bits - 1 - i))
            cnt = count(lambda key, idx: (key == ts_) & (idx < cand))
            return jnp.where(cnt < want, cand, m)
        m_ref[...] = lax.fori_loop(0, tie_bits, tie_bisect, jnp.zeros((1, LANES), I32))

    m_last = m_ref[...]
    ts_adm = jnp.maximum(ts_, INT_MIN + 1)
    eye = jnp.where(lax.broadcasted_iota(I32, (LANES, LANES), 0) == lax.broadcasted_iota(I32, (LANES, LANES), 1),
                    1.0, 0.0).astype(BF16)

    def selected(key, idx):
        sel = (key > ts_adm) | ((key == ts_adm) & (idx <= m_last))
        return jnp.where(sel, 1.0, 0.0).astype(BF16)

    for c in range(nck):
        sel_q = lax.dot_general(eye, selected(key_ref[c], c * ck + krow), nt, preferred_element_type=F32)
        mask_ref[:, c * ck:(c + 1) * ck] = sel_q.astype(BF16)
    sel_n = lax.dot_general(eye, selected(keyn_ref[...], npos), nt, preferred_element_type=F32)
    mask_ref[:, past:past + LANES] = jnp.concatenate(
        [sel_n, jnp.zeros((LANES, LANES - ts), F32)], axis=1).astype(BF16)


def _sample_select(cache_ik, p_f, wi_t, np_, nbs, ts, past):
    gs = LANES // ts
    ck = 2 * LANES
    assert LANES % ts == 0 and nbs % gs == 0 and past % ck == 0 and np_ % LANES == 0 and ts % SUBLANES == 0
    l_true = past + ts
    k_sel = min(TOPK_MAX, l_true // 4)
    kern = functools.partial(_sample_select_kernel, ts=ts, gs=gs, past=past, ck=ck, k_sel=k_sel,
                             tie_bits=max(1, (l_true - 1).bit_length()))
    rb = np_ // LANES
    return pl.pallas_call(
        kern,
        grid=(nbs // gs,),
        in_specs=[pl.BlockSpec((gs, past, IDX_DIM), lambda g: (g, 0, 0)),
                  pl.BlockSpec((LANES, LANES), lambda g: (rb + g, KI_OFF // LANES)),
                  pl.BlockSpec((LANES, IDX_HEADS * IDX_DIM), lambda g: (rb + g, QI_OFF // (IDX_HEADS * IDX_DIM))),
                  pl.BlockSpec((IDX_HEADS, LANES), lambda g: (0, rb + g))],
        out_specs=pl.BlockSpec((LANES, past + LANES), lambda g: (g, 0)),
        out_shape=jax.ShapeDtypeStruct((nbs * ts, past + LANES), BF16),
        scratch_shapes=[pltpu.VMEM((past // ck, ck, LANES), I32),
                        pltpu.VMEM((ts, LANES), I32),
                        pltpu.VMEM((1, LANES), I32)],
        compiler_params=_cp("arbitrary"),
        name="sample_select",
    )(cache_ik, p_f, p_f, wi_t)


def _sample_attend_kernel(q_ref, kc_ref, vc_ref, kn_ref, vn_ref, mask_ref, o_ref, lg_ref, *, ts, past, ca):
    nt = (((1,), (1,)), ((), ()))
    qscale = (ATT_HD ** -0.5) * 1.4426950408889634
    nca = past // ca
    rows = KV_GROUP * ts

    def bias_of(m):
        b = (m.astype(F32) - 1.0) * (-NEG_BIG)
        return jnp.concatenate([b] * KV_GROUP, axis=0)

    bias_n = bias_of(mask_ref[:, past:past + ts])
    for g in range(KV_HEADS):
        qs = jnp.concatenate(
            [q_ref[:, (g * KV_GROUP + r) * ATT_HD:(g * KV_GROUP + r + 1) * ATT_HD] for r in range(KV_GROUP)], axis=0)
        qs = (qs.astype(F32) * qscale).astype(BF16)
        cols = slice(g * ATT_HD, (g + 1) * ATT_HD)
        lg_n = lax.dot_general(qs, kn_ref[:, cols].astype(BF16), nt, preferred_element_type=F32) + bias_n
        m_run = jnp.full((rows, LANES), NEG_BIG, F32)
        for c in range(nca):
            kc = kc_ref[c * ca:(c + 1) * ca, cols].astype(BF16)
            lg = lax.dot_general(qs, kc, nt, preferred_element_type=F32) + bias_of(mask_ref[:, c * ca:(c + 1) * ca])
            lg_ref[g, :, c * ca:(c + 1) * ca] = lg
            for u in range(ca // LANES):
                m_run = jnp.maximum(m_run, lg[:, u * LANES:(u + 1) * LANES])
        m_row = jnp.maximum(jnp.max(m_run, axis=-1, keepdims=True), jnp.max(lg_n, axis=-1, keepdims=True))
        ones_n = jnp.ones((ts, ATT_HD), BF16)
        acc = jnp.dot(jnp.exp2(lg_n - m_row).astype(BF16),
                      jnp.concatenate([vn_ref[:, cols].astype(BF16), ones_n], axis=1), preferred_element_type=F32)
        ones_c = jnp.ones((ca, ATT_HD), BF16)
        for c in range(nca):
            p = jnp.exp2(lg_ref[g, :, c * ca:(c + 1) * ca] - m_row).astype(BF16)
            vc = vc_ref[c * ca:(c + 1) * ca, cols].astype(BF16)
            acc = acc + jnp.dot(p, jnp.concatenate([vc, ones_c], axis=1), preferred_element_type=F32)
        o = acc[:, :ATT_HD] / acc[:, ATT_HD:]
        for r in range(KV_GROUP):
            h = g * KV_GROUP + r
            o_ref[:, h * ATT_HD:(h + 1) * ATT_HD] = o[r * ts:(r + 1) * ts].astype(o_ref.dtype)


def _sample_attend(p_bf, qa_blk, cache_k, cache_v, p_f, mask, np_, nbs, ts, past):
    kvw = KV_HEADS * ATT_HD
    hq = ATT_HEADS * ATT_HD
    ca = 4 * LANES
    assert past % ca == 0
    rb = np_ // ts
    kern = functools.partial(_sample_attend_kernel, ts=ts, past=past, ca=ca)
    return pl.pallas_call(
        kern,
        grid=(nbs,),
        in_specs=[pl.BlockSpec((ts, hq), lambda b: (rb + b, qa_blk)),
                  pl.BlockSpec((None, past, kvw), lambda b: (b, 0, 0)),
                  pl.BlockSpec((None, past, kvw), lambda b: (b, 0, 0)),
                  pl.BlockSpec((ts, kvw), lambda b: (rb + b, KA_OFF // kvw)),
                  pl.BlockSpec((ts, kvw), lambda b: (rb + b, VA_OFF // kvw)),
                  pl.BlockSpec((ts, past + LANES), lambda b: (b, 0))],
        out_specs=pl.BlockSpec((ts, hq), lambda b: (b, 0)),
        out_shape=jax.ShapeDtypeStruct((nbs * ts, hq), BF16),
        scratch_shapes=[pltpu.VMEM((KV_HEADS, KV_GROUP * ts, past), F32)],
        compiler_params=_cp("arbitrary"),
        name="sample_attend",
    )(p_bf, cache_k, cache_v, p_f, p_f, mask)


def _ret_kernel(q_ref, k_ref, v_ref, g_ref, cs_ref, sn_ref, dec_ref, xi_ref, zt_ref, gp_ref, gn_ref, s0_ref,
                y_ref, sf_ref, s_ref):
    c = pl.program_id(1)
    dk, dv = RET_DK, RET_DV

    @pl.when(c == 0)
    def _():
        s_ref[...] = s0_ref[...]

    cs = cs_ref[...]
    sn = sn_ref[...]

    def rot(x):
        xf = x.astype(F32)
        return xf * cs + pltpu.roll(xf, dk // 2, 1) * sn

    for h in range(RET_HEADS):
        q = rot(q_ref[:, h * dk:(h + 1) * dk])
        k = rot(k_ref[:, h * dk:(h + 1) * dk]) * (dk ** -0.5)
        qb = q.astype(BF16)
        kb = k.astype(BF16)
        v = v_ref[:, h * dv:(h + 1) * dv]
        inner = lax.dot_general(qb, kb, (((1,), (1,)), ((), ())), preferred_element_type=F32) * dec_ref[h]
        s = s_ref[h]
        o = jnp.dot(inner.astype(BF16), v, preferred_element_type=F32)
        o = o + jnp.dot(qb, s.astype(BF16), preferred_element_type=F32) * xi_ref[h]
        kz = (k * zt_ref[h]).T.astype(BF16)
        s_new = s * gp_ref[h] + jnp.dot(kz, v, preferred_element_type=F32)
        s_ref[h] = s_new
        sf_ref[h] = s_new

        mu = jnp.mean(o, axis=-1, keepdims=True)
        oc = o - mu
        var = jnp.mean(oc * oc, axis=-1, keepdims=True)
        yn = oc * lax.rsqrt(var + LN_EPS) * gn_ref[:, h * dv:(h + 1) * dv]
        gg = g_ref[:, h * dv:(h + 1) * dv].astype(F32)
        y_ref[:, h * dv:(h + 1) * dv] = (gg * jax.nn.sigmoid(gg) * yn).astype(y_ref.dtype)


def _retention(p_bf, row0, nb, t, cl, offset, s0, gn_g):
    nc = t // cl
    h_, dk, dv = RET_HEADS, RET_DK, RET_DV
    half = dk // 2
    pos = (offset + jnp.arange(t, dtype=I32)).astype(F32)
    inv = 1.0 / (ROPE_BASE ** (jnp.arange(half, dtype=F32) / half))
    ang = pos[:, None] * inv[None, :]
    cos, sin = jnp.cos(ang), jnp.sin(ang)
    cs2 = jnp.concatenate([cos, cos], axis=-1)
    sn2 = jnp.concatenate([-sin, sin], axis=-1)
    log_g = jnp.log1p(-jnp.exp2(-5.0 - jnp.arange(h_, dtype=F32)))
    n = jnp.arange(cl, dtype=F32)
    diff = n[:, None] - n[None, :]
    decay = jnp.where(diff >= 0, jnp.exp(log_g[:, None, None] * jnp.maximum(diff, 0.0)), 0.0)
    xi = jnp.exp(log_g[:, None] * (n + 1.0))[..., None]
    zeta = jnp.exp(log_g[:, None] * (cl - 1.0 - n))[..., None]
    gpow = jnp.broadcast_to(jnp.exp(log_g * cl)[:, None, None], (h_, 1, dv))
    rb = row0 // cl
    return pl.pallas_call(
        _ret_kernel,
        grid=(nb, nc),
        in_specs=[pl.BlockSpec((cl, h_ * dk), lambda b, c: (rb + b * nc + c, QR_OFF // (h_ * dk))),
                  pl.BlockSpec((cl, h_ * dk), lambda b, c: (rb + b * nc + c, KR_OFF // (h_ * dk))),
                  pl.BlockSpec((cl, h_ * dv), lambda b, c: (rb + b * nc + c, VR_OFF // (h_ * dv))),
                  pl.BlockSpec((cl, h_ * dv), lambda b, c: (rb + b * nc + c, GR_OFF // (h_ * dv))),
                  pl.BlockSpec((cl, dk), lambda b, c: (c, 0)),
                  pl.BlockSpec((cl, dk), lambda b, c: (c, 0)),
                  pl.BlockSpec((h_, cl, cl), lambda b, c: (0, 0, 0)),
                  pl.BlockSpec((h_, cl, 1), lambda b, c: (0, 0, 0)),
                  pl.BlockSpec((h_, cl, 1), lambda b, c: (0, 0, 0)),
                  pl.BlockSpec((h_, 1, dv), lambda b, c: (0, 0, 0)),
                  pl.BlockSpec((1, h_ * dv), lambda b, c: (0, 0)),
                  pl.BlockSpec((None, h_, dk, dv), lambda b, c: (b, 0, 0, 0))],
        out_specs=[pl.BlockSpec((cl, h_ * dv), lambda b, c: (b * nc + c, 0)),
                   pl.BlockSpec((None, h_, dk, dv), lambda b, c: (b, 0, 0, 0))],
        out_shape=[jax.ShapeDtypeStruct((nb * t, h_ * dv), BF16),
                   jax.ShapeDtypeStruct((nb, h_, dk, dv), F32)],
        scratch_shapes=[pltpu.VMEM((h_, dk, dv), F32)],
        compiler_params=_cp("arbitrary", "arbitrary"),
        name="retention",
    )(p_bf, p_bf, p_bf, p_bf, cs2, sn2, decay, xi, zeta, gpow, gn_g.reshape(1, -1), s0)


CONV_HALO = 32
CONV_ROWS = 64
CONV_BLOCK = 64


def _conv_kernel(ca_ref, cb_ref, st_ref, w_ref, b_ref, lg_ref, lb_ref, y_ref, tail_ref, sh_ref, h_ref, *, tt):
    t = pl.program_id(1)
    ext = tt + CONV_HALO

    @pl.when(t == 0)
    def _():
        sh_ref[0, 0:CONV_HALO] = st_ref[...]

    @pl.when(t > 0)
    def _():
        sh_ref[0, 0:CONV_HALO] = sh_ref[0, tt:ext]

    ca = ca_ref[...].astype(F32)
    cb = cb_ref[...].astype(F32)
    sh_ref[0, CONV_HALO:ext] = ca * jax.nn.sigmoid(cb)
    tail_ref[...] = sh_ref[0, tt:ext]
    for s in range(1, SUBLANES):
        sh_ref[s, 0:ext - SUBLANES] = sh_ref[0, s:ext - SUBLANES + s]

    lead = CONV_HALO - (CONV_W - 1)
    blk = min(CONV_BLOCK, tt)
    nblk = blk // SUBLANES

    for lc in range(CONV_CH // LANES):
        cols = slice(lc * LANES, (lc + 1) * LANES)
        taps = [jnp.broadcast_to(w_ref[k:k + 1, cols], (SUBLANES, LANES)) for k in range(CONV_W)]
        bias = jnp.broadcast_to(b_ref[:, cols], (SUBLANES, LANES))

        def conv_body(i, carry):
            r0 = pl.multiple_of(i * blk, blk)
            accs = [bias] * nblk
            for k in range(CONV_W):
                a, s = divmod(k + lead, SUBLANES)
                u = sh_ref[s, pl.ds(r0 + a * SUBLANES, blk), cols]
                accs = [accs[j] + taps[k] * u[j * SUBLANES:(j + 1) * SUBLANES] for j in range(nblk)]
            h_ref[pl.ds(r0, blk), cols] = jnp.concatenate(accs, axis=0)
            return carry

        lax.fori_loop(0, tt // blk, conv_body, 0)

    rows = min(CONV_ROWS, tt)

    def norm_body(i, carry):
        r0 = pl.multiple_of(i * rows, rows)
        hn = _layer_norm(h_ref[pl.ds(r0, rows), :], lg_ref[...], lb_ref[...])
        y_ref[pl.ds(r0, rows), :] = (hn * jax.nn.sigmoid(hn)).astype(y_ref.dtype)
        return carry

    lax.fori_loop(0, tt // rows, norm_body, 0)


def _conv_module(p_bf, row0, nb, t, tt, state, dw, dw_b, ln_g, ln_b):
    nt = t // tt
    lead = CONV_HALO - (CONV_W - 1)
    st = jnp.pad(state, ((0, 0), (lead, 0), (0, 0)))
    rb = row0 // tt
    kern = functools.partial(_conv_kernel, tt=tt)
    y, tail = pl.pallas_call(
        kern,
        grid=(nb, nt),
        in_specs=[pl.BlockSpec((tt, CONV_CH), lambda b, i: (rb + b * nt + i, CIN_OFF // CONV_CH)),
                  pl.BlockSpec((tt, CONV_CH), lambda b, i: (rb + b * nt + i, CIN_OFF // CONV_CH + 1)),
                  pl.BlockSpec((None, CONV_HALO, CONV_CH), lambda b, i: (b, 0, 0)),
                  pl.BlockSpec((CONV_W, CONV_CH), lambda b, i: (0, 0)),
                  pl.BlockSpec((1, CONV_CH), lambda b, i: (0, 0)),
                  pl.BlockSpec((1, CONV_CH), lambda b, i: (0, 0)),
                  pl.BlockSpec((1, CONV_CH), lambda b, i: (0, 0))],
        out_specs=[pl.BlockSpec((tt, CONV_CH), lambda b, i: (b * nt + i, 0)),
                   pl.BlockSpec((None, CONV_HALO, CONV_CH), lambda b, i: (b, 0, 0))],
        out_shape=[jax.ShapeDtypeStruct((nb * t, CONV_CH), BF16),
                   jax.ShapeDtypeStruct((nb, CONV_HALO, CONV_CH), F32)],
        scratch_shapes=[pltpu.VMEM((SUBLANES, tt + CONV_HALO, CONV_CH), F32),
                        pltpu.VMEM((tt, CONV_CH), F32)],
        compiler_params=_cp("arbitrary", "arbitrary"),
        name="conv_module",
    )(p_bf, p_bf, st, dw, dw_b.reshape(1, -1), ln_g.reshape(1, -1), ln_b.reshape(1, -1))
    return y, tail[:, lead:, :]


def _merge_kernel(ap_ref, rp_ref, cp_ref, as_ref, rs_ref, cs_ref, ga_ref, gb_ref, gc_ref, ba_ref, bb_ref, bc_ref,
                  wa_ref, wr_ref, wc_ref, z_ref, *, prompt_tiles):
    ga = jax.nn.sigmoid(ga_ref[...].astype(F32) + ba_ref[...])
    gb = jax.nn.sigmoid(gb_ref[...].astype(F32) + bb_ref[...])
    gc = jax.nn.sigmoid(gc_ref[...].astype(F32) + bc_ref[...])

    def mix(a_ref, r_ref, c_ref):
        ya = jnp.dot(a_ref[...], wa_ref[...], preferred_element_type=F32)
        yb = jnp.dot(r_ref[...], wr_ref[...], preferred_element_type=F32)
        yc = jnp.dot(c_ref[...], wc_ref[...], preferred_element_type=F32)
        z_ref[...] = (ga * ya + gb * yb + gc * yc).astype(z_ref.dtype)

    @pl.when(pl.program_id(1) < prompt_tiles)
    def _():
        mix(ap_ref, rp_ref, cp_ref)

    @pl.when(pl.program_id(1) >= prompt_tiles)
    def _():
        mix(as_ref, rs_ref, cs_ref)


def _merge(att_p, ret_p, cnv_p, att_s, ret_s, cnv_s, p_bf, b_gate, wa, wr, wc, tm, tn):
    np_, ns = att_p.shape[0], att_s.shape[0]
    assert np_ % tm == 0 and ns % tm == 0
    m = np_ + ns
    pt = np_ // tm
    d = wa.shape[1]
    nj = d // tn
    g0 = GATE_OFF // tn
    bg = b_gate.reshape(1, -1)

    def prompt_rows(j, i):
        return (jnp.minimum(i, pt - 1), 0)

    def sample_rows(j, i):
        return (jnp.maximum(i - pt, 0), 0)

    return pl.pallas_call(
        functools.partial(_merge_kernel, prompt_tiles=pt),
        grid=(nj, m // tm),
        in_specs=[pl.BlockSpec((tm, att_p.shape[1]), prompt_rows),
                  pl.BlockSpec((tm, ret_p.shape[1]), prompt_rows),
                  pl.BlockSpec((tm, cnv_p.shape[1]), prompt_rows),
                  pl.BlockSpec((tm, att_s.shape[1]), sample_rows),
                  pl.BlockSpec((tm, ret_s.shape[1]), sample_rows),
                  pl.BlockSpec((tm, cnv_s.shape[1]), sample_rows),
                  pl.BlockSpec((tm, tn), lambda j, i: (i, g0 + j)),
                  pl.BlockSpec((tm, tn), lambda j, i: (i, g0 + nj + j)),
                  pl.BlockSpec((tm, tn), lambda j, i: (i, g0 + 2 * nj + j)),
                  pl.BlockSpec((1, tn), lambda j, i: (0, j)),
                  pl.BlockSpec((1, tn), lambda j, i: (0, nj + j)),
                  pl.BlockSpec((1, tn), lambda j, i: (0, 2 * nj + j)),
                  pl.BlockSpec((wa.shape[0], tn), lambda j, i: (0, j)),
                  pl.BlockSpec((wr.shape[0], tn), lambda j, i: (0, j)),
                  pl.BlockSpec((wc.shape[0], tn), lambda j, i: (0, j))],
        out_specs=pl.BlockSpec((tm, tn), lambda j, i: (i, j)),
        out_shape=jax.ShapeDtypeStruct((m, d), BF16),
        compiler_params=_cp("arbitrary", "arbitrary"),
        name="branch_merge",
    )(att_p, ret_p, cnv_p, att_s, ret_s, cnv_s, p_bf, p_bf, p_bf, bg, bg, bg, wa, wr, wc)


def _outproj_ln_kernel(z_ref, w_ref, x_ref, g_ref, b_ref, of_ref, ob_ref, *, alpha):
    mix = jnp.dot(z_ref[...], w_ref[...], preferred_element_type=F32)
    y = _layer_norm(alpha * x_ref[...] + mix, g_ref[...], b_ref[...])
    of_ref[...] = y
    ob_ref[...] = y.astype(BF16)


def _outproj_ln(z, w_out, x, g, b, alpha, tm):
    m, d = x.shape
    kern = functools.partial(_outproj_ln_kernel, alpha=alpha)
    return pl.pallas_call(
        kern,
        grid=(m // tm,),
        in_specs=[pl.BlockSpec((tm, d), lambda i: (i, 0)),
                  pl.BlockSpec((d, d), lambda i: (0, 0)),
                  pl.BlockSpec((tm, d), lambda i: (i, 0)),
                  pl.BlockSpec((1, d), lambda i: (0, 0)),
                  pl.BlockSpec((1, d), lambda i: (0, 0))],
        out_specs=[pl.BlockSpec((tm, d), lambda i: (i, 0)), pl.BlockSpec((tm, d), lambda i: (i, 0))],
        out_shape=[jax.ShapeDtypeStruct((m, d), F32), jax.ShapeDtypeStruct((m, d), BF16)],
        compiler_params=_cp("arbitrary"),
        name="out_proj_ln1",
    )(z, w_out, x, g.reshape(1, -1), b.reshape(1, -1))


def _router_kernel(x_ref, w_ref, b_ref, gate_ref):
    tm = x_ref.shape[0]
    logits = jnp.dot(x_ref[...], w_ref[...], precision=lax.Precision.HIGHEST, preferred_element_type=F32) + b_ref[...]
    lane = lax.broadcasted_iota(I32, (tm, LANES), 1)
    valid = lane < N_EXPERTS
    lm = jnp.where(valid, logits, NEG_BIG)
    e = jnp.where(valid, jnp.exp(lm - jnp.max(lm, axis=-1, keepdims=True)), 0.0)
    aff = e / jnp.sum(e, axis=-1, keepdims=True)

    def top2(vals):
        v1 = jnp.max(vals, axis=-1, keepdims=True)
        i1 = jnp.min(jnp.where(vals == v1, lane, LANES), axis=-1, keepdims=True)
        rest = jnp.where(lane == i1, -2.0, vals)
        v2 = jnp.max(rest, axis=-1, keepdims=True)
        i2 = jnp.min(jnp.where(rest == v2, lane, LANES), axis=-1, keepdims=True)
        return v1, i1, v2, i2

    grp = lane // EXP_PER_GROUP
    best = jnp.zeros((tm, 1), I32)
    best_score = None
    for g in range(N_GROUPS):
        v1, _, v2, _ = top2(jnp.where(grp == g, aff, -1.0))
        score = v1 + v2
        if g == 0:
            best_score = score
        else:
            better = score > best_score
            best = jnp.where(better, g, best)
            best_score = jnp.where(better, score, best_score)
    v1, i1, v2, i2 = top2(jnp.where(grp == best, aff, -1.0))
    tot = v1 + v2
    gate_ref[...] = (jnp.where(lane == 0, i1.astype(F32), 0.0) + jnp.where(lane == 1, i2.astype(F32), 0.0)
                     + jnp.where(lane == 2, v1 / tot, 0.0) + jnp.where(lane == 3, v2 / tot, 0.0))


def _router(x, w_router, b_router, tm):
    m, d = x.shape
    wr = jnp.pad(w_router, ((0, 0), (0, LANES - N_EXPERTS)))
    br = jnp.pad(b_router, (0, LANES - N_EXPERTS)).reshape(1, -1)
    return pl.pallas_call(
        _router_kernel,
        grid=(m // tm,),
        in_specs=[pl.BlockSpec((tm, d), lambda i: (i, 0)),
                  pl.BlockSpec((d, LANES), lambda i: (0, 0)),
                  pl.BlockSpec((1, LANES), lambda i: (0, 0))],
        out_specs=pl.BlockSpec((tm, LANES), lambda i: (i, 0)),
        out_shape=jax.ShapeDtypeStruct((m, LANES), F32),
        compiler_params=_cp("arbitrary"),
        name="router",
    )(x, wr, br)


MOE_TILE = 256


def _gather_rows(idx_ref, src_hbm, dst_ref, sem, n):
    for r in range(n):
        pltpu.make_async_copy(src_hbm.at[pl.ds(idx_ref[0, 0, r], 1)], dst_ref.at[pl.ds(r, 1)], sem).start(priority=r % 2)


def _wait_rows(src_hbm, dst_ref, sem, n):
    pltpu.make_async_copy(src_hbm.at[pl.ds(0, n)], dst_ref, sem).wait()


def _moe_expert_kernel(te_ref, cur_ref, nxt_ref, x_hbm, w1_ref, w3_ref, w2_ref, o_ref,
                       xg_ref, sem, w1b_ref, w3b_ref, w2b_ref):
    t = pl.program_id(0)
    nt = pl.num_programs(0)
    tile = xg_ref.shape[1]
    slot = t % 2

    @pl.when(t == 0)
    def _():
        _gather_rows(cur_ref, x_hbm, xg_ref.at[0], sem.at[0], tile)

    _gather_rows(nxt_ref, x_hbm, xg_ref.at[1 - slot], sem.at[1 - slot], tile)

    @pl.when((t == 0) | (te_ref[t] != te_ref[jnp.maximum(t - 1, 0)]))
    def _():
        w1b_ref[...] = w1_ref[...].astype(BF16)
        w3b_ref[...] = w3_ref[...].astype(BF16)
        w2b_ref[...] = w2_ref[...].astype(BF16)

    _wait_rows(x_hbm, xg_ref.at[slot], sem.at[slot], tile)
    xb = xg_ref[slot].astype(BF16)
    h1 = jnp.dot(xb, w1b_ref[...], preferred_element_type=F32)
    h3 = jnp.dot(xb, w3b_ref[...], preferred_element_type=F32)
    h = h1 * jax.nn.sigmoid(h1) * h3
    o_ref[...] = jnp.dot(h.astype(BF16), w2b_ref[...], preferred_element_type=F32)

    @pl.when(t == nt - 1)
    def _():
        _wait_rows(x_hbm, xg_ref.at[1 - slot], sem.at[1 - slot], tile)


def _moe_experts(x, w1, w3, w2, tile_expert, src_tiles):
    n, d = x.shape
    _, _, de = w1.shape
    nt = src_tiles.shape[0] - 1
    tile = src_tiles.shape[2]
    grid_spec = pltpu.PrefetchScalarGridSpec(
        num_scalar_prefetch=1,
        grid=(nt,),
        in_specs=[pl.BlockSpec((1, 1, tile), lambda t, te: (t, 0, 0), memory_space=pltpu.SMEM),
                  pl.BlockSpec((1, 1, tile), lambda t, te: (t + 1, 0, 0), memory_space=pltpu.SMEM),
                  pl.BlockSpec(memory_space=pl.ANY),
                  pl.BlockSpec((None, d, de), lambda t, te: (te[t], 0, 0)),
                  pl.BlockSpec((None, d, de), lambda t, te: (te[t], 0, 0)),
                  pl.BlockSpec((None, de, d), lambda t, te: (te[t], 0, 0))],
        out_specs=pl.BlockSpec((tile, d), lambda t, te: (t, 0)),
        scratch_shapes=[pltpu.VMEM((2, tile, d), F32),
                        pltpu.SemaphoreType.DMA((2,)),
                        pltpu.VMEM((d, de), BF16),
                        pltpu.VMEM((d, de), BF16),
                        pltpu.VMEM((de, d), BF16)],
    )
    return pl.pallas_call(
        _moe_expert_kernel,
        grid_spec=grid_spec,
        out_shape=jax.ShapeDtypeStruct((nt * tile, d), F32),
        compiler_params=_cp("arbitrary"),
        name="moe_experts",
    )(tile_expert, src_tiles, src_tiles, x, w1, w3, w2)


def _moe_combine_kernel(cur_ref, nxt_ref, y_hbm, rt_ref, x_ref, g_ref, b_ref, of_ref, ob_ref, yg_ref, sem, *, alpha):
    t = pl.program_id(0)
    nt = pl.num_programs(0)
    tm = x_ref.shape[0]
    slot = t % 2

    @pl.when(t == 0)
    def _():
        _gather_rows(cur_ref, y_hbm, yg_ref.at[0], sem.at[0], 2 * tm)

    _gather_rows(nxt_ref, y_hbm, yg_ref.at[1 - slot], sem.at[1 - slot], 2 * tm)
    _wait_rows(y_hbm, yg_ref.at[slot], sem.at[slot], 2 * tm)
    rt = rt_ref[...]
    moe = rt[:, 2:3] * yg_ref[slot, 0:tm] + rt[:, 3:4] * yg_ref[slot, tm:2 * tm]
    y = _layer_norm(alpha * x_ref[...] + moe, g_ref[...], b_ref[...])
    of_ref[...] = y
    ob_ref[...] = y.astype(BF16)

    @pl.when(t == nt - 1)
    def _():
        _wait_rows(y_hbm, yg_ref.at[1 - slot], sem.at[1 - slot], 2 * tm)


def _moe_combine_ln(y_sorted, pos_tiles, route, x, g, b, alpha, tm):
    m, d = x.shape
    kern = functools.partial(_moe_combine_kernel, alpha=alpha)
    return pl.pallas_call(
        kern,
        grid=(m // tm,),
        in_specs=[pl.BlockSpec((1, 1, 2 * tm), lambda t: (t, 0, 0), memory_space=pltpu.SMEM),
                  pl.BlockSpec((1, 1, 2 * tm), lambda t: (t + 1, 0, 0), memory_space=pltpu.SMEM),
                  pl.BlockSpec(memory_space=pl.ANY),
                  pl.BlockSpec((tm, LANES), lambda t: (t, 0)),
                  pl.BlockSpec((tm, d), lambda t: (t, 0)),
                  pl.BlockSpec((1, d), lambda t: (0, 0)),
                  pl.BlockSpec((1, d), lambda t: (0, 0))],
        out_specs=[pl.BlockSpec((tm, d), lambda t: (t, 0)), pl.BlockSpec((tm, d), lambda t: (t, 0))],
        out_shape=[jax.ShapeDtypeStruct((m, d), F32), jax.ShapeDtypeStruct((m, d), BF16)],
        scratch_shapes=[pltpu.VMEM((2, 2 * tm, d), F32), pltpu.SemaphoreType.DMA((2,))],
        compiler_params=_cp("arbitrary"),
        name="moe_combine_ln2",
    )(pos_tiles, pos_tiles, y_sorted, route, x, g.reshape(1, -1), b.reshape(1, -1))


def _moe_layout(route, tile, tm):
    n = route.shape[0]
    e_flat = route[:, 0:2].astype(I32).reshape(-1)
    npair = 2 * n
    nt = -(-npair // tile) + N_EXPERTS
    onehot = (e_flat[:, None] == jnp.arange(N_EXPERTS, dtype=I32)[None, :]).astype(I32)
    csum = jnp.cumsum(onehot, axis=0)
    counts = csum[-1]
    rank = jnp.sum((csum - onehot) * onehot, axis=1)
    padded = (counts + tile - 1) // tile * tile
    ends = jnp.cumsum(padded)
    starts = ends - padded
    pos = starts[e_flat] + rank
    rows = nt * tile
    src = jnp.zeros((rows,), I32).at[pos].set(jnp.arange(npair, dtype=I32) // 2, unique_indices=True)
    tile_expert = jnp.minimum(jnp.searchsorted(ends, jnp.arange(nt, dtype=I32) * tile, side='right'),
                              N_EXPERTS - 1).astype(I32)
    src_tiles = jnp.concatenate([src, jnp.zeros((tile,), I32)]).reshape(nt + 1, 1, tile)
    pos_t = pos.reshape(n // tm, tm, 2).transpose(0, 2, 1).reshape(n // tm, 1, 2 * tm)
    pos_tiles = jnp.concatenate([pos_t, jnp.zeros((1, 1, 2 * tm), I32)], axis=0)
    return tile_expert, src_tiles, pos_tiles


def kernel(x_prompt, x_sample, cache_k, cache_v, cache_idx_k, state_ret, state_conv, w_in, b_gate, w_att_o, ret_gn_g,
           w_ret_o, conv_dw, conv_dw_b, conv_ln_g, conv_ln_b, w_conv_o, w_out, ln1_g, ln1_b, w_router, b_router,
           moe_w1, moe_w3, moe_w2, ln2_g, ln2_b):
    nbp, tp, d = x_prompt.shape
    nbs, ts, _ = x_sample.shape
    depth = w_in.shape[0]
    past = cache_k.shape[2]
    np_, ns = nbp * tp, nbs * ts
    n = np_ + ns
    alpha = (2 * depth) ** 0.25
    kvw = KV_HEADS * ATT_HD

    tm = _pow2_tile(n, 512)
    tq_p = _pow2_tile(tp, 256)
    cl_p = _pow2_tile(tp, 256)
    tt_p = _pow2_tile(tp, 256)
    ls_true = past + ts
    ls_pad = -(-ls_true // (2 * LANES)) * (2 * LANES)
    assert tp % (2 * LANES) == 0 and np_ % max(tq_p, cl_p, tt_p, ts) == 0

    x = jnp.concatenate([x_prompt.reshape(np_, d), x_sample.reshape(ns, d)], axis=0)
    xb = x.astype(BF16)

    c1 = ATT_HEADS * ATT_HD
    c2 = c1 + F32_USED
    ks, vs, iks, rps, cps, kss, vss, ikss, rss, css = ([] for _ in range(10))
    zero_ret = jnp.zeros((nbp, RET_HEADS, RET_DK, RET_DV), F32)
    zero_conv = jnp.zeros((nbp, CONV_W - 1, CONV_CH), F32)

    for l in range(depth):
        w_l = w_in[l]
        w_bf = jnp.concatenate([w_l[:, c2:], w_l[:, :c1]], axis=1).astype(BF16)
        w_f = jnp.pad(w_l[:, c1:c2], ((0, 0), (0, F32_WIDTH - F32_USED))).astype(BF16)
        p_bf = _matmul(xb, w_bf, BF16, tm, 1536)
        p_f = _matmul(xb, w_f, F32, tm, F32_WIDTH)

        k_new = p_f[:, KA_OFF:KA_OFF + kvw]
        v_new = p_f[:, VA_OFF:VA_OFF + kvw]
        ik_new = p_f[:, KI_OFF:KI_OFF + IDX_DIM]
        wi_t = p_f[:, WI_OFF:WI_OFF + IDX_HEADS].T
        ks.append(k_new[:np_].reshape(nbp, tp, KV_HEADS, ATT_HD))
        vs.append(v_new[:np_].reshape(nbp, tp, KV_HEADS, ATT_HD))
        iks.append(ik_new[:np_].reshape(nbp, tp, IDX_DIM))
        kss.append(k_new[np_:].reshape(nbs, ts, KV_HEADS, ATT_HD))
        vss.append(v_new[np_:].reshape(nbs, ts, KV_HEADS, ATT_HD))
        ikss.append(ik_new[np_:].reshape(nbs, ts, IDX_DIM))

        nq = tp // tq_p
        hq = ATT_HEADS * ATT_HD
        qa_blk = (GATE_OFF + 3 * d) // hq
        att_p = _attention(
            p_bf, lambda tq: ((tq, hq), lambda b, j: (b * nq + j, qa_blk)),
            p_f, lambda lp: ((lp, kvw), lambda b, j: (b, KA_OFF // kvw)),
            p_f, lambda lp: ((lp, kvw), lambda b, j: (b, VA_OFF // kvw)),
            p_f, lambda tq: ((tq, IDX_HEADS * IDX_DIM), lambda b, j: (b * nq + j, QI_OFF // (IDX_HEADS * IDX_DIM))),
            p_f, lambda lp: ((lp, LANES), lambda b, j: (b, KI_OFF // LANES)),
            wi_t[:, :np_], lambda tq: ((IDX_HEADS, tq), lambda b, j: (0, b * nq + j)),
            nb=nbp, t=tp, l_pad=tp, l_true=tp, offset=0, tq=tq_p)

        mask_s = _sample_select(cache_idx_k[l], p_f, wi_t, np_, nbs, ts, past)
        att_s = _sample_attend(p_bf, qa_blk, cache_k[l].reshape(nbs, past, kvw), cache_v[l].reshape(nbs, past, kvw),
                               p_f, mask_s, np_, nbs, ts, past)

        ret_p, rs_p = _retention(p_bf, 0, nbp, tp, cl_p, 0, zero_ret, ret_gn_g[l])
        ret_s, rs_s = _retention(p_bf, np_, nbs, ts, ts, past, state_ret[l], ret_gn_g[l])
        rps.append(rs_p)
        rss.append(rs_s)

        cnv_p, cs_p = _conv_module(p_bf, 0, nbp, tp, tt_p, zero_conv, conv_dw[l], conv_dw_b[l], conv_ln_g[l],
                                   conv_ln_b[l])
        cnv_s, cs_s = _conv_module(p_bf, np_, nbs, ts, ts, state_conv[l], conv_dw[l], conv_dw_b[l], conv_ln_g[l],
                                   conv_ln_b[l])
        cps.append(cs_p)
        css.append(cs_s)

        z = _merge(att_p, ret_p, cnv_p, att_s, ret_s, cnv_s, p_bf, b_gate[l], w_att_o[l].astype(BF16),
                   w_ret_o[l].astype(BF16), w_conv_o[l].astype(BF16), math.gcd(tm, ns), 512)
        x, xb = _outproj_ln(z, w_out[l].astype(BF16), x, ln1_g[l], ln1_b[l], alpha, _pow2_tile(n, 256))
        route = _router(x, w_router, b_router, _pow2_tile(n, 256))
        tm_c = _pow2_tile(n, 256)
        tile_expert, src_tiles, pos_tiles = _moe_layout(route, MOE_TILE, tm_c)
        y_sorted = _moe_experts(x, moe_w1[l], moe_w3[l], moe_w2[l], tile_expert, src_tiles)
        x, xb = _moe_combine_ln(y_sorted, pos_tiles, route, x, ln2_g[l], ln2_b[l], alpha, tm_c)

    y_prompt = x[:np_].reshape(nbp, tp, d)
    y_sample = x[np_:].reshape(nbs, ts, d)
    st = jnp.stack
    return (y_prompt, y_sample, st(ks), st(vs), st(iks), st(rps), st(cps), st(kss), st(vss), st(ikss), st(rss), st(css))
```

```python
import functools
import math

import jax
import jax.numpy as jnp
from jax import lax
from jax.experimental import pallas as pl
from jax.experimental.pallas import tpu as pltpu

F32 = jnp.float32
BF16 = jnp.bfloat16
I32 = jnp.int32

CHUNK = 64
TOPK_MAX = 256
ATT_HEADS, ATT_HD, KV_HEADS = 8, 128, 2
KV_GROUP = ATT_HEADS // KV_HEADS
IDX_HEADS, IDX_DIM = 8, 64
RET_HEADS, RET_DK, RET_DV = 8, 128, 256
ROPE_BASE = 10000.0
CONV_CH, CONV_W = 1024, 31
N_EXPERTS, N_GROUPS = 16, 4
EXP_PER_GROUP = N_EXPERTS // N_GROUPS
LN_EPS = 1e-5

LANES = 128
SUBLANES = 8

INT_MIN = -2 ** 31
NEG_BIG = -1e30

QR_OFF = 0
KR_OFF = QR_OFF + RET_HEADS * RET_DK
VR_OFF = KR_OFF + RET_HEADS * RET_DK
GR_OFF = VR_OFF + RET_HEADS * RET_DV
CIN_OFF = GR_OFF + RET_HEADS * RET_DV
GATE_OFF = CIN_OFF + 2 * CONV_CH
KA_OFF = 0
VA_OFF = KA_OFF + KV_HEADS * ATT_HD
QI_OFF = VA_OFF + KV_HEADS * ATT_HD
KI_OFF = QI_OFF + IDX_HEADS * IDX_DIM
WI_OFF = KI_OFF + IDX_DIM
F32_USED = WI_OFF + IDX_HEADS
F32_WIDTH = -(-F32_USED // LANES) * LANES

VMEM_LIMIT = 56 * 1024 * 1024


def _cp(*sem):
    return pltpu.CompilerParams(dimension_semantics=sem, vmem_limit_bytes=VMEM_LIMIT)


def _pow2_tile(n, pref):
    t = pref
    while n % t:
        t //= 2
    return t


def _layer_norm(x, g, b):
    mu = jnp.mean(x, axis=-1, keepdims=True)
    xc = x - mu
    var = jnp.mean(xc * xc, axis=-1, keepdims=True)
    return xc * lax.rsqrt(var + LN_EPS) * g + b


def _mm_kernel(x_ref, w_ref, o_ref):
    o_ref[...] = jnp.dot(x_ref[...], w_ref[...], preferred_element_type=F32).astype(o_ref.dtype)


def _matmul(x, w, out_dtype, tm, tn):
    m, k = x.shape
    n = w.shape[1]
    return pl.pallas_call(
        _mm_kernel,
        grid=(n // tn, m // tm),
        in_specs=[pl.BlockSpec((tm, k), lambda j, i: (i, 0)),
                  pl.BlockSpec((k, tn), lambda j, i: (0, j))],
        out_specs=pl.BlockSpec((tm, tn), lambda j, i: (i, j)),
        out_shape=jax.ShapeDtypeStruct((m, n), out_dtype),
        compiler_params=_cp("arbitrary", "arbitrary"),
        name="in_proj",
    )(x, w)


def _attn_kernel(q_ref, k_ref, v_ref, qi_ref, ki_ref, wi_ref, o_ref,
                 key_ref, lg_ref, mx_ref, acc_ref, m_ref,
                 *, tq, nq, ck, nck, offset, l_true, k_sel, tie_bits):
    j = pl.program_id(1)
    nsub = ck // LANES
    qlane = lax.broadcasted_iota(I32, (1, tq), 1)
    krow = lax.broadcasted_iota(I32, (ck, tq), 0)
    pos = offset + j * tq + qlane
    limit = jnp.minimum((pos // CHUNK + 1) * CHUNK, l_true)
    if nq == 1:
        n_act = (min(((offset + tq - 1) // CHUNK + 1) * CHUNK, l_true) + ck - 1) // ck
    else:
        last_pos = offset + (j + 1) * tq - 1
        max_limit = jnp.minimum((last_pos // CHUNK + 1) * CHUNK, l_true)
        n_act = (max_limit + ck - 1) // ck

    qi = (qi_ref[...] * (IDX_DIM ** -0.5)).astype(BF16)
    qis = [qi[:, h * IDX_DIM:(h + 1) * IDX_DIM] for h in range(IDX_HEADS)]
    stacked = tq % LANES == 0
    if stacked:
        qstack = jnp.concatenate(qis, axis=0)
    wt = wi_ref[...] * (IDX_HEADS ** -0.5)
    wrows = [wt[h:h + 1, :] for h in range(IDX_HEADS)]
    nt = (((1,), (1,)), ((), ()))

    def score_body(c, carry):
        for u in range(nsub):
            r0 = pl.multiple_of(c * ck + u * LANES, LANES)
            kiu = ki_ref[pl.ds(r0, LANES), :][:, :IDX_DIM].astype(BF16)
            if stacked:
                d_all = lax.dot_general(kiu, qstack, nt, preferred_element_type=F32)
                ds = [d_all[:, h * tq:(h + 1) * tq] for h in range(IDX_HEADS)]
            else:
                ds = [lax.dot_general(kiu, qis[h], nt, preferred_element_type=F32) for h in range(IDX_HEADS)]
            s = jnp.zeros((LANES, tq), F32)
            for h in range(IDX_HEADS):
                s = s + jnp.maximum(ds[h], 0.0) * wrows[h]
            bits = lax.bitcast_convert_type(s, I32)
            bits = jnp.where(bits == INT_MIN, 0, bits)
            key = jnp.where(bits >= 0, bits, bits ^ 0x7FFFFFFF)
            kpos = r0 + lax.broadcasted_iota(I32, (LANES, tq), 0)
            key_ref[c, u * LANES:(u + 1) * LANES, :] = jnp.where(kpos < limit, key, INT_MIN)
        return carry

    lax.fori_loop(0, n_act, score_body, 0)

    def count(pred):
        def body(c, acc):
            p = jnp.where(pred(key_ref[c], c * ck + krow), 1.0, 0.0)
            parts = [p[r * SUBLANES:(r + 1) * SUBLANES] for r in range(ck // SUBLANES)]
            while len(parts) > 1:
                parts = [a + b for a, b in zip(parts[0::2], parts[1::2])]
            return acc + parts[0]
        acc = lax.fori_loop(0, n_act, body, jnp.zeros((SUBLANES, tq), F32))
        return jnp.sum(acc, axis=0, keepdims=True)

    def bisect(i, tu):
        cand_u = tu | (jnp.int32(1) << (31 - i))
        cand_s = cand_u ^ INT_MIN
        cnt = count(lambda key, idx: key >= cand_s)
        return jnp.where(cnt >= k_sel, cand_u, tu)

    tu = lax.fori_loop(0, 32, bisect, jnp.zeros((1, tq), I32))
    ts = tu ^ INT_MIN
    c_gt = count(lambda key, idx: key > ts)
    c_ge = count(lambda key, idx: key >= ts)
    want = k_sel - c_gt

    eye = jnp.where(lax.broadcasted_iota(I32, (tq, tq), 0) == lax.broadcasted_iota(I32, (tq, tq), 1), 1.0, 0.0).astype(BF16)
    ts_adm = jnp.maximum(ts, INT_MIN + 1)
    m_ref[...] = jnp.full((1, tq), 2 ** 30, I32)

    @pl.when(jnp.max(c_ge) > k_sel)
    def _():
        def tie_bisect(i, m):
            cand = m | (jnp.int32(1) << (tie_bits - 1 - i))
            cnt = count(lambda key, idx: (key == ts) & (idx < cand))
            return jnp.where(cnt < want, cand, m)
        m_ref[...] = lax.fori_loop(0, tie_bits, tie_bisect, jnp.zeros((1, tq), I32))

    m_last = m_ref[...]

    qscale = (ATT_HD ** -0.5) * 1.4426950408889634
    groups = range(KV_HEADS)
    qss = []
    for g in groups:
        qs = jnp.concatenate(
            [q_ref[:, (g * KV_GROUP + r) * ATT_HD:(g * KV_GROUP + r + 1) * ATT_HD] for r in range(KV_GROUP)], axis=0)
        qss.append((qs.astype(F32) * qscale).astype(BF16))
    mx_ref[...] = jnp.full(mx_ref.shape, NEG_BIG, F32)

    def logit_body(c, carry):
        r0 = pl.multiple_of(c * ck, ck)
        key = key_ref[c]
        sel = (key > ts_adm) | ((key == ts_adm) & (c * ck + krow <= m_last))
        sel_t = jnp.where(sel, 1.0, 0.0).astype(BF16)
        sel_q = lax.dot_general(eye, sel_t, nt, preferred_element_type=F32)
        b = (sel_q - 1.0) * (-NEG_BIG)
        b4 = jnp.concatenate([b] * KV_GROUP, axis=0)
        for g in groups:
            kc = k_ref[pl.ds(r0, ck), g * ATT_HD:(g + 1) * ATT_HD].astype(BF16)
            lg = lax.dot_general(qss[g], kc, nt, preferred_element_type=F32) + b4
            lg_ref[g, c] = lg
            m = mx_ref[g]
            for u in range(nsub):
                m = jnp.maximum(m, lg[:, u * LANES:(u + 1) * LANES])
            mx_ref[g] = m
        return carry

    lax.fori_loop(0, n_act, logit_body, 0)
    m_rows = [jnp.max(mx_ref[g], axis=-1, keepdims=True) for g in groups]
    acc_ref[...] = jnp.zeros(acc_ref.shape, F32)
    ones_cols = jnp.ones((ck, ATT_HD), BF16)

    def pv_body(c, carry):
        r0 = pl.multiple_of(c * ck, ck)
        for g in groups:
            p = jnp.exp2(lg_ref[g, c] - m_rows[g]).astype(BF16)
            vc = v_ref[pl.ds(r0, ck), g * ATT_HD:(g + 1) * ATT_HD].astype(BF16)
            acc_ref[g] += jnp.dot(p, jnp.concatenate([vc, ones_cols], axis=1), preferred_element_type=F32)
        return carry

    lax.fori_loop(0, n_act, pv_body, 0)
    for g in groups:
        a = acc_ref[g]
        o = a[:, :ATT_HD] / a[:, ATT_HD:]
        for r in range(KV_GROUP):
            h = g * KV_GROUP + r
            o_ref[:, h * ATT_HD:(h + 1) * ATT_HD] = o[r * tq:(r + 1) * tq].astype(o_ref.dtype)


def _attention(q_arr, q_map, k_arr, k_map, v_arr, v_map, qi_arr, qi_map, ki_arr, ki_map, wi_arr, wi_map,
               *, nb, t, l_pad, l_true, offset, tq):
    ck = 2 * LANES
    nck = l_pad // ck
    assert l_pad % ck == 0 and t % tq == 0 and nck * (ck // LANES) < 256
    nq = t // tq
    k_sel = min(TOPK_MAX, l_true // 4)
    tie_bits = max(1, (l_pad - 1).bit_length())
    kern = functools.partial(_attn_kernel, tq=tq, nq=nq, ck=ck, nck=nck, offset=offset, l_true=l_true,
                             k_sel=k_sel, tie_bits=tie_bits)
    hq = ATT_HEADS * ATT_HD
    return pl.pallas_call(
        kern,
        grid=(nb, nq),
        in_specs=[pl.BlockSpec(*q_map(tq)), pl.BlockSpec(*k_map(l_pad)), pl.BlockSpec(*v_map(l_pad)),
                  pl.BlockSpec(*qi_map(tq)), pl.BlockSpec(*ki_map(l_pad)), pl.BlockSpec(*wi_map(tq))],
        out_specs=pl.BlockSpec((tq, hq), lambda b, j: (b * nq + j, 0)),
        out_shape=jax.ShapeDtypeStruct((nb * t, hq), BF16),
        scratch_shapes=[pltpu.VMEM((nck, ck, tq), I32),
                        pltpu.VMEM((KV_HEADS, nck, KV_GROUP * tq, ck), F32),
                        pltpu.VMEM((KV_HEADS, KV_GROUP * tq, LANES), F32),
                        pltpu.VMEM((KV_HEADS, KV_GROUP * tq, 2 * ATT_HD), F32),
                        pltpu.VMEM((1, tq), I32)],
        compiler_params=_cp("arbitrary", "arbitrary"),
        name="dsa_attention",
    )(q_arr, k_arr, v_arr, qi_arr, ki_arr, wi_arr)


def _order_key(s):
    bits = lax.bitcast_convert_type(s, I32)
    bits = jnp.where(bits == INT_MIN, 0, bits)
    return jnp.where(bits >= 0, bits, bits ^ 0x7FFFFFFF)


def _sample_select_kernel(kic_ref, new_ref, qi_ref, wi_ref, mask_ref, key_ref, keyn_ref, m_ref,
                          *, ts, gs, past, ck, k_sel, tie_bits):
    nck = past // ck
    nsub = ck // LANES
    l_true = past + ts
    lane = lax.broadcasted_iota(I32, (1, LANES), 1)
    lane_s = lane // ts
    pos = past + lane % ts
    limit = jnp.minimum((pos // CHUNK + 1) * CHUNK, l_true)
    nt = (((1,), (1,)), ((), ()))

    qi = (qi_ref[...] * (IDX_DIM ** -0.5)).astype(BF16)
    qstack = jnp.concatenate([qi[:, h * IDX_DIM:(h + 1) * IDX_DIM] for h in range(IDX_HEADS)], axis=0)
    wt = wi_ref[...] * (IDX_HEADS ** -0.5)
    wrows = [wt[h:h + 1, :] for h in range(IDX_HEADS)]

    def own_stream(d, rows):
        out = d[(gs - 1) * rows:gs * rows]
        for s in range(gs - 2, -1, -1):
            out = jnp.where(lane_s == s, d[s * rows:(s + 1) * rows], out)
        return out

    def scores(kmat, rows):
        d_all = lax.dot_general(kmat, qstack, nt, preferred_element_type=F32)
        s = jnp.zeros((rows, LANES), F32)
        for h in range(IDX_HEADS):
            s = s + jnp.maximum(own_stream(d_all[:, h * LANES:(h + 1) * LANES], rows), 0.0) * wrows[h]
        return _order_key(s)

    def score_body(c, carry):
        for u in range(nsub):
            r0 = pl.multiple_of(c * ck + u * LANES, LANES)
            kmat = jnp.concatenate([kic_ref[s, pl.ds(r0, LANES), :] for s in range(gs)], axis=0).astype(BF16)
            kpos = r0 + lax.broadcasted_iota(I32, (LANES, LANES), 0)
            key_ref[c, u * LANES:(u + 1) * LANES, :] = jnp.where(kpos < limit, scores(kmat, LANES), INT_MIN)
        return carry

    lax.fori_loop(0, nck, score_body, 0)
    kn = new_ref[:, :IDX_DIM].astype(BF16)
    npos = past + lax.broadcasted_iota(I32, (ts, LANES), 0)
    keyn_ref[...] = jnp.where(npos < limit, scores(kn, ts), INT_MIN)

    krow = lax.broadcasted_iota(I32, (ck, LANES), 0)

    def tree(p, rows):
        parts = [p[r * SUBLANES:(r + 1) * SUBLANES] for r in range(rows // SUBLANES)]
        while len(parts) > 1:
            parts = [a + b for a, b in zip(parts[0::2], parts[1::2])]
        return parts[0]

    def count(pred):
        def body(c, acc):
            p = jnp.where(pred(key_ref[c], c * ck + krow), 1.0, 0.0)
            return acc + tree(p, ck)
        acc = lax.fori_loop(0, nck, body, jnp.zeros((SUBLANES, LANES), F32))
        acc = acc + tree(jnp.where(pred(keyn_ref[...], npos), 1.0, 0.0), ts)
        return jnp.sum(acc, axis=0, keepdims=True)

    def bisect(i, tu):
        cand_u = tu | (jnp.int32(1) << (31 - i))
        cand_s = cand_u ^ INT_MIN
        cnt = count(lambda key, idx: key >= cand_s)
        return jnp.where(cnt >= k_sel, cand_u, tu)

    tu = lax.fori_loop(0, 32, bisect, jnp.zeros((1, LANES), I32))
    ts_ = tu ^ INT_MIN
    c_gt = count(lambda key, idx: key > ts_)
    c_ge = count(lambda key, idx: key >= ts_)
    want = k_sel - c_gt
    m_ref[...] = jnp.full((1, LANES), 2 ** 30, I32)

    @pl.when(jnp.max(c_ge) > k_sel)
    def _():
        def tie_bisect(i, m):
            cand = m | (jnp.int32(1) << (tie_bits - 1 - i))
            cnt = count(lambda key, idx: (key == ts_) & (idx < cand))
            return jnp.where(cnt < want, cand, m)
        m_ref[...] = lax.fori_loop(0, tie_bits, tie_bisect, jnp.zeros((1, LANES), I32))

    m_last = m_ref[...]
    ts_adm = jnp.maximum(ts_, INT_MIN + 1)
    eye = jnp.where(lax.broadcasted_iota(I32, (LANES, LANES), 0) == lax.broadcasted_iota(I32, (LANES, LANES), 1),
                    1.0, 0.0).astype(BF16)

    def selected(key, idx):
        sel = (key > ts_adm) | ((key == ts_adm) & (idx <= m_last))
        return jnp.where(sel, 1.0, 0.0).astype(BF16)

    for c in range(nck):
        sel_q = lax.dot_general(eye, selected(key_ref[c], c * ck + krow), nt, preferred_element_type=F32)
        mask_ref[:, c * ck:(c + 1) * ck] = sel_q.astype(BF16)
    sel_n = lax.dot_general(eye, selected(keyn_ref[...], npos), nt, preferred_element_type=F32)
    mask_ref[:, past:past + LANES] = jnp.concatenate(
        [sel_n, jnp.zeros((LANES, LANES - ts), F32)], axis=1).astype(BF16)


def _sample_select(cache_ik, p_f, wi_t, np_, nbs, ts, past):
    gs = LANES // ts
    ck = 2 * LANES
    assert LANES % ts == 0 and nbs % gs == 0 and past % ck == 0 and np_ % LANES == 0 and ts % SUBLANES == 0
    l_true = past + ts
    k_sel = min(TOPK_MAX, l_true // 4)
    kern = functools.partial(_sample_select_kernel, ts=ts, gs=gs, past=past, ck=ck, k_sel=k_sel,
                             tie_bits=max(1, (l_true - 1).bit_length()))
    rb = np_ // LANES
    return pl.pallas_call(
        kern,
        grid=(nbs // gs,),
        in_specs=[pl.BlockSpec((gs, past, IDX_DIM), lambda g: (g, 0, 0)),
                  pl.BlockSpec((LANES, LANES), lambda g: (rb + g, KI_OFF // LANES)),
                  pl.BlockSpec((LANES, IDX_HEADS * IDX_DIM), lambda g: (rb + g, QI_OFF // (IDX_HEADS * IDX_DIM))),
                  pl.BlockSpec((IDX_HEADS, LANES), lambda g: (0, rb + g))],
        out_specs=pl.BlockSpec((LANES, past + LANES), lambda g: (g, 0)),
        out_shape=jax.ShapeDtypeStruct((nbs * ts, past + LANES), BF16),
        scratch_shapes=[pltpu.VMEM((past // ck, ck, LANES), I32),
                        pltpu.VMEM((ts, LANES), I32),
                        pltpu.VMEM((1, LANES), I32)],
        compiler_params=_cp("arbitrary"),
        name="sample_select",
    )(cache_ik, p_f, p_f, wi_t)


def _sample_attend_kernel(q_ref, kc_ref, vc_ref, kn_ref, vn_ref, mask_ref, o_ref, lg_ref, *, ts, past, ca):
    nt = (((1,), (1,)), ((), ()))
    qscale = (ATT_HD ** -0.5) * 1.4426950408889634
    nca = past // ca
    rows = KV_GROUP * ts

    def bias_of(m):
        b = (m.astype(F32) - 1.0) * (-NEG_BIG)
        return jnp.concatenate([b] * KV_GROUP, axis=0)

    bias_n = bias_of(mask_ref[:, past:past + ts])
    for g in range(KV_HEADS):
        qs = jnp.concatenate(
            [q_ref[:, (g * KV_GROUP + r) * ATT_HD:(g * KV_GROUP + r + 1) * ATT_HD] for r in range(KV_GROUP)], axis=0)
        qs = (qs.astype(F32) * qscale).astype(BF16)
        cols = slice(g * ATT_HD, (g + 1) * ATT_HD)
        lg_n = lax.dot_general(qs, kn_ref[:, cols].astype(BF16), nt, preferred_element_type=F32) + bias_n
        m_run = jnp.full((rows, LANES), NEG_BIG, F32)
        for c in range(nca):
            kc = kc_ref[c * ca:(c + 1) * ca, cols].astype(BF16)
            lg = lax.dot_general(qs, kc, nt, preferred_element_type=F32) + bias_of(mask_ref[:, c * ca:(c + 1) * ca])
            lg_ref[g, :, c * ca:(c + 1) * ca] = lg
            for u in range(ca // LANES):
                m_run = jnp.maximum(m_run, lg[:, u * LANES:(u + 1) * LANES])
        m_row = jnp.maximum(jnp.max(m_run, axis=-1, keepdims=True), jnp.max(lg_n, axis=-1, keepdims=True))
        ones_n = jnp.ones((ts, ATT_HD), BF16)
        acc = jnp.dot(jnp.exp2(lg_n - m_row).astype(BF16),
                      jnp.concatenate([vn_ref[:, cols].astype(BF16), ones_n], axis=1), preferred_element_type=F32)
        ones_c = jnp.ones((ca, ATT_HD), BF16)
        for c in range(nca):
            p = jnp.exp2(lg_ref[g, :, c * ca:(c + 1) * ca] - m_row).astype(BF16)
            vc = vc_ref[c * ca:(c + 1) * ca, cols].astype(BF16)
            acc = acc + jnp.dot(p, jnp.concatenate([vc, ones_c], axis=1), preferred_element_type=F32)
        o = acc[:, :ATT_HD] / acc[:, ATT_HD:]
        for r in range(KV_GROUP):
            h = g * KV_GROUP + r
            o_ref[:, h * ATT_HD:(h + 1) * ATT_HD] = o[r * ts:(r + 1) * ts].astype(o_ref.dtype)


def _sample_attend(p_bf, qa_blk, cache_k, cache_v, p_f, mask, np_, nbs, ts, past):
    kvw = KV_HEADS * ATT_HD
    hq = ATT_HEADS * ATT_HD
    ca = 4 * LANES
    assert past % ca == 0
    rb = np_ // ts
    kern = functools.partial(_sample_attend_kernel, ts=ts, past=past, ca=ca)
    return pl.pallas_call(
        kern,
        grid=(nbs,),
        in_specs=[pl.BlockSpec((ts, hq), lambda b: (rb + b, qa_blk)),
                  pl.BlockSpec((None, past, kvw), lambda b: (b, 0, 0)),
                  pl.BlockSpec((None, past, kvw), lambda b: (b, 0, 0)),
                  pl.BlockSpec((ts, kvw), lambda b: (rb + b, KA_OFF // kvw)),
                  pl.BlockSpec((ts, kvw), lambda b: (rb + b, VA_OFF // kvw)),
                  pl.BlockSpec((ts, past + LANES), lambda b: (b, 0))],
        out_specs=pl.BlockSpec((ts, hq), lambda b: (b, 0)),
        out_shape=jax.ShapeDtypeStruct((nbs * ts, hq), BF16),
        scratch_shapes=[pltpu.VMEM((KV_HEADS, KV_GROUP * ts, past), F32)],
        compiler_params=_cp("arbitrary"),
        name="sample_attend",
    )(p_bf, cache_k, cache_v, p_f, p_f, mask)


def _ret_kernel(q_ref, k_ref, v_ref, g_ref, cs_ref, sn_ref, dec_ref, xi_ref, zt_ref, gp_ref, gn_ref, s0_ref,
                y_ref, sf_ref, s_ref):
    c = pl.program_id(1)
    dk, dv = RET_DK, RET_DV

    @pl.when(c == 0)
    def _():
        s_ref[...] = s0_ref[...]

    cs = cs_ref[...]
    sn = sn_ref[...]

    def rot(x):
        xf = x.astype(F32)
        return xf * cs + pltpu.roll(xf, dk // 2, 1) * sn

    for h in range(RET_HEADS):
        q = rot(q_ref[:, h * dk:(h + 1) * dk])
        k = rot(k_ref[:, h * dk:(h + 1) * dk]) * (dk ** -0.5)
        qb = q.astype(BF16)
        kb = k.astype(BF16)
        v = v_ref[:, h * dv:(h + 1) * dv]
        inner = lax.dot_general(qb, kb, (((1,), (1,)), ((), ())), preferred_element_type=F32) * dec_ref[h]
        s = s_ref[h]
        o = jnp.dot(inner.astype(BF16), v, preferred_element_type=F32)
        o = o + jnp.dot(qb, s.astype(BF16), preferred_element_type=F32) * xi_ref[h]
        kz = (k * zt_ref[h]).T.astype(BF16)
        s_new = s * gp_ref[h] + jnp.dot(kz, v, preferred_element_type=F32)
        s_ref[h] = s_new
        sf_ref[h] = s_new

        mu = jnp.mean(o, axis=-1, keepdims=True)
        oc = o - mu
        var = jnp.mean(oc * oc, axis=-1, keepdims=True)
        yn = oc * lax.rsqrt(var + LN_EPS) * gn_ref[:, h * dv:(h + 1) * dv]
        gg = g_ref[:, h * dv:(h + 1) * dv].astype(F32)
        y_ref[:, h * dv:(h + 1) * dv] = (gg * jax.nn.sigmoid(gg) * yn).astype(y_ref.dtype)


def _retention(p_bf, row0, nb, t, cl, offset, s0, gn_g):
    nc = t // cl
    h_, dk, dv = RET_HEADS, RET_DK, RET_DV
    half = dk // 2
    pos = (offset + jnp.arange(t, dtype=I32)).astype(F32)
    inv = 1.0 / (ROPE_BASE ** (jnp.arange(half, dtype=F32) / half))
    ang = pos[:, None] * inv[None, :]
    cos, sin = jnp.cos(ang), jnp.sin(ang)
    cs2 = jnp.concatenate([cos, cos], axis=-1)
    sn2 = jnp.concatenate([-sin, sin], axis=-1)
    log_g = jnp.log1p(-jnp.exp2(-5.0 - jnp.arange(h_, dtype=F32)))
    n = jnp.arange(cl, dtype=F32)
    diff = n[:, None] - n[None, :]
    decay = jnp.where(diff >= 0, jnp.exp(log_g[:, None, None] * jnp.maximum(diff, 0.0)), 0.0)
    xi = jnp.exp(log_g[:, None] * (n + 1.0))[..., None]
    zeta = jnp.exp(log_g[:, None] * (cl - 1.0 - n))[..., None]
    gpow = jnp.broadcast_to(jnp.exp(log_g * cl)[:, None, None], (h_, 1, dv))
    rb = row0 // cl
    return pl.pallas_call(
        _ret_kernel,
        grid=(nb, nc),
        in_specs=[pl.BlockSpec((cl, h_ * dk), lambda b, c: (rb + b * nc + c, QR_OFF // (h_ * dk))),
                  pl.BlockSpec((cl, h_ * dk), lambda b, c: (rb + b * nc + c, KR_OFF // (h_ * dk))),
                  pl.BlockSpec((cl, h_ * dv), lambda b, c: (rb + b * nc + c, VR_OFF // (h_ * dv))),
                  pl.BlockSpec((cl, h_ * dv), lambda b, c: (rb + b * nc + c, GR_OFF // (h_ * dv))),
                  pl.BlockSpec((cl, dk), lambda b, c: (c, 0)),
                  pl.BlockSpec((cl, dk), lambda b, c: (c, 0)),
                  pl.BlockSpec((h_, cl, cl), lambda b, c: (0, 0, 0)),
                  pl.BlockSpec((h_, cl, 1), lambda b, c: (0, 0, 0)),
                  pl.BlockSpec((h_, cl, 1), lambda b, c: (0, 0, 0)),
                  pl.BlockSpec((h_, 1, dv), lambda b, c: (0, 0, 0)),
                  pl.BlockSpec((1, h_ * dv), lambda b, c: (0, 0)),
                  pl.BlockSpec((None, h_, dk, dv), lambda b, c: (b, 0, 0, 0))],
        out_specs=[pl.BlockSpec((cl, h_ * dv), lambda b, c: (b * nc + c, 0)),
                   pl.BlockSpec((None, h_, dk, dv), lambda b, c: (b, 0, 0, 0))],
        out_shape=[jax.ShapeDtypeStruct((nb * t, h_ * dv), BF16),
                   jax.ShapeDtypeStruct((nb, h_, dk, dv), F32)],
        scratch_shapes=[pltpu.VMEM((h_, dk, dv), F32)],
        compiler_params=_cp("arbitrary", "arbitrary"),
        name="retention",
    )(p_bf, p_bf, p_bf, p_bf, cs2, sn2, decay, xi, zeta, gpow, gn_g.reshape(1, -1), s0)


CONV_HALO = 32
CONV_ROWS = 64
CONV_BLOCK = 64


def _conv_kernel(ca_ref, cb_ref, st_ref, w_ref, b_ref, lg_ref, lb_ref, y_ref, tail_ref, sh_ref, h_ref, *, tt):
    t = pl.program_id(1)
    ext = tt + CONV_HALO

    @pl.when(t == 0)
    def _():
        sh_ref[0, 0:CONV_HALO] = st_ref[...]

    @pl.when(t > 0)
    def _():
        sh_ref[0, 0:CONV_HALO] = sh_ref[0, tt:ext]

    ca = ca_ref[...].astype(F32)
    cb = cb_ref[...].astype(F32)
    sh_ref[0, CONV_HALO:ext] = ca * jax.nn.sigmoid(cb)
    tail_ref[...] = sh_ref[0, tt:ext]
    for s in range(1, SUBLANES):
        sh_ref[s, 0:ext - SUBLANES] = sh_ref[0, s:ext - SUBLANES + s]

    lead = CONV_HALO - (CONV_W - 1)
    blk = min(CONV_BLOCK, tt)
    nblk = blk // SUBLANES

    for lc in range(CONV_CH // LANES):
        cols = slice(lc * LANES, (lc + 1) * LANES)
        taps = [jnp.broadcast_to(w_ref[k:k + 1, cols], (SUBLANES, LANES)) for k in range(CONV_W)]
        bias = jnp.broadcast_to(b_ref[:, cols], (SUBLANES, LANES))

        def conv_body(i, carry):
            r0 = pl.multiple_of(i * blk, blk)
            accs = [bias] * nblk
            for k in range(CONV_W):
                a, s = divmod(k + lead, SUBLANES)
                u = sh_ref[s, pl.ds(r0 + a * SUBLANES, blk), cols]
                accs = [accs[j] + taps[k] * u[j * SUBLANES:(j + 1) * SUBLANES] for j in range(nblk)]
            h_ref[pl.ds(r0, blk), cols] = jnp.concatenate(accs, axis=0)
            return carry

        lax.fori_loop(0, tt // blk, conv_body, 0)

    rows = min(CONV_ROWS, tt)

    def norm_body(i, carry):
        r0 = pl.multiple_of(i * rows, rows)
        hn = _layer_norm(h_ref[pl.ds(r0, rows), :], lg_ref[...], lb_ref[...])
        y_ref[pl.ds(r0, rows), :] = (hn * jax.nn.sigmoid(hn)).astype(y_ref.dtype)
        return carry

    lax.fori_loop(0, tt // rows, norm_body, 0)


def _conv_module(p_bf, row0, nb, t, tt, state, dw, dw_b, ln_g, ln_b):
    nt = t // tt
    lead = CONV_HALO - (CONV_W - 1)
    st = jnp.pad(state, ((0, 0), (lead, 0), (0, 0)))
    rb = row0 // tt
    kern = functools.partial(_conv_kernel, tt=tt)
    y, tail = pl.pallas_call(
        kern,
        grid=(nb, nt),
        in_specs=[pl.BlockSpec((tt, CONV_CH), lambda b, i: (rb + b * nt + i, CIN_OFF // CONV_CH)),
                  pl.BlockSpec((tt, CONV_CH), lambda b, i: (rb + b * nt + i, CIN_OFF // CONV_CH + 1)),
                  pl.BlockSpec((None, CONV_HALO, CONV_CH), lambda b, i: (b, 0, 0)),
                  pl.BlockSpec((CONV_W, CONV_CH), lambda b, i: (0, 0)),
                  pl.BlockSpec((1, CONV_CH), lambda b, i: (0, 0)),
                  pl.BlockSpec((1, CONV_CH), lambda b, i: (0, 0)),
                  pl.BlockSpec((1, CONV_CH), lambda b, i: (0, 0))],
        out_specs=[pl.BlockSpec((tt, CONV_CH), lambda b, i: (b * nt + i, 0)),
                   pl.BlockSpec((None, CONV_HALO, CONV_CH), lambda b, i: (b, 0, 0))],
        out_shape=[jax.ShapeDtypeStruct((nb * t, CONV_CH), BF16),
                   jax.ShapeDtypeStruct((nb, CONV_HALO, CONV_CH), F32)],
        scratch_shapes=[pltpu.VMEM((SUBLANES, tt + CONV_HALO, CONV_CH), F32),
                        pltpu.VMEM((tt, CONV_CH), F32)],
        compiler_params=_cp("arbitrary", "arbitrary"),
        name="conv_module",
    )(p_bf, p_bf, st, dw, dw_b.reshape(1, -1), ln_g.reshape(1, -1), ln_b.reshape(1, -1))
    return y, tail[:, lead:, :]


def _merge_kernel(ap_ref, rp_ref, cp_ref, as_ref, rs_ref, cs_ref, ga_ref, gb_ref, gc_ref, ba_ref, bb_ref, bc_ref,
                  wa_ref, wr_ref, wc_ref, z_ref, *, prompt_tiles):
    ga = jax.nn.sigmoid(ga_ref[...].astype(F32) + ba_ref[...])
    gb = jax.nn.sigmoid(gb_ref[...].astype(F32) + bb_ref[...])
    gc = jax.nn.sigmoid(gc_ref[...].astype(F32) + bc_ref[...])

    is_prompt = pl.program_id(1) < prompt_tiles
    ya = jnp.dot(jnp.where(is_prompt, ap_ref[...], as_ref[...]), wa_ref[...], preferred_element_type=F32)
    yb = jnp.dot(jnp.where(is_prompt, rp_ref[...], rs_ref[...]), wr_ref[...], preferred_element_type=F32)
    yc = jnp.dot(jnp.where(is_prompt, cp_ref[...], cs_ref[...]), wc_ref[...], preferred_element_type=F32)
    z_ref[...] = (ga * ya + gb * yb + gc * yc).astype(z_ref.dtype)


def _merge(att_p, ret_p, cnv_p, att_s, ret_s, cnv_s, p_bf, b_gate, wa, wr, wc, tm, tn):
    np_, ns = att_p.shape[0], att_s.shape[0]
    assert np_ % tm == 0 and ns % tm == 0
    m = np_ + ns
    pt = np_ // tm
    d = wa.shape[1]
    nj = d // tn
    g0 = GATE_OFF // tn
    bg = b_gate.reshape(1, -1)

    def prompt_rows(j, i):
        return (jnp.minimum(i, pt - 1), 0)

    def sample_rows(j, i):
        return (jnp.maximum(i - pt, 0), 0)

    return pl.pallas_call(
        functools.partial(_merge_kernel, prompt_tiles=pt),
        grid=(nj, m // tm),
        in_specs=[pl.BlockSpec((tm, att_p.shape[1]), prompt_rows),
                  pl.BlockSpec((tm, ret_p.shape[1]), prompt_rows),
                  pl.BlockSpec((tm, cnv_p.shape[1]), prompt_rows),
                  pl.BlockSpec((tm, att_s.shape[1]), sample_rows),
                  pl.BlockSpec((tm, ret_s.shape[1]), sample_rows),
                  pl.BlockSpec((tm, cnv_s.shape[1]), sample_rows),
                  pl.BlockSpec((tm, tn), lambda j, i: (i, g0 + j)),
                  pl.BlockSpec((tm, tn), lambda j, i: (i, g0 + nj + j)),
                  pl.BlockSpec((tm, tn), lambda j, i: (i, g0 + 2 * nj + j)),
                  pl.BlockSpec((1, tn), lambda j, i: (0, j)),
                  pl.BlockSpec((1, tn), lambda j, i: (0, nj + j)),
                  pl.BlockSpec((1, tn), lambda j, i: (0, 2 * nj + j)),
                  pl.BlockSpec((wa.shape[0], tn), lambda j, i: (0, j)),
                  pl.BlockSpec((wr.shape[0], tn), lambda j, i: (0, j)),
                  pl.BlockSpec((wc.shape[0], tn), lambda j, i: (0, j))],
        out_specs=pl.BlockSpec((tm, tn), lambda j, i: (i, j)),
        out_shape=jax.ShapeDtypeStruct((m, d), BF16),
        compiler_params=_cp("arbitrary", "arbitrary"),
        name="branch_merge",
    )(att_p, ret_p, cnv_p, att_s, ret_s, cnv_s, p_bf, p_bf, p_bf, bg, bg, bg, wa, wr, wc)


def _route(x, w, b):
    tm = x.shape[0]
    logits = jnp.dot(x, w, precision=lax.Precision.HIGHEST, preferred_element_type=F32) + b
    lane = lax.broadcasted_iota(I32, (tm, LANES), 1)
    valid = lane < N_EXPERTS
    lm = jnp.where(valid, logits, NEG_BIG)
    e = jnp.where(valid, jnp.exp(lm - jnp.max(lm, axis=-1, keepdims=True)), 0.0)
    aff = e / jnp.sum(e, axis=-1, keepdims=True)

    def top2(vals):
        v1 = jnp.max(vals, axis=-1, keepdims=True)
        i1 = jnp.min(jnp.where(vals == v1, lane, LANES), axis=-1, keepdims=True)
        rest = jnp.where(lane == i1, -2.0, vals)
        v2 = jnp.max(rest, axis=-1, keepdims=True)
        i2 = jnp.min(jnp.where(rest == v2, lane, LANES), axis=-1, keepdims=True)
        return v1, i1, v2, i2

    grp = lane // EXP_PER_GROUP
    best = jnp.zeros((tm, 1), I32)
    best_score = None
    for g in range(N_GROUPS):
        v1, _, v2, _ = top2(jnp.where(grp == g, aff, -1.0))
        score = v1 + v2
        if g == 0:
            best_score = score
        else:
            better = score > best_score
            best = jnp.where(better, g, best)
            best_score = jnp.where(better, score, best_score)
    v1, i1, v2, i2 = top2(jnp.where(grp == best, aff, -1.0))
    tot = v1 + v2
    return (jnp.where(lane == 0, i1.astype(F32), 0.0) + jnp.where(lane == 1, i2.astype(F32), 0.0)
            + jnp.where(lane == 2, v1 / tot, 0.0) + jnp.where(lane == 3, v2 / tot, 0.0))


def _outproj_ln_kernel(z_ref, w_ref, x_ref, g_ref, b_ref, wr_ref, br_ref, of_ref, ob_ref, rt_ref, *, alpha):
    mix = jnp.dot(z_ref[...], w_ref[...], preferred_element_type=F32)
    y = _layer_norm(alpha * x_ref[...] + mix, g_ref[...], b_ref[...])
    of_ref[...] = y
    ob_ref[...] = y.astype(BF16)
    rt_ref[...] = _route(y, wr_ref[...], br_ref[...])


def _outproj_ln(z, w_out, x, g, b, w_router, b_router, alpha, tm):
    m, d = x.shape
    wr = jnp.pad(w_router, ((0, 0), (0, LANES - N_EXPERTS)))
    br = jnp.pad(b_router, (0, LANES - N_EXPERTS)).reshape(1, -1)
    kern = functools.partial(_outproj_ln_kernel, alpha=alpha)
    return pl.pallas_call(
        kern,
        grid=(m // tm,),
        in_specs=[pl.BlockSpec((tm, d), lambda i: (i, 0)),
                  pl.BlockSpec((d, d), lambda i: (0, 0)),
                  pl.BlockSpec((tm, d), lambda i: (i, 0)),
                  pl.BlockSpec((1, d), lambda i: (0, 0)),
                  pl.BlockSpec((1, d), lambda i: (0, 0)),
                  pl.BlockSpec((d, LANES), lambda i: (0, 0)),
                  pl.BlockSpec((1, LANES), lambda i: (0, 0))],
        out_specs=[pl.BlockSpec((tm, d), lambda i: (i, 0)), pl.BlockSpec((tm, d), lambda i: (i, 0)),
                   pl.BlockSpec((tm, LANES), lambda i: (i, 0))],
        out_shape=[jax.ShapeDtypeStruct((m, d), F32), jax.ShapeDtypeStruct((m, d), BF16),
                   jax.ShapeDtypeStruct((m, LANES), F32)],
        compiler_params=_cp("arbitrary"),
        name="out_proj_ln1_route",
    )(z, w_out, x, g.reshape(1, -1), b.reshape(1, -1), wr, br)


MOE_TILE = 256
MOE_ISSUE_GROUPS = 8


def _gather_rows(idx_ref, src_hbm, dst_ref, sem, lo, hi):
    for r in range(lo, hi):
        pltpu.make_async_copy(src_hbm.at[pl.ds(idx_ref[0, 0, r], 1)], dst_ref.at[pl.ds(r, 1)], sem).start()


def _wait_rows(src_hbm, dst_ref, sem, n):
    pltpu.make_async_copy(src_hbm.at[pl.ds(0, n)], dst_ref, sem).wait()


def _moe_expert_kernel(te_ref, cur_ref, nxt_ref, x_hbm, w1_ref, w3_ref, w2_ref, o_ref,
                       xg_ref, sem, w1b_ref, w3b_ref, w2b_ref):
    t = pl.program_id(0)
    nt = pl.num_programs(0)
    tile = xg_ref.shape[1]
    slot = t % 2

    @pl.when(t == 0)
    def _():
        _gather_rows(cur_ref, x_hbm, xg_ref.at[0], sem.at[0], 0, tile)

    @pl.when((t == 0) | (te_ref[t] != te_ref[jnp.maximum(t - 1, 0)]))
    def _():
        w1b_ref[...] = w1_ref[...].astype(BF16)
        w3b_ref[...] = w3_ref[...].astype(BF16)
        w2b_ref[...] = w2_ref[...].astype(BF16)

    _wait_rows(x_hbm, xg_ref.at[slot], sem.at[slot], tile)
    d = xg_ref.shape[2]
    kc, rc = d // MOE_ISSUE_GROUPS, tile // MOE_ISSUE_GROUPS
    h1 = h3 = None
    for c in range(MOE_ISSUE_GROUPS):
        xk = xg_ref[slot, :, c * kc:(c + 1) * kc].astype(BF16)
        p1 = jnp.dot(xk, w1b_ref[c * kc:(c + 1) * kc, :], preferred_element_type=F32)
        p3 = jnp.dot(xk, w3b_ref[c * kc:(c + 1) * kc, :], preferred_element_type=F32)
        h1 = p1 if h1 is None else h1 + p1
        h3 = p3 if h3 is None else h3 + p3
        _gather_rows(nxt_ref, x_hbm, xg_ref.at[1 - slot], sem.at[1 - slot], c * rc, (c + 1) * rc)
    h = h1 * jax.nn.sigmoid(h1) * h3
    o_ref[...] = jnp.dot(h.astype(BF16), w2b_ref[...], preferred_element_type=F32)

    @pl.when(t == nt - 1)
    def _():
        _wait_rows(x_hbm, xg_ref.at[1 - slot], sem.at[1 - slot], tile)


def _moe_experts(x, w1, w3, w2, tile_expert, src_tiles):
    n, d = x.shape
    _, _, de = w1.shape
    nt = src_tiles.shape[0] - 1
    tile = src_tiles.shape[2]
    grid_spec = pltpu.PrefetchScalarGridSpec(
        num_scalar_prefetch=1,
        grid=(nt,),
        in_specs=[pl.BlockSpec((1, 1, tile), lambda t, te: (t, 0, 0), memory_space=pltpu.SMEM),
                  pl.BlockSpec((1, 1, tile), lambda t, te: (t + 1, 0, 0), memory_space=pltpu.SMEM),
                  pl.BlockSpec(memory_space=pl.ANY),
                  pl.BlockSpec((None, d, de), lambda t, te: (te[t], 0, 0)),
                  pl.BlockSpec((None, d, de), lambda t, te: (te[t], 0, 0)),
                  pl.BlockSpec((None, de, d), lambda t, te: (te[t], 0, 0))],
        out_specs=pl.BlockSpec((tile, d), lambda t, te: (t, 0)),
        scratch_shapes=[pltpu.VMEM((2, tile, d), F32),
                        pltpu.SemaphoreType.DMA((2,)),
                        pltpu.VMEM((d, de), BF16),
                        pltpu.VMEM((d, de), BF16),
                        pltpu.VMEM((de, d), BF16)],
    )
    return pl.pallas_call(
        _moe_expert_kernel,
        grid_spec=grid_spec,
        out_shape=jax.ShapeDtypeStruct((nt * tile, d), F32),
        compiler_params=_cp("arbitrary"),
        name="moe_experts",
    )(tile_expert, src_tiles, src_tiles, x, w1, w3, w2)


def _moe_combine_kernel(cur_ref, nxt_ref, y_hbm, rt_ref, x_ref, g_ref, b_ref, of_ref, ob_ref, yg_ref, sem, *, alpha):
    t = pl.program_id(0)
    nt = pl.num_programs(0)
    tm = x_ref.shape[0]
    slot = t % 2

    @pl.when(t == 0)
    def _():
        _gather_rows(cur_ref, y_hbm, yg_ref.at[0], sem.at[0], 0, 2 * tm)

    _gather_rows(nxt_ref, y_hbm, yg_ref.at[1 - slot], sem.at[1 - slot], 0, 2 * tm)
    _wait_rows(y_hbm, yg_ref.at[slot], sem.at[slot], 2 * tm)
    rt = rt_ref[...]
    moe = rt[:, 2:3] * yg_ref[slot, 0:tm] + rt[:, 3:4] * yg_ref[slot, tm:2 * tm]
    y = _layer_norm(alpha * x_ref[...] + moe, g_ref[...], b_ref[...])
    of_ref[...] = y
    ob_ref[...] = y.astype(BF16)

    @pl.when(t == nt - 1)
    def _():
        _wait_rows(y_hbm, yg_ref.at[1 - slot], sem.at[1 - slot], 2 * tm)


def _moe_combine_ln(y_sorted, pos_tiles, route, x, g, b, alpha, tm):
    m, d = x.shape
    kern = functools.partial(_moe_combine_kernel, alpha=alpha)
    return pl.pallas_call(
        kern,
        grid=(m // tm,),
        in_specs=[pl.BlockSpec((1, 1, 2 * tm), lambda t: (t, 0, 0), memory_space=pltpu.SMEM),
                  pl.BlockSpec((1, 1, 2 * tm), lambda t: (t + 1, 0, 0), memory_space=pltpu.SMEM),
                  pl.BlockSpec(memory_space=pl.ANY),
                  pl.BlockSpec((tm, LANES), lambda t: (t, 0)),
                  pl.BlockSpec((tm, d), lambda t: (t, 0)),
                  pl.BlockSpec((1, d), lambda t: (0, 0)),
                  pl.BlockSpec((1, d), lambda t: (0, 0))],
        out_specs=[pl.BlockSpec((tm, d), lambda t: (t, 0)), pl.BlockSpec((tm, d), lambda t: (t, 0))],
        out_shape=[jax.ShapeDtypeStruct((m, d), F32), jax.ShapeDtypeStruct((m, d), BF16)],
        scratch_shapes=[pltpu.VMEM((2, 2 * tm, d), F32), pltpu.SemaphoreType.DMA((2,))],
        compiler_params=_cp("arbitrary"),
        name="moe_combine_ln2",
    )(pos_tiles, pos_tiles, y_sorted, route, x, g.reshape(1, -1), b.reshape(1, -1))


def _moe_layout(route, tile, tm):
    n = route.shape[0]
    e_flat = route[:, 0:2].astype(I32).reshape(-1)
    npair = 2 * n
    nt = -(-npair // tile) + N_EXPERTS
    onehot = (e_flat[:, None] == jnp.arange(N_EXPERTS, dtype=I32)[None, :]).astype(I32)
    csum = jnp.cumsum(onehot, axis=0)
    counts = csum[-1]
    rank = jnp.sum((csum - onehot) * onehot, axis=1)
    padded = (counts + tile - 1) // tile * tile
    ends = jnp.cumsum(padded)
    starts = ends - padded
    pos = starts[e_flat] + rank
    rows = nt * tile
    src = jnp.zeros((rows,), I32).at[pos].set(jnp.arange(npair, dtype=I32) // 2, unique_indices=True)
    tile_expert = jnp.minimum(jnp.searchsorted(ends, jnp.arange(nt, dtype=I32) * tile, side='right'),
                              N_EXPERTS - 1).astype(I32)
    src_tiles = jnp.concatenate([src, jnp.zeros((tile,), I32)]).reshape(nt + 1, 1, tile)
    pos_t = pos.reshape(n // tm, tm, 2).transpose(0, 2, 1).reshape(n // tm, 1, 2 * tm)
    pos_tiles = jnp.concatenate([pos_t, jnp.zeros((1, 1, 2 * tm), I32)], axis=0)
    return tile_expert, src_tiles, pos_tiles


def kernel(x_prompt, x_sample, cache_k, cache_v, cache_idx_k, state_ret, state_conv, w_in, b_gate, w_att_o, ret_gn_g,
           w_ret_o, conv_dw, conv_dw_b, conv_ln_g, conv_ln_b, w_conv_o, w_out, ln1_g, ln1_b, w_router, b_router,
           moe_w1, moe_w3, moe_w2, ln2_g, ln2_b):
    nbp, tp, d = x_prompt.shape
    nbs, ts, _ = x_sample.shape
    depth = w_in.shape[0]
    past = cache_k.shape[2]
    np_, ns = nbp * tp, nbs * ts
    n = np_ + ns
    alpha = (2 * depth) ** 0.25
    kvw = KV_HEADS * ATT_HD

    tm = _pow2_tile(n, 512)
    tq_p = _pow2_tile(tp, 256)
    cl_p = _pow2_tile(tp, 256)
    tt_p = _pow2_tile(tp, 256)
    ls_true = past + ts
    ls_pad = -(-ls_true // (2 * LANES)) * (2 * LANES)
    assert tp % (2 * LANES) == 0 and np_ % max(tq_p, cl_p, tt_p, ts) == 0

    x = jnp.concatenate([x_prompt.reshape(np_, d), x_sample.reshape(ns, d)], axis=0)
    xb = x.astype(BF16)

    c1 = ATT_HEADS * ATT_HD
    c2 = c1 + F32_USED
    ks, vs, iks, rps, cps, kss, vss, ikss, rss, css = ([] for _ in range(10))
    zero_ret = jnp.zeros((nbp, RET_HEADS, RET_DK, RET_DV), F32)
    zero_conv = jnp.zeros((nbp, CONV_W - 1, CONV_CH), F32)

    for l in range(depth):
        w_l = w_in[l]
        w_bf = jnp.concatenate([w_l[:, c2:], w_l[:, :c1]], axis=1).astype(BF16)
        w_f = jnp.pad(w_l[:, c1:c2], ((0, 0), (0, F32_WIDTH - F32_USED))).astype(BF16)
        p_bf = _matmul(xb, w_bf, BF16, tm, 1536)
        p_f = _matmul(xb, w_f, F32, tm, F32_WIDTH)

        k_new = p_f[:, KA_OFF:KA_OFF + kvw]
        v_new = p_f[:, VA_OFF:VA_OFF + kvw]
        ik_new = p_f[:, KI_OFF:KI_OFF + IDX_DIM]
        wi_t = p_f[:, WI_OFF:WI_OFF + IDX_HEADS].T
        ks.append(k_new[:np_].reshape(nbp, tp, KV_HEADS, ATT_HD))
        vs.append(v_new[:np_].reshape(nbp, tp, KV_HEADS, ATT_HD))
        iks.append(ik_new[:np_].reshape(nbp, tp, IDX_DIM))
        kss.append(k_new[np_:].reshape(nbs, ts, KV_HEADS, ATT_HD))
        vss.append(v_new[np_:].reshape(nbs, ts, KV_HEADS, ATT_HD))
        ikss.append(ik_new[np_:].reshape(nbs, ts, IDX_DIM))

        nq = tp // tq_p
        hq = ATT_HEADS * ATT_HD
        qa_blk = (GATE_OFF + 3 * d) // hq
        att_p = _attention(
            p_bf, lambda tq: ((tq, hq), lambda b, j: (b * nq + j, qa_blk)),
            p_f, lambda lp: ((lp, kvw), lambda b, j: (b, KA_OFF // kvw)),
            p_f, lambda lp: ((lp, kvw), lambda b, j: (b, VA_OFF // kvw)),
            p_f, lambda tq: ((tq, IDX_HEADS * IDX_DIM), lambda b, j: (b * nq + j, QI_OFF // (IDX_HEADS * IDX_DIM))),
            p_f, lambda lp: ((lp, LANES), lambda b, j: (b, KI_OFF // LANES)),
            wi_t[:, :np_], lambda tq: ((IDX_HEADS, tq), lambda b, j: (0, b * nq + j)),
            nb=nbp, t=tp, l_pad=tp, l_true=tp, offset=0, tq=tq_p)

        mask_s = _sample_select(cache_idx_k[l], p_f, wi_t, np_, nbs, ts, past)
        att_s = _sample_attend(p_bf, qa_blk, cache_k[l].reshape(nbs, past, kvw), cache_v[l].reshape(nbs, past, kvw),
                               p_f, mask_s, np_, nbs, ts, past)

        ret_p, rs_p = _retention(p_bf, 0, nbp, tp, cl_p, 0, zero_ret, ret_gn_g[l])
        ret_s, rs_s = _retention(p_bf, np_, nbs, ts, ts, past, state_ret[l], ret_gn_g[l])
        rps.append(rs_p)
        rss.append(rs_s)

        cnv_p, cs_p = _conv_module(p_bf, 0, nbp, tp, tt_p, zero_conv, conv_dw[l], conv_dw_b[l], conv_ln_g[l],
                                   conv_ln_b[l])
        cnv_s, cs_s = _conv_module(p_bf, np_, nbs, ts, ts, state_conv[l], conv_dw[l], conv_dw_b[l], conv_ln_g[l],
                                   conv_ln_b[l])
        cps.append(cs_p)
        css.append(cs_s)

        z = _merge(att_p, ret_p, cnv_p, att_s, ret_s, cnv_s, p_bf, b_gate[l], w_att_o[l].astype(BF16),
                   w_ret_o[l].astype(BF16), w_conv_o[l].astype(BF16), math.gcd(tm, ns), 512)
        tm_c = _pow2_tile(n, 256)
        x, xb, route = _outproj_ln(z, w_out[l].astype(BF16), x, ln1_g[l], ln1_b[l], w_router, b_router, alpha, tm_c)
        tile_expert, src_tiles, pos_tiles = _moe_layout(route, MOE_TILE, tm_c)
        y_sorted = _moe_experts(x, moe_w1[l], moe_w3[l], moe_w2[l], tile_expert, src_tiles)
        x, xb = _moe_combine_ln(y_sorted, pos_tiles, route, x, ln2_g[l], ln2_b[l], alpha, tm_c)

    y_prompt = x[:np_].reshape(nbp, tp, d)
    y_sample = x[np_:].reshape(nbs, ts, d)
    st = jnp.stack
    return (y_prompt, y_sample, st(ks), st(vs), st(iks), st(rps), st(cps), st(kss), st(vss), st(ikss), st(rss), st(css))
```

```python
import functools
import math

import jax
import jax.numpy as jnp
from jax import lax
from jax.experimental import pallas as pl
from jax.experimental.pallas import tpu as pltpu

F32 = jnp.float32
BF16 = jnp.bfloat16
I32 = jnp.int32

CHUNK = 64
TOPK_MAX = 256
ATT_HEADS, ATT_HD, KV_HEADS = 8, 128, 2
KV_GROUP = ATT_HEADS // KV_HEADS
IDX_HEADS, IDX_DIM = 8, 64
RET_HEADS, RET_DK, RET_DV = 8, 128, 256
ROPE_BASE = 10000.0
CONV_CH, CONV_W = 1024, 31
N_EXPERTS, N_GROUPS = 16, 4
EXP_PER_GROUP = N_EXPERTS // N_GROUPS
LN_EPS = 1e-5

LANES = 128
SUBLANES = 8

INT_MIN = -2 ** 31
NEG_BIG = -1e30

QR_OFF = 0
KR_OFF = QR_OFF + RET_HEADS * RET_DK
VR_OFF = KR_OFF + RET_HEADS * RET_DK
GR_OFF = VR_OFF + RET_HEADS * RET_DV
CIN_OFF = GR_OFF + RET_HEADS * RET_DV
GATE_OFF = CIN_OFF + 2 * CONV_CH
KA_OFF = 0
VA_OFF = KA_OFF + KV_HEADS * ATT_HD
QI_OFF = VA_OFF + KV_HEADS * ATT_HD
KI_OFF = QI_OFF + IDX_HEADS * IDX_DIM
WI_OFF = KI_OFF + IDX_DIM
F32_USED = WI_OFF + IDX_HEADS
F32_WIDTH = -(-F32_USED // LANES) * LANES

VMEM_LIMIT = 56 * 1024 * 1024


def _cp(*sem):
    return pltpu.CompilerParams(dimension_semantics=sem, vmem_limit_bytes=VMEM_LIMIT)


def _pow2_tile(n, pref):
    t = pref
    while n % t:
        t //= 2
    return t


def _layer_norm(x, g, b):
    mu = jnp.mean(x, axis=-1, keepdims=True)
    xc = x - mu
    var = jnp.mean(xc * xc, axis=-1, keepdims=True)
    return xc * lax.rsqrt(var + LN_EPS) * g + b


def _mm_kernel(x_ref, w_ref, o_ref):
    o_ref[...] = jnp.dot(x_ref[...], w_ref[...], preferred_element_type=F32).astype(o_ref.dtype)


def _matmul(x, w, out_dtype, tm, tn):
    m, k = x.shape
    n = w.shape[1]
    return pl.pallas_call(
        _mm_kernel,
        grid=(n // tn, m // tm),
        in_specs=[pl.BlockSpec((tm, k), lambda j, i: (i, 0)),
                  pl.BlockSpec((k, tn), lambda j, i: (0, j))],
        out_specs=pl.BlockSpec((tm, tn), lambda j, i: (i, j)),
        out_shape=jax.ShapeDtypeStruct((m, n), out_dtype),
        compiler_params=_cp("arbitrary", "arbitrary"),
        name="in_proj",
    )(x, w)


def _attn_kernel(q_ref, k_ref, v_ref, qi_ref, ki_ref, wi_ref, o_ref,
                 key_ref, lg_ref, mx_ref, acc_ref, m_ref,
                 *, tq, nq, ck, nck, offset, l_true, k_sel, tie_bits):
    j = pl.program_id(1)
    nsub = ck // LANES
    qlane = lax.broadcasted_iota(I32, (1, tq), 1)
    krow = lax.broadcasted_iota(I32, (ck, tq), 0)
    pos = offset + j * tq + qlane
    limit = jnp.minimum((pos // CHUNK + 1) * CHUNK, l_true)
    if nq == 1:
        n_act = (min(((offset + tq - 1) // CHUNK + 1) * CHUNK, l_true) + ck - 1) // ck
    else:
        last_pos = offset + (j + 1) * tq - 1
        max_limit = jnp.minimum((last_pos // CHUNK + 1) * CHUNK, l_true)
        n_act = (max_limit + ck - 1) // ck

    qi = (qi_ref[...] * (IDX_DIM ** -0.5)).astype(BF16)
    qis = [qi[:, h * IDX_DIM:(h + 1) * IDX_DIM] for h in range(IDX_HEADS)]
    stacked = tq % LANES == 0
    if stacked:
        qstack = jnp.concatenate(qis, axis=0)
    wt = wi_ref[...] * (IDX_HEADS ** -0.5)
    wrows = [wt[h:h + 1, :] for h in range(IDX_HEADS)]
    nt = (((1,), (1,)), ((), ()))

    def score_body(c, carry):
        for u in range(nsub):
            r0 = pl.multiple_of(c * ck + u * LANES, LANES)
            kiu = ki_ref[pl.ds(r0, LANES), :][:, :IDX_DIM].astype(BF16)
            if stacked:
                d_all = lax.dot_general(kiu, qstack, nt, preferred_element_type=F32)
                ds = [d_all[:, h * tq:(h + 1) * tq] for h in range(IDX_HEADS)]
            else:
                ds = [lax.dot_general(kiu, qis[h], nt, preferred_element_type=F32) for h in range(IDX_HEADS)]
            s = jnp.zeros((LANES, tq), F32)
            for h in range(IDX_HEADS):
                s = s + jnp.maximum(ds[h], 0.0) * wrows[h]
            bits = lax.bitcast_convert_type(s, I32)
            bits = jnp.where(bits == INT_MIN, 0, bits)
            key = jnp.where(bits >= 0, bits, bits ^ 0x7FFFFFFF)
            kpos = r0 + lax.broadcasted_iota(I32, (LANES, tq), 0)
            key_ref[c, u * LANES:(u + 1) * LANES, :] = jnp.where(kpos < limit, key, INT_MIN)
        return carry

    lax.fori_loop(0, n_act, score_body, 0)

    def count(pred):
        def body(c, acc):
            p = jnp.where(pred(key_ref[c], c * ck + krow), 1.0, 0.0)
            parts = [p[r * SUBLANES:(r + 1) * SUBLANES] for r in range(ck // SUBLANES)]
            while len(parts) > 1:
                parts = [a + b for a, b in zip(parts[0::2], parts[1::2])]
            return acc + parts[0]
        acc = lax.fori_loop(0, n_act, body, jnp.zeros((SUBLANES, tq), F32))
        return jnp.sum(acc, axis=0, keepdims=True)

    def bisect(i, tu):
        cand_u = tu | (jnp.int32(1) << (31 - i))
        cand_s = cand_u ^ INT_MIN
        cnt = count(lambda key, idx: key >= cand_s)
        return jnp.where(cnt >= k_sel, cand_u, tu)

    tu = lax.fori_loop(0, 32, bisect, jnp.zeros((1, tq), I32))
    ts = tu ^ INT_MIN
    c_gt = count(lambda key, idx: key > ts)
    c_ge = count(lambda key, idx: key >= ts)
    want = k_sel - c_gt

    eye = jnp.where(lax.broadcasted_iota(I32, (tq, tq), 0) == lax.broadcasted_iota(I32, (tq, tq), 1), 1.0, 0.0).astype(BF16)
    ts_adm = jnp.maximum(ts, INT_MIN + 1)
    m_ref[...] = jnp.full((1, tq), 2 ** 30, I32)

    @pl.when(jnp.max(c_ge) > k_sel)
    def _():
        def tie_bisect(i, m):
            cand = m | (jnp.int32(1) << (tie_bits - 1 - i))
            cnt = count(lambda key, idx: (key == ts) & (idx < cand))
            return jnp.where(cnt < want, cand, m)
        m_ref[...] = lax.fori_loop(0, tie_bits, tie_bisect, jnp.zeros((1, tq), I32))

    m_last = m_ref[...]

    qscale = (ATT_HD ** -0.5) * 1.4426950408889634
    groups = range(KV_HEADS)
    qss = []
    for g in groups:
        qs = jnp.concatenate(
            [q_ref[:, (g * KV_GROUP + r) * ATT_HD:(g * KV_GROUP + r + 1) * ATT_HD] for r in range(KV_GROUP)], axis=0)
        qss.append((qs.astype(F32) * qscale).astype(BF16))
    mx_ref[...] = jnp.full(mx_ref.shape, NEG_BIG, F32)

    def logit_body(c, carry):
        r0 = pl.multiple_of(c * ck, ck)
        key = key_ref[c]
        sel = (key > ts_adm) | ((key == ts_adm) & (c * ck + krow <= m_last))
        sel_t = jnp.where(sel, 1.0, 0.0).astype(BF16)
        sel_q = lax.dot_general(eye, sel_t, nt, preferred_element_type=F32)
        b = (sel_q - 1.0) * (-NEG_BIG)
        b4 = jnp.concatenate([b] * KV_GROUP, axis=0)
        for g in groups:
            kc = k_ref[pl.ds(r0, ck), g * ATT_HD:(g + 1) * ATT_HD].astype(BF16)
            lg = lax.dot_general(qss[g], kc, nt, preferred_element_type=F32) + b4
            lg_ref[g, c] = lg
            m = mx_ref[g]
            for u in range(nsub):
                m = jnp.maximum(m, lg[:, u * LANES:(u + 1) * LANES])
            mx_ref[g] = m
        return carry

    lax.fori_loop(0, n_act, logit_body, 0)
    m_rows = [jnp.max(mx_ref[g], axis=-1, keepdims=True) for g in groups]
    acc_ref[...] = jnp.zeros(acc_ref.shape, F32)
    ones_cols = jnp.ones((ck, ATT_HD), BF16)

    def pv_body(c, carry):
        r0 = pl.multiple_of(c * ck, ck)
        for g in groups:
            p = jnp.exp2(lg_ref[g, c] - m_rows[g]).astype(BF16)
            vc = v_ref[pl.ds(r0, ck), g * ATT_HD:(g + 1) * ATT_HD].astype(BF16)
            acc_ref[g] += jnp.dot(p, jnp.concatenate([vc, ones_cols], axis=1), preferred_element_type=F32)
        return carry

    lax.fori_loop(0, n_act, pv_body, 0)
    for g in groups:
        a = acc_ref[g]
        o = a[:, :ATT_HD] / a[:, ATT_HD:]
        for r in range(KV_GROUP):
            h = g * KV_GROUP + r
            o_ref[:, h * ATT_HD:(h + 1) * ATT_HD] = o[r * tq:(r + 1) * tq].astype(o_ref.dtype)


def _attention(q_arr, q_map, k_arr, k_map, v_arr, v_map, qi_arr, qi_map, ki_arr, ki_map, wi_arr, wi_map,
               *, nb, t, l_pad, l_true, offset, tq):
    ck = 2 * LANES
    nck = l_pad // ck
    assert l_pad % ck == 0 and t % tq == 0 and nck * (ck // LANES) < 256
    nq = t // tq
    k_sel = min(TOPK_MAX, l_true // 4)
    tie_bits = max(1, (l_pad - 1).bit_length())
    kern = functools.partial(_attn_kernel, tq=tq, nq=nq, ck=ck, nck=nck, offset=offset, l_true=l_true,
                             k_sel=k_sel, tie_bits=tie_bits)
    hq = ATT_HEADS * ATT_HD
    return pl.pallas_call(
        kern,
        grid=(nb, nq),
        in_specs=[pl.BlockSpec(*q_map(tq)), pl.BlockSpec(*k_map(l_pad)), pl.BlockSpec(*v_map(l_pad)),
                  pl.BlockSpec(*qi_map(tq)), pl.BlockSpec(*ki_map(l_pad)), pl.BlockSpec(*wi_map(tq))],
        out_specs=pl.BlockSpec((tq, hq), lambda b, j: (b * nq + j, 0)),
        out_shape=jax.ShapeDtypeStruct((nb * t, hq), BF16),
        scratch_shapes=[pltpu.VMEM((nck, ck, tq), I32),
                        pltpu.VMEM((KV_HEADS, nck, KV_GROUP * tq, ck), F32),
                        pltpu.VMEM((KV_HEADS, KV_GROUP * tq, LANES), F32),
                        pltpu.VMEM((KV_HEADS, KV_GROUP * tq, 2 * ATT_HD), F32),
                        pltpu.VMEM((1, tq), I32)],
        compiler_params=_cp("arbitrary", "arbitrary"),
        name="dsa_attention",
    )(q_arr, k_arr, v_arr, qi_arr, ki_arr, wi_arr)


def _order_key(s):
    bits = lax.bitcast_convert_type(s, I32)
    bits = jnp.where(bits == INT_MIN, 0, bits)
    return jnp.where(bits >= 0, bits, bits ^ 0x7FFFFFFF)


def _sample_select_kernel(kic_ref, new_ref, qi_ref, wi_ref, mask_ref, key_ref, keyn_ref, m_ref,
                          *, ts, gs, past, ck, k_sel, tie_bits):
    nck = past // ck
    nsub = ck // LANES
    l_true = past + ts
    lane = lax.broadcasted_iota(I32, (1, LANES), 1)
    lane_s = lane // ts
    pos = past + lane % ts
    limit = jnp.minimum((pos // CHUNK + 1) * CHUNK, l_true)
    nt = (((1,), (1,)), ((), ()))

    qi = (qi_ref[...] * (IDX_DIM ** -0.5)).astype(BF16)
    qstack = jnp.concatenate([qi[:, h * IDX_DIM:(h + 1) * IDX_DIM] for h in range(IDX_HEADS)], axis=0)
    wt = wi_ref[...] * (IDX_HEADS ** -0.5)
    wrows = [wt[h:h + 1, :] for h in range(IDX_HEADS)]

    def own_stream(d, rows):
        out = d[(gs - 1) * rows:gs * rows]
        for s in range(gs - 2, -1, -1):
            out = jnp.where(lane_s == s, d[s * rows:(s + 1) * rows], out)
        return out

    def scores(kmat, rows):
        d_all = lax.dot_general(kmat, qstack, nt, preferred_element_type=F32)
        s = jnp.zeros((rows, LANES), F32)
        for h in range(IDX_HEADS):
            s = s + jnp.maximum(own_stream(d_all[:, h * LANES:(h + 1) * LANES], rows), 0.0) * wrows[h]
        return _order_key(s)

    def score_body(c, carry):
        for u in range(nsub):
            r0 = pl.multiple_of(c * ck + u * LANES, LANES)
            kmat = jnp.concatenate([kic_ref[s, pl.ds(r0, LANES), :] for s in range(gs)], axis=0).astype(BF16)
            kpos = r0 + lax.broadcasted_iota(I32, (LANES, LANES), 0)
            key_ref[c, u * LANES:(u + 1) * LANES, :] = jnp.where(kpos < limit, scores(kmat, LANES), INT_MIN)
        return carry

    lax.fori_loop(0, nck, score_body, 0)
    kn = new_ref[:, :IDX_DIM].astype(BF16)
    npos = past + lax.broadcasted_iota(I32, (ts, LANES), 0)
    keyn_ref[...] = jnp.where(npos < limit, scores(kn, ts), INT_MIN)

    krow = lax.broadcasted_iota(I32, (ck, LANES), 0)

    def tree(p, rows):
        parts = [p[r * SUBLANES:(r + 1) * SUBLANES] for r in range(rows // SUBLANES)]
        while len(parts) > 1:
            parts = [a + b for a, b in zip(parts[0::2], parts[1::2])]
        return parts[0]

    def count(pred):
        def body(c, acc):
            p = jnp.where(pred(key_ref[c], c * ck + krow), 1.0, 0.0)
            return acc + tree(p, ck)
        acc = lax.fori_loop(0, nck, body, jnp.zeros((SUBLANES, LANES), F32))
        acc = acc + tree(jnp.where(pred(keyn_ref[...], npos), 1.0, 0.0), ts)
        return jnp.sum(acc, axis=0, keepdims=True)

    def bisect(i, tu):
        cand_u = tu | (jnp.int32(1) << (31 - i))
        cand_s = cand_u ^ INT_MIN
        cnt = count(lambda key, idx: key >= cand_s)
        return jnp.where(cnt >= k_sel, cand_u, tu)

    tu = lax.fori_loop(0, 32, bisect, jnp.zeros((1, LANES), I32))
    ts_ = tu ^ INT_MIN
    c_gt = count(lambda key, idx: key > ts_)
    c_ge = count(lambda key, idx: key >= ts_)
    want = k_sel - c_gt
    m_ref[...] = jnp.full((1, LANES), 2 ** 30, I32)

    @pl.when(jnp.max(c_ge) > k_sel)
    def _():
        def tie_bisect(i, m):
            cand = m | (jnp.int32(1) << (tie_bits - 1 - i))
            cnt = count(lambda key, idx: (key == ts_) & (idx < cand))
            return jnp.where(cnt < want, cand, m)
        m_ref[...] = lax.fori_loop(0, tie_bits, tie_bisect, jnp.zeros((1, LANES), I32))

    m_last = m_ref[...]
    ts_adm = jnp.maximum(ts_, INT_MIN + 1)
    eye = jnp.where(lax.broadcasted_iota(I32, (LANES, LANES), 0) == lax.broadcasted_iota(I32, (LANES, LANES), 1),
                    1.0, 0.0).astype(BF16)

    def selected(key, idx):
        sel = (key > ts_adm) | ((key == ts_adm) & (idx <= m_last))
        return jnp.where(sel, 1.0, 0.0).astype(BF16)

    for c in range(nck):
        sel_q = lax.dot_general(eye, selected(key_ref[c], c * ck + krow), nt, preferred_element_type=F32)
        mask_ref[:, c * ck:(c + 1) * ck] = sel_q.astype(BF16)
    sel_n = lax.dot_general(eye, selected(keyn_ref[...], npos), nt, preferred_element_type=F32)
    mask_ref[:, past:past + LANES] = jnp.concatenate(
        [sel_n, jnp.zeros((LANES, LANES - ts), F32)], axis=1).astype(BF16)


def _sample_select(cache_ik, p_f, wi_t, np_, nbs, ts, past):
    gs = LANES // ts
    ck = 2 * LANES
    assert LANES % ts == 0 and nbs % gs == 0 and past % ck == 0 and np_ % LANES == 0 and ts % SUBLANES == 0
    l_true = past + ts
    k_sel = min(TOPK_MAX, l_true // 4)
    kern = functools.partial(_sample_select_kernel, ts=ts, gs=gs, past=past, ck=ck, k_sel=k_sel,
                             tie_bits=max(1, (l_true - 1).bit_length()))
    rb = np_ // LANES
    return pl.pallas_call(
        kern,
        grid=(nbs // gs,),
        in_specs=[pl.BlockSpec((gs, past, IDX_DIM), lambda g: (g, 0, 0)),
                  pl.BlockSpec((LANES, LANES), lambda g: (rb + g, KI_OFF // LANES)),
                  pl.BlockSpec((LANES, IDX_HEADS * IDX_DIM), lambda g: (rb + g, QI_OFF // (IDX_HEADS * IDX_DIM))),
                  pl.BlockSpec((IDX_HEADS, LANES), lambda g: (0, rb + g))],
        out_specs=pl.BlockSpec((LANES, past + LANES), lambda g: (g, 0)),
        out_shape=jax.ShapeDtypeStruct((nbs * ts, past + LANES), BF16),
        scratch_shapes=[pltpu.VMEM((past // ck, ck, LANES), I32),
                        pltpu.VMEM((ts, LANES), I32),
                        pltpu.VMEM((1, LANES), I32)],
        compiler_params=_cp("arbitrary"),
        name="sample_select",
    )(cache_ik, p_f, p_f, wi_t)


def _sample_attend_kernel(q_ref, kc_ref, vc_ref, kn_ref, vn_ref, mask_ref, o_ref, lg_ref, *, ts, past, ca):
    nt = (((1,), (1,)), ((), ()))
    qscale = (ATT_HD ** -0.5) * 1.4426950408889634
    nca = past // ca
    rows = KV_GROUP * ts

    def bias_of(m):
        b = (m.astype(F32) - 1.0) * (-NEG_BIG)
        return jnp.concatenate([b] * KV_GROUP, axis=0)

    bias_n = bias_of(mask_ref[:, past:past + ts])
    for g in range(KV_HEADS):
        qs = jnp.concatenate(
            [q_ref[:, (g * KV_GROUP + r) * ATT_HD:(g * KV_GROUP + r + 1) * ATT_HD] for r in range(KV_GROUP)], axis=0)
        qs = (qs.astype(F32) * qscale).astype(BF16)
        cols = slice(g * ATT_HD, (g + 1) * ATT_HD)
        lg_n = lax.dot_general(qs, kn_ref[:, cols].astype(BF16), nt, preferred_element_type=F32) + bias_n
        m_run = jnp.full((rows, LANES), NEG_BIG, F32)
        for c in range(nca):
            kc = kc_ref[c * ca:(c + 1) * ca, cols].astype(BF16)
            lg = lax.dot_general(qs, kc, nt, preferred_element_type=F32) + bias_of(mask_ref[:, c * ca:(c + 1) * ca])
            lg_ref[g, :, c * ca:(c + 1) * ca] = lg
            for u in range(ca // LANES):
                m_run = jnp.maximum(m_run, lg[:, u * LANES:(u + 1) * LANES])
        m_row = jnp.maximum(jnp.max(m_run, axis=-1, keepdims=True), jnp.max(lg_n, axis=-1, keepdims=True))
        ones_n = jnp.ones((ts, ATT_HD), BF16)
        acc = jnp.dot(jnp.exp2(lg_n - m_row).astype(BF16),
                      jnp.concatenate([vn_ref[:, cols].astype(BF16), ones_n], axis=1), preferred_element_type=F32)
        ones_c = jnp.ones((ca, ATT_HD), BF16)
        for c in range(nca):
            p = jnp.exp2(lg_ref[g, :, c * ca:(c + 1) * ca] - m_row).astype(BF16)
            vc = vc_ref[c * ca:(c + 1) * ca, cols].astype(BF16)
            acc = acc + jnp.dot(p, jnp.concatenate([vc, ones_c], axis=1), preferred_element_type=F32)
        o = acc[:, :ATT_HD] / acc[:, ATT_HD:]
        for r in range(KV_GROUP):
            h = g * KV_GROUP + r
            o_ref[:, h * ATT_HD:(h + 1) * ATT_HD] = o[r * ts:(r + 1) * ts].astype(o_ref.dtype)


def _sample_attend(p_bf, qa_blk, cache_k, cache_v, p_f, mask, np_, nbs, ts, past):
    kvw = KV_HEADS * ATT_HD
    hq = ATT_HEADS * ATT_HD
    ca = 4 * LANES
    assert past % ca == 0
    rb = np_ // ts
    kern = functools.partial(_sample_attend_kernel, ts=ts, past=past, ca=ca)
    return pl.pallas_call(
        kern,
        grid=(nbs,),
        in_specs=[pl.BlockSpec((ts, hq), lambda b: (rb + b, qa_blk)),
                  pl.BlockSpec((None, past, kvw), lambda b: (b, 0, 0)),
                  pl.BlockSpec((None, past, kvw), lambda b: (b, 0, 0)),
                  pl.BlockSpec((ts, kvw), lambda b: (rb + b, KA_OFF // kvw)),
                  pl.BlockSpec((ts, kvw), lambda b: (rb + b, VA_OFF // kvw)),
                  pl.BlockSpec((ts, past + LANES), lambda b: (b, 0))],
        out_specs=pl.BlockSpec((ts, hq), lambda b: (b, 0)),
        out_shape=jax.ShapeDtypeStruct((nbs * ts, hq), BF16),
        scratch_shapes=[pltpu.VMEM((KV_HEADS, KV_GROUP * ts, past), F32)],
        compiler_params=_cp("arbitrary"),
        name="sample_attend",
    )(p_bf, cache_k, cache_v, p_f, p_f, mask)


def _ret_kernel(q_ref, k_ref, v_ref, g_ref, cs_ref, sn_ref, dec_ref, xi_ref, zt_ref, gp_ref, gn_ref, s0_ref,
                y_ref, sf_ref, s_ref):
    c = pl.program_id(1)
    dk, dv = RET_DK, RET_DV

    @pl.when(c == 0)
    def _():
        s_ref[...] = s0_ref[...]

    cs = cs_ref[...]
    sn = sn_ref[...]

    def rot(x):
        xf = x.astype(F32)
        return xf * cs + pltpu.roll(xf, dk // 2, 1) * sn

    for h in range(RET_HEADS):
        q = rot(q_ref[:, h * dk:(h + 1) * dk])
        k = rot(k_ref[:, h * dk:(h + 1) * dk]) * (dk ** -0.5)
        qb = q.astype(BF16)
        kb = k.astype(BF16)
        v = v_ref[:, h * dv:(h + 1) * dv]
        inner = lax.dot_general(qb, kb, (((1,), (1,)), ((), ())), preferred_element_type=F32) * dec_ref[h]
        s = s_ref[h]
        o = jnp.dot(inner.astype(BF16), v, preferred_element_type=F32)
        o = o + jnp.dot(qb, s.astype(BF16), preferred_element_type=F32) * xi_ref[h]
        kz = (k * zt_ref[h]).T.astype(BF16)
        s_new = s * gp_ref[h] + jnp.dot(kz, v, preferred_element_type=F32)
        s_ref[h] = s_new
        sf_ref[h] = s_new

        mu = jnp.mean(o, axis=-1, keepdims=True)
        oc = o - mu
        var = jnp.mean(oc * oc, axis=-1, keepdims=True)
        yn = oc * lax.rsqrt(var + LN_EPS) * gn_ref[:, h * dv:(h + 1) * dv]
        gg = g_ref[:, h * dv:(h + 1) * dv].astype(F32)
        y_ref[:, h * dv:(h + 1) * dv] = (gg * jax.nn.sigmoid(gg) * yn).astype(y_ref.dtype)


def _retention(p_bf, row0, nb, t, cl, offset, s0, gn_g):
    nc = t // cl
    h_, dk, dv = RET_HEADS, RET_DK, RET_DV
    half = dk // 2
    pos = (offset + jnp.arange(t, dtype=I32)).astype(F32)
    inv = 1.0 / (ROPE_BASE ** (jnp.arange(half, dtype=F32) / half))
    ang = pos[:, None] * inv[None, :]
    cos, sin = jnp.cos(ang), jnp.sin(ang)
    cs2 = jnp.concatenate([cos, cos], axis=-1)
    sn2 = jnp.concatenate([-sin, sin], axis=-1)
    log_g = jnp.log1p(-jnp.exp2(-5.0 - jnp.arange(h_, dtype=F32)))
    n = jnp.arange(cl, dtype=F32)
    diff = n[:, None] - n[None, :]
    decay = jnp.where(diff >= 0, jnp.exp(log_g[:, None, None] * jnp.maximum(diff, 0.0)), 0.0)
    xi = jnp.exp(log_g[:, None] * (n + 1.0))[..., None]
    zeta = jnp.exp(log_g[:, None] * (cl - 1.0 - n))[..., None]
    gpow = jnp.broadcast_to(jnp.exp(log_g * cl)[:, None, None], (h_, 1, dv))
    rb = row0 // cl
    return pl.pallas_call(
        _ret_kernel,
        grid=(nb, nc),
        in_specs=[pl.BlockSpec((cl, h_ * dk), lambda b, c: (rb + b * nc + c, QR_OFF // (h_ * dk))),
                  pl.BlockSpec((cl, h_ * dk), lambda b, c: (rb + b * nc + c, KR_OFF // (h_ * dk))),
                  pl.BlockSpec((cl, h_ * dv), lambda b, c: (rb + b * nc + c, VR_OFF // (h_ * dv))),
                  pl.BlockSpec((cl, h_ * dv), lambda b, c: (rb + b * nc + c, GR_OFF // (h_ * dv))),
                  pl.BlockSpec((cl, dk), lambda b, c: (c, 0)),
                  pl.BlockSpec((cl, dk), lambda b, c: (c, 0)),
                  pl.BlockSpec((h_, cl, cl), lambda b, c: (0, 0, 0)),
                  pl.BlockSpec((h_, cl, 1), lambda b, c: (0, 0, 0)),
                  pl.BlockSpec((h_, cl, 1), lambda b, c: (0, 0, 0)),
                  pl.BlockSpec((h_, 1, dv), lambda b, c: (0, 0, 0)),
                  pl.BlockSpec((1, h_ * dv), lambda b, c: (0, 0)),
                  pl.BlockSpec((None, h_, dk, dv), lambda b, c: (b, 0, 0, 0))],
        out_specs=[pl.BlockSpec((cl, h_ * dv), lambda b, c: (b * nc + c, 0)),
                   pl.BlockSpec((None, h_, dk, dv), lambda b, c: (b, 0, 0, 0))],
        out_shape=[jax.ShapeDtypeStruct((nb * t, h_ * dv), BF16),
                   jax.ShapeDtypeStruct((nb, h_, dk, dv), F32)],
        scratch_shapes=[pltpu.VMEM((h_, dk, dv), F32)],
        compiler_params=_cp("arbitrary", "arbitrary"),
        name="retention",
    )(p_bf, p_bf, p_bf, p_bf, cs2, sn2, decay, xi, zeta, gpow, gn_g.reshape(1, -1), s0)


CONV_HALO = 32
CONV_ROWS = 64
CONV_BLOCK = 64


def _conv_kernel(ca_ref, cb_ref, st_ref, w_ref, b_ref, lg_ref, lb_ref, y_ref, tail_ref, sh_ref, h_ref, *, tt):
    t = pl.program_id(1)
    ext = tt + CONV_HALO

    @pl.when(t == 0)
    def _():
        sh_ref[0, 0:CONV_HALO] = st_ref[...]

    @pl.when(t > 0)
    def _():
        sh_ref[0, 0:CONV_HALO] = sh_ref[0, tt:ext]

    ca = ca_ref[...].astype(F32)
    cb = cb_ref[...].astype(F32)
    sh_ref[0, CONV_HALO:ext] = ca * jax.nn.sigmoid(cb)
    tail_ref[...] = sh_ref[0, tt:ext]
    for s in range(1, SUBLANES):
        sh_ref[s, 0:ext - SUBLANES] = sh_ref[0, s:ext - SUBLANES + s]

    lead = CONV_HALO - (CONV_W - 1)
    blk = min(CONV_BLOCK, tt)
    nblk = blk // SUBLANES

    for lc in range(CONV_CH // LANES):
        cols = slice(lc * LANES, (lc + 1) * LANES)
        taps = [jnp.broadcast_to(w_ref[k:k + 1, cols], (SUBLANES, LANES)) for k in range(CONV_W)]
        bias = jnp.broadcast_to(b_ref[:, cols], (SUBLANES, LANES))

        def conv_body(i, carry):
            r0 = pl.multiple_of(i * blk, blk)
            accs = [bias] * nblk
            for k in range(CONV_W):
                a, s = divmod(k + lead, SUBLANES)
                u = sh_ref[s, pl.ds(r0 + a * SUBLANES, blk), cols]
                accs = [accs[j] + taps[k] * u[j * SUBLANES:(j + 1) * SUBLANES] for j in range(nblk)]
            h_ref[pl.ds(r0, blk), cols] = jnp.concatenate(accs, axis=0)
            return carry

        lax.fori_loop(0, tt // blk, conv_body, 0)

    rows = min(CONV_ROWS, tt)

    def norm_body(i, carry):
        r0 = pl.multiple_of(i * rows, rows)
        hn = _layer_norm(h_ref[pl.ds(r0, rows), :], lg_ref[...], lb_ref[...])
        y_ref[pl.ds(r0, rows), :] = (hn * jax.nn.sigmoid(hn)).astype(y_ref.dtype)
        return carry

    lax.fori_loop(0, tt // rows, norm_body, 0)


def _conv_module(p_bf, row0, nb, t, tt, state, dw, dw_b, ln_g, ln_b):
    nt = t // tt
    lead = CONV_HALO - (CONV_W - 1)
    st = jnp.pad(state, ((0, 0), (lead, 0), (0, 0)))
    rb = row0 // tt
    kern = functools.partial(_conv_kernel, tt=tt)
    y, tail = pl.pallas_call(
        kern,
        grid=(nb, nt),
        in_specs=[pl.BlockSpec((tt, CONV_CH), lambda b, i: (rb + b * nt + i, CIN_OFF // CONV_CH)),
                  pl.BlockSpec((tt, CONV_CH), lambda b, i: (rb + b * nt + i, CIN_OFF // CONV_CH + 1)),
                  pl.BlockSpec((None, CONV_HALO, CONV_CH), lambda b, i: (b, 0, 0)),
                  pl.BlockSpec((CONV_W, CONV_CH), lambda b, i: (0, 0)),
                  pl.BlockSpec((1, CONV_CH), lambda b, i: (0, 0)),
                  pl.BlockSpec((1, CONV_CH), lambda b, i: (0, 0)),
                  pl.BlockSpec((1, CONV_CH), lambda b, i: (0, 0))],
        out_specs=[pl.BlockSpec((tt, CONV_CH), lambda b, i: (b * nt + i, 0)),
                   pl.BlockSpec((None, CONV_HALO, CONV_CH), lambda b, i: (b, 0, 0))],
        out_shape=[jax.ShapeDtypeStruct((nb * t, CONV_CH), BF16),
                   jax.ShapeDtypeStruct((nb, CONV_HALO, CONV_CH), F32)],
        scratch_shapes=[pltpu.VMEM((SUBLANES, tt + CONV_HALO, CONV_CH), F32),
                        pltpu.VMEM((tt, CONV_CH), F32)],
        compiler_params=_cp("arbitrary", "arbitrary"),
        name="conv_module",
    )(p_bf, p_bf, st, dw, dw_b.reshape(1, -1), ln_g.reshape(1, -1), ln_b.reshape(1, -1))
    return y, tail[:, lead:, :]


def _merge_kernel(ap_ref, rp_ref, cp_ref, as_ref, rs_ref, cs_ref, ga_ref, gb_ref, gc_ref, ba_ref, bb_ref, bc_ref,
                  wa_ref, wr_ref, wc_ref, z_ref, *, prompt_tiles):
    ga = jax.nn.sigmoid(ga_ref[...].astype(F32) + ba_ref[...])
    gb = jax.nn.sigmoid(gb_ref[...].astype(F32) + bb_ref[...])
    gc = jax.nn.sigmoid(gc_ref[...].astype(F32) + bc_ref[...])

    is_prompt = pl.program_id(1) < prompt_tiles
    ya = jnp.dot(jnp.where(is_prompt, ap_ref[...], as_ref[...]), wa_ref[...], preferred_element_type=F32)
    yb = jnp.dot(jnp.where(is_prompt, rp_ref[...], rs_ref[...]), wr_ref[...], preferred_element_type=F32)
    yc = jnp.dot(jnp.where(is_prompt, cp_ref[...], cs_ref[...]), wc_ref[...], preferred_element_type=F32)
    z_ref[...] = (ga * ya + gb * yb + gc * yc).astype(z_ref.dtype)


def _merge(att_p, ret_p, cnv_p, att_s, ret_s, cnv_s, p_bf, b_gate, wa, wr, wc, tm, tn):
    np_, ns = att_p.shape[0], att_s.shape[0]
    assert np_ % tm == 0 and ns % tm == 0
    m = np_ + ns
    pt = np_ // tm
    d = wa.shape[1]
    nj = d // tn
    g0 = GATE_OFF // tn
    bg = b_gate.reshape(1, -1)

    def prompt_rows(j, i):
        return (jnp.minimum(i, pt - 1), 0)

    def sample_rows(j, i):
        return (jnp.maximum(i - pt, 0), 0)

    return pl.pallas_call(
        functools.partial(_merge_kernel, prompt_tiles=pt),
        grid=(nj, m // tm),
        in_specs=[pl.BlockSpec((tm, att_p.shape[1]), prompt_rows),
                  pl.BlockSpec((tm, ret_p.shape[1]), prompt_rows),
                  pl.BlockSpec((tm, cnv_p.shape[1]), prompt_rows),
                  pl.BlockSpec((tm, att_s.shape[1]), sample_rows),
                  pl.BlockSpec((tm, ret_s.shape[1]), sample_rows),
                  pl.BlockSpec((tm, cnv_s.shape[1]), sample_rows),
                  pl.BlockSpec((tm, tn), lambda j, i: (i, g0 + j)),
                  pl.BlockSpec((tm, tn), lambda j, i: (i, g0 + nj + j)),
                  pl.BlockSpec((tm, tn), lambda j, i: (i, g0 + 2 * nj + j)),
                  pl.BlockSpec((1, tn), lambda j, i: (0, j)),
                  pl.BlockSpec((1, tn), lambda j, i: (0, nj + j)),
                  pl.BlockSpec((1, tn), lambda j, i: (0, 2 * nj + j)),
                  pl.BlockSpec((wa.shape[0], tn), lambda j, i: (0, j)),
                  pl.BlockSpec((wr.shape[0], tn), lambda j, i: (0, j)),
                  pl.BlockSpec((wc.shape[0], tn), lambda j, i: (0, j))],
        out_specs=pl.BlockSpec((tm, tn), lambda j, i: (i, j)),
        out_shape=jax.ShapeDtypeStruct((m, d), BF16),
        compiler_params=_cp("arbitrary", "arbitrary"),
        name="branch_merge",
    )(att_p, ret_p, cnv_p, att_s, ret_s, cnv_s, p_bf, p_bf, p_bf, bg, bg, bg, wa, wr, wc)


def _route(x, w_hl, b):
    tm = x.shape[0]
    x_hi = x.astype(BF16)
    x_lo = (x - x_hi.astype(F32)).astype(BF16)
    d_hi = jnp.dot(x_hi, w_hl, preferred_element_type=F32)
    d_lo = jnp.dot(x_lo, w_hl[:, :LANES], preferred_element_type=F32)
    logits = d_hi[:, :LANES] + d_hi[:, LANES:] + d_lo + b
    lane = lax.broadcasted_iota(I32, (tm, LANES), 1)
    valid = lane < N_EXPERTS
    lm = jnp.where(valid, logits, NEG_BIG)
    e = jnp.where(valid, jnp.exp(lm - jnp.max(lm, axis=-1, keepdims=True)), 0.0)
    aff = e / jnp.sum(e, axis=-1, keepdims=True)

    def top2(vals):
        v1 = jnp.max(vals, axis=-1, keepdims=True)
        i1 = jnp.min(jnp.where(vals == v1, lane, LANES), axis=-1, keepdims=True)
        rest = jnp.where(lane == i1, -2.0, vals)
        v2 = jnp.max(rest, axis=-1, keepdims=True)
        i2 = jnp.min(jnp.where(rest == v2, lane, LANES), axis=-1, keepdims=True)
        return v1, i1, v2, i2

    grp = lane // EXP_PER_GROUP
    best = jnp.zeros((tm, 1), I32)
    best_score = None
    for g in range(N_GROUPS):
        v1, _, v2, _ = top2(jnp.where(grp == g, aff, -1.0))
        score = v1 + v2
        if g == 0:
            best_score = score
        else:
            better = score > best_score
            best = jnp.where(better, g, best)
            best_score = jnp.where(better, score, best_score)
    v1, i1, v2, i2 = top2(jnp.where(grp == best, aff, -1.0))
    tot = v1 + v2
    return (jnp.where(lane == 0, i1.astype(F32), 0.0) + jnp.where(lane == 1, i2.astype(F32), 0.0)
            + jnp.where(lane == 2, v1 / tot, 0.0) + jnp.where(lane == 3, v2 / tot, 0.0))


def _outproj_ln_kernel(z_ref, w_ref, x_ref, g_ref, b_ref, wr_ref, br_ref, of_ref, ob_ref, rt_ref, *, alpha):
    mix = jnp.dot(z_ref[...], w_ref[...], preferred_element_type=F32)
    y = _layer_norm(alpha * x_ref[...] + mix, g_ref[...], b_ref[...])
    of_ref[...] = y
    ob_ref[...] = y.astype(BF16)
    rt_ref[...] = _route(y, wr_ref[...], br_ref[...])


def _outproj_ln(z, w_out, x, g, b, w_router, b_router, alpha, tm):
    m, d = x.shape
    wr = jnp.pad(w_router, ((0, 0), (0, LANES - N_EXPERTS)))
    wr_hi = wr.astype(BF16)
    wr = jnp.concatenate([wr_hi, (wr - wr_hi.astype(F32)).astype(BF16)], axis=1)
    br = jnp.pad(b_router, (0, LANES - N_EXPERTS)).reshape(1, -1)
    kern = functools.partial(_outproj_ln_kernel, alpha=alpha)
    return pl.pallas_call(
        kern,
        grid=(m // tm,),
        in_specs=[pl.BlockSpec((tm, d), lambda i: (i, 0)),
                  pl.BlockSpec((d, d), lambda i: (0, 0)),
                  pl.BlockSpec((tm, d), lambda i: (i, 0)),
                  pl.BlockSpec((1, d), lambda i: (0, 0)),
                  pl.BlockSpec((1, d), lambda i: (0, 0)),
                  pl.BlockSpec((d, 2 * LANES), lambda i: (0, 0)),
                  pl.BlockSpec((1, LANES), lambda i: (0, 0))],
        out_specs=[pl.BlockSpec((tm, d), lambda i: (i, 0)), pl.BlockSpec((tm, d), lambda i: (i, 0)),
                   pl.BlockSpec((tm, LANES), lambda i: (i, 0))],
        out_shape=[jax.ShapeDtypeStruct((m, d), F32), jax.ShapeDtypeStruct((m, d), BF16),
                   jax.ShapeDtypeStruct((m, LANES), F32)],
        compiler_params=_cp("arbitrary"),
        name="out_proj_ln1_route",
    )(z, w_out, x, g.reshape(1, -1), b.reshape(1, -1), wr, br)


MOE_TILE = 256
MOE_ISSUE_GROUPS = 8


def _gather_rows(idx_ref, src_hbm, dst_ref, sem, lo, hi):
    for r in range(lo, hi):
        pltpu.make_async_copy(src_hbm.at[pl.ds(idx_ref[0, 0, r], 1)], dst_ref.at[pl.ds(r, 1)], sem).start()


def _wait_rows(src_hbm, dst_ref, sem, n):
    pltpu.make_async_copy(src_hbm.at[pl.ds(0, n)], dst_ref, sem).wait()


def _moe_expert_kernel(te_ref, cur_ref, nxt_ref, x_hbm, w1_ref, w3_ref, w2_ref, o_ref,
                       xg_ref, sem, w1b_ref, w3b_ref, w2b_ref):
    t = pl.program_id(0)
    nt = pl.num_programs(0)
    tile = xg_ref.shape[1]
    slot = t % 2

    @pl.when(t == 0)
    def _():
        _gather_rows(cur_ref, x_hbm, xg_ref.at[0], sem.at[0], 0, tile)

    @pl.when((t == 0) | (te_ref[t] != te_ref[jnp.maximum(t - 1, 0)]))
    def _():
        w1b_ref[...] = w1_ref[...].astype(BF16)
        w3b_ref[...] = w3_ref[...].astype(BF16)
        w2b_ref[...] = w2_ref[...].astype(BF16)

    _wait_rows(x_hbm, xg_ref.at[slot], sem.at[slot], tile)
    d = xg_ref.shape[2]
    kc, rc = d // MOE_ISSUE_GROUPS, tile // MOE_ISSUE_GROUPS
    h1 = h3 = None
    for c in range(MOE_ISSUE_GROUPS):
        xk = xg_ref[slot, :, c * kc:(c + 1) * kc].astype(BF16)
        p1 = jnp.dot(xk, w1b_ref[c * kc:(c + 1) * kc, :], preferred_element_type=F32)
        p3 = jnp.dot(xk, w3b_ref[c * kc:(c + 1) * kc, :], preferred_element_type=F32)
        h1 = p1 if h1 is None else h1 + p1
        h3 = p3 if h3 is None else h3 + p3
        _gather_rows(nxt_ref, x_hbm, xg_ref.at[1 - slot], sem.at[1 - slot], c * rc, (c + 1) * rc)
    h = h1 * jax.nn.sigmoid(h1) * h3
    o_ref[...] = jnp.dot(h.astype(BF16), w2b_ref[...], preferred_element_type=F32)

    @pl.when(t == nt - 1)
    def _():
        _wait_rows(x_hbm, xg_ref.at[1 - slot], sem.at[1 - slot], tile)


def _moe_experts(x, w1, w3, w2, tile_expert, src_tiles):
    n, d = x.shape
    _, _, de = w1.shape
    nt = src_tiles.shape[0] - 1
    tile = src_tiles.shape[2]
    grid_spec = pltpu.PrefetchScalarGridSpec(
        num_scalar_prefetch=1,
        grid=(nt,),
        in_specs=[pl.BlockSpec((1, 1, tile), lambda t, te: (t, 0, 0), memory_space=pltpu.SMEM),
                  pl.BlockSpec((1, 1, tile), lambda t, te: (t + 1, 0, 0), memory_space=pltpu.SMEM),
                  pl.BlockSpec(memory_space=pl.ANY),
                  pl.BlockSpec((None, d, de), lambda t, te: (te[t], 0, 0)),
                  pl.BlockSpec((None, d, de), lambda t, te: (te[t], 0, 0)),
                  pl.BlockSpec((None, de, d), lambda t, te: (te[t], 0, 0))],
        out_specs=pl.BlockSpec((tile, d), lambda t, te: (t, 0)),
        scratch_shapes=[pltpu.VMEM((2, tile, d), F32),
                        pltpu.SemaphoreType.DMA((2,)),
                        pltpu.VMEM((d, de), BF16),
                        pltpu.VMEM((d, de), BF16),
                        pltpu.VMEM((de, d), BF16)],
    )
    return pl.pallas_call(
        _moe_expert_kernel,
        grid_spec=grid_spec,
        out_shape=jax.ShapeDtypeStruct((nt * tile, d), F32),
        compiler_params=_cp("arbitrary"),
        name="moe_experts",
    )(tile_expert, src_tiles, src_tiles, x, w1, w3, w2)


def _moe_combine_kernel(cur_ref, nxt_ref, y_hbm, rt_ref, x_ref, g_ref, b_ref, of_ref, ob_ref, yg_ref, sem, *, alpha):
    t = pl.program_id(0)
    nt = pl.num_programs(0)
    tm = x_ref.shape[0]
    slot = t % 2

    @pl.when(t == 0)
    def _():
        _gather_rows(cur_ref, y_hbm, yg_ref.at[0], sem.at[0], 0, 2 * tm)

    _gather_rows(nxt_ref, y_hbm, yg_ref.at[1 - slot], sem.at[1 - slot], 0, 2 * tm)
    _wait_rows(y_hbm, yg_ref.at[slot], sem.at[slot], 2 * tm)
    rt = rt_ref[...]
    moe = rt[:, 2:3] * yg_ref[slot, 0:tm] + rt[:, 3:4] * yg_ref[slot, tm:2 * tm]
    y = _layer_norm(alpha * x_ref[...] + moe, g_ref[...], b_ref[...])
    of_ref[...] = y
    ob_ref[...] = y.astype(BF16)

    @pl.when(t == nt - 1)
    def _():
        _wait_rows(y_hbm, yg_ref.at[1 - slot], sem.at[1 - slot], 2 * tm)


def _moe_combine_ln(y_sorted, pos_tiles, route, x, g, b, alpha, tm):
    m, d = x.shape
    kern = functools.partial(_moe_combine_kernel, alpha=alpha)
    return pl.pallas_call(
        kern,
        grid=(m // tm,),
        in_specs=[pl.BlockSpec((1, 1, 2 * tm), lambda t: (t, 0, 0), memory_space=pltpu.SMEM),
                  pl.BlockSpec((1, 1, 2 * tm), lambda t: (t + 1, 0, 0), memory_space=pltpu.SMEM),
                  pl.BlockSpec(memory_space=pl.ANY),
                  pl.BlockSpec((tm, LANES), lambda t: (t, 0)),
                  pl.BlockSpec((tm, d), lambda t: (t, 0)),
                  pl.BlockSpec((1, d), lambda t: (0, 0)),
                  pl.BlockSpec((1, d), lambda t: (0, 0))],
        out_specs=[pl.BlockSpec((tm, d), lambda t: (t, 0)), pl.BlockSpec((tm, d), lambda t: (t, 0))],
        out_shape=[jax.ShapeDtypeStruct((m, d), F32), jax.ShapeDtypeStruct((m, d), BF16)],
        scratch_shapes=[pltpu.VMEM((2, 2 * tm, d), F32), pltpu.SemaphoreType.DMA((2,))],
        compiler_params=_cp("arbitrary"),
        name="moe_combine_ln2",
    )(pos_tiles, pos_tiles, y_sorted, route, x, g.reshape(1, -1), b.reshape(1, -1))


def _moe_layout(route, tile, tm):
    n = route.shape[0]
    e_flat = route[:, 0:2].astype(I32).reshape(-1)
    npair = 2 * n
    nt = -(-npair // tile) + N_EXPERTS
    onehot = (e_flat[:, None] == jnp.arange(N_EXPERTS, dtype=I32)[None, :]).astype(I32)
    csum = jnp.cumsum(onehot, axis=0)
    counts = csum[-1]
    rank = jnp.sum((csum - onehot) * onehot, axis=1)
    padded = (counts + tile - 1) // tile * tile
    ends = jnp.cumsum(padded)
    starts = ends - padded
    pos = starts[e_flat] + rank
    rows = nt * tile
    src = jnp.zeros((rows,), I32).at[pos].set(jnp.arange(npair, dtype=I32) // 2, unique_indices=True)
    tile_expert = jnp.minimum(jnp.searchsorted(ends, jnp.arange(nt, dtype=I32) * tile, side='right'),
                              N_EXPERTS - 1).astype(I32)
    src_tiles = jnp.concatenate([src, jnp.zeros((tile,), I32)]).reshape(nt + 1, 1, tile)
    pos_t = pos.reshape(n // tm, tm, 2).transpose(0, 2, 1).reshape(n // tm, 1, 2 * tm)
    pos_tiles = jnp.concatenate([pos_t, jnp.zeros((1, 1, 2 * tm), I32)], axis=0)
    return tile_expert, src_tiles, pos_tiles


def kernel(x_prompt, x_sample, cache_k, cache_v, cache_idx_k, state_ret, state_conv, w_in, b_gate, w_att_o, ret_gn_g,
           w_ret_o, conv_dw, conv_dw_b, conv_ln_g, conv_ln_b, w_conv_o, w_out, ln1_g, ln1_b, w_router, b_router,
           moe_w1, moe_w3, moe_w2, ln2_g, ln2_b):
    nbp, tp, d = x_prompt.shape
    nbs, ts, _ = x_sample.shape
    depth = w_in.shape[0]
    past = cache_k.shape[2]
    np_, ns = nbp * tp, nbs * ts
    n = np_ + ns
    alpha = (2 * depth) ** 0.25
    kvw = KV_HEADS * ATT_HD

    tm = _pow2_tile(n, 512)
    tq_p = _pow2_tile(tp, 256)
    cl_p = _pow2_tile(tp, 256)
    tt_p = _pow2_tile(tp, 256)
    ls_true = past + ts
    ls_pad = -(-ls_true // (2 * LANES)) * (2 * LANES)
    assert tp % (2 * LANES) == 0 and np_ % max(tq_p, cl_p, tt_p, ts) == 0

    x = jnp.concatenate([x_prompt.reshape(np_, d), x_sample.reshape(ns, d)], axis=0)
    xb = x.astype(BF16)

    c1 = ATT_HEADS * ATT_HD
    c2 = c1 + F32_USED
    ks, vs, iks, rps, cps, kss, vss, ikss, rss, css = ([] for _ in range(10))
    zero_ret = jnp.zeros((nbp, RET_HEADS, RET_DK, RET_DV), F32)
    zero_conv = jnp.zeros((nbp, CONV_W - 1, CONV_CH), F32)

    for l in range(depth):
        w_l = w_in[l]
        w_bf = jnp.concatenate([w_l[:, c2:], w_l[:, :c1]], axis=1).astype(BF16)
        w_f = jnp.pad(w_l[:, c1:c2], ((0, 0), (0, F32_WIDTH - F32_USED))).astype(BF16)
        p_bf = _matmul(xb, w_bf, BF16, tm, 1536)
        p_f = _matmul(xb, w_f, F32, tm, F32_WIDTH)

        k_new = p_f[:, KA_OFF:KA_OFF + kvw]
        v_new = p_f[:, VA_OFF:VA_OFF + kvw]
        ik_new = p_f[:, KI_OFF:KI_OFF + IDX_DIM]
        wi_t = p_f[:, WI_OFF:WI_OFF + IDX_HEADS].T
        ks.append(k_new[:np_].reshape(nbp, tp, KV_HEADS, ATT_HD))
        vs.append(v_new[:np_].reshape(nbp, tp, KV_HEADS, ATT_HD))
        iks.append(ik_new[:np_].reshape(nbp, tp, IDX_DIM))
        kss.append(k_new[np_:].reshape(nbs, ts, KV_HEADS, ATT_HD))
        vss.append(v_new[np_:].reshape(nbs, ts, KV_HEADS, ATT_HD))
        ikss.append(ik_new[np_:].reshape(nbs, ts, IDX_DIM))

        nq = tp // tq_p
        hq = ATT_HEADS * ATT_HD
        qa_blk = (GATE_OFF + 3 * d) // hq
        att_p = _attention(
            p_bf, lambda tq: ((tq, hq), lambda b, j: (b * nq + j, qa_blk)),
            p_f, lambda lp: ((lp, kvw), lambda b, j: (b, KA_OFF // kvw)),
            p_f, lambda lp: ((lp, kvw), lambda b, j: (b, VA_OFF // kvw)),
            p_f, lambda tq: ((tq, IDX_HEADS * IDX_DIM), lambda b, j: (b * nq + j, QI_OFF // (IDX_HEADS * IDX_DIM))),
            p_f, lambda lp: ((lp, LANES), lambda b, j: (b, KI_OFF // LANES)),
            wi_t[:, :np_], lambda tq: ((IDX_HEADS, tq), lambda b, j: (0, b * nq + j)),
            nb=nbp, t=tp, l_pad=tp, l_true=tp, offset=0, tq=tq_p)

        mask_s = _sample_select(cache_idx_k[l], p_f, wi_t, np_, nbs, ts, past)
        att_s = _sample_attend(p_bf, qa_blk, cache_k[l].reshape(nbs, past, kvw), cache_v[l].reshape(nbs, past, kvw),
                               p_f, mask_s, np_, nbs, ts, past)

        ret_p, rs_p = _retention(p_bf, 0, nbp, tp, cl_p, 0, zero_ret, ret_gn_g[l])
        ret_s, rs_s = _retention(p_bf, np_, nbs, ts, ts, past, state_ret[l], ret_gn_g[l])
        rps.append(rs_p)
        rss.append(rs_s)

        cnv_p, cs_p = _conv_module(p_bf, 0, nbp, tp, tt_p, zero_conv, conv_dw[l], conv_dw_b[l], conv_ln_g[l],
                                   conv_ln_b[l])
        cnv_s, cs_s = _conv_module(p_bf, np_, nbs, ts, ts, state_conv[l], conv_dw[l], conv_dw_b[l], conv_ln_g[l],
                                   conv_ln_b[l])
        cps.append(cs_p)
        css.append(cs_s)

        z = _merge(att_p, ret_p, cnv_p, att_s, ret_s, cnv_s, p_bf, b_gate[l], w_att_o[l].astype(BF16),
                   w_ret_o[l].astype(BF16), w_conv_o[l].astype(BF16), math.gcd(tm, ns), 512)
        tm_c = _pow2_tile(n, 256)
        x, xb, route = _outproj_ln(z, w_out[l].astype(BF16), x, ln1_g[l], ln1_b[l], w_router, b_router, alpha, tm_c)
        tile_expert, src_tiles, pos_tiles = _moe_layout(route, MOE_TILE, tm_c)
        y_sorted = _moe_experts(x, moe_w1[l], moe_w3[l], moe_w2[l], tile_expert, src_tiles)
        x, xb = _moe_combine_ln(y_sorted, pos_tiles, route, x, ln2_g[l], ln2_b[l], alpha, tm_c)

    y_prompt = x[:np_].reshape(nbp, tp, d)
    y_sample = x[np_:].reshape(nbs, ts, d)
    st = jnp.stack
    return (y_prompt, y_sample, st(ks), st(vs), st(iks), st(rps), st(cps), st(kss), st(vss), st(ikss), st(rss), st(css))
```

```python
import functools
import math

import jax
import jax.numpy as jnp
from jax import lax
from jax.experimental import pallas as pl
from jax.experimental.pallas import tpu as pltpu

F32 = jnp.float32
BF16 = jnp.bfloat16
I32 = jnp.int32

CHUNK = 64
TOPK_MAX = 256
ATT_HEADS, ATT_HD, KV_HEADS = 8, 128, 2
KV_GROUP = ATT_HEADS // KV_HEADS
IDX_HEADS, IDX_DIM = 8, 64
RET_HEADS, RET_DK, RET_DV = 8, 128, 256
ROPE_BASE = 10000.0
CONV_CH, CONV_W = 1024, 31
N_EXPERTS, N_GROUPS = 16, 4
EXP_PER_GROUP = N_EXPERTS // N_GROUPS
LN_EPS = 1e-5

LANES = 128
SUBLANES = 8

INT_MIN = -2 ** 31
NEG_BIG = -1e30

QR_OFF = 0
KR_OFF = QR_OFF + RET_HEADS * RET_DK
VR_OFF = KR_OFF + RET_HEADS * RET_DK
GR_OFF = VR_OFF + RET_HEADS * RET_DV
CIN_OFF = GR_OFF + RET_HEADS * RET_DV
GATE_OFF = CIN_OFF + 2 * CONV_CH
KA_OFF = 0
VA_OFF = KA_OFF + KV_HEADS * ATT_HD
QI_OFF = VA_OFF + KV_HEADS * ATT_HD
KI_OFF = QI_OFF + IDX_HEADS * IDX_DIM
WI_OFF = KI_OFF + IDX_DIM
F32_USED = WI_OFF + IDX_HEADS
F32_WIDTH = -(-F32_USED // LANES) * LANES

VMEM_LIMIT = 56 * 1024 * 1024


def _cp(*sem):
    return pltpu.CompilerParams(dimension_semantics=sem, vmem_limit_bytes=VMEM_LIMIT)


def _pow2_tile(n, pref):
    t = pref
    while n % t:
        t //= 2
    return t


def _layer_norm(x, g, b):
    mu = jnp.mean(x, axis=-1, keepdims=True)
    xc = x - mu
    var = jnp.mean(xc * xc, axis=-1, keepdims=True)
    return xc * lax.rsqrt(var + LN_EPS) * g + b


def _mm_kernel(x_ref, w_ref, o_ref):
    o_ref[...] = jnp.dot(x_ref[...], w_ref[...], preferred_element_type=F32).astype(o_ref.dtype)


def _matmul(x, w, out_dtype, tm, tn):
    m, k = x.shape
    n = w.shape[1]
    return pl.pallas_call(
        _mm_kernel,
        grid=(n // tn, m // tm),
        in_specs=[pl.BlockSpec((tm, k), lambda j, i: (i, 0)),
                  pl.BlockSpec((k, tn), lambda j, i: (0, j))],
        out_specs=pl.BlockSpec((tm, tn), lambda j, i: (i, j)),
        out_shape=jax.ShapeDtypeStruct((m, n), out_dtype),
        compiler_params=_cp("arbitrary", "arbitrary"),
        name="in_proj",
    )(x, w)


def _attn_kernel(q_ref, k_ref, v_ref, qi_ref, ki_ref, wi_ref, o_ref,
                 key_ref, lg_ref, mx_ref, acc_ref, m_ref,
                 *, tq, nq, ck, nck, offset, l_true, k_sel, tie_bits):
    j = pl.program_id(1)
    nsub = ck // LANES
    qlane = lax.broadcasted_iota(I32, (1, tq), 1)
    krow = lax.broadcasted_iota(I32, (ck, tq), 0)
    pos = offset + j * tq + qlane
    limit = jnp.minimum((pos // CHUNK + 1) * CHUNK, l_true)
    if nq == 1:
        n_act = (min(((offset + tq - 1) // CHUNK + 1) * CHUNK, l_true) + ck - 1) // ck
    else:
        last_pos = offset + (j + 1) * tq - 1
        max_limit = jnp.minimum((last_pos // CHUNK + 1) * CHUNK, l_true)
        n_act = (max_limit + ck - 1) // ck

    qi = (qi_ref[...] * (IDX_DIM ** -0.5)).astype(BF16)
    qis = [qi[:, h * IDX_DIM:(h + 1) * IDX_DIM] for h in range(IDX_HEADS)]
    stacked = tq % LANES == 0
    if stacked:
        qstack = jnp.concatenate(qis, axis=0)
    wt = wi_ref[...] * (IDX_HEADS ** -0.5)
    wrows = [wt[h:h + 1, :] for h in range(IDX_HEADS)]
    nt = (((1,), (1,)), ((), ()))

    def score_body(c, carry):
        for u in range(nsub):
            r0 = pl.multiple_of(c * ck + u * LANES, LANES)
            kiu = ki_ref[pl.ds(r0, LANES), :][:, :IDX_DIM].astype(BF16)
            if stacked:
                d_all = lax.dot_general(kiu, qstack, nt, preferred_element_type=F32)
                ds = [d_all[:, h * tq:(h + 1) * tq] for h in range(IDX_HEADS)]
            else:
                ds = [lax.dot_general(kiu, qis[h], nt, preferred_element_type=F32) for h in range(IDX_HEADS)]
            s = jnp.zeros((LANES, tq), F32)
            for h in range(IDX_HEADS):
                s = s + jnp.maximum(ds[h], 0.0) * wrows[h]
            bits = lax.bitcast_convert_type(s, I32)
            bits = jnp.where(bits == INT_MIN, 0, bits)
            key = jnp.where(bits >= 0, bits, bits ^ 0x7FFFFFFF)
            kpos = r0 + lax.broadcasted_iota(I32, (LANES, tq), 0)
            key_ref[c, u * LANES:(u + 1) * LANES, :] = jnp.where(kpos < limit, key, INT_MIN)
        return carry

    lax.fori_loop(0, n_act, score_body, 0)

    def count(pred):
        def body(c, acc):
            p = jnp.where(pred(key_ref[c], c * ck + krow), 1.0, 0.0)
            parts = [p[r * SUBLANES:(r + 1) * SUBLANES] for r in range(ck // SUBLANES)]
            while len(parts) > 1:
                parts = [a + b for a, b in zip(parts[0::2], parts[1::2])]
            return acc + parts[0]
        acc = lax.fori_loop(0, n_act, body, jnp.zeros((SUBLANES, tq), F32))
        return jnp.sum(acc, axis=0, keepdims=True)

    def bisect(i, tu):
        cand_u = tu | (jnp.int32(1) << (31 - i))
        cand_s = cand_u ^ INT_MIN
        cnt = count(lambda key, idx: key >= cand_s)
        return jnp.where(cnt >= k_sel, cand_u, tu)

    tu = lax.fori_loop(0, 32, bisect, jnp.zeros((1, tq), I32))
    ts = tu ^ INT_MIN
    c_gt = count(lambda key, idx: key > ts)
    c_ge = count(lambda key, idx: key >= ts)
    want = k_sel - c_gt

    eye = jnp.where(lax.broadcasted_iota(I32, (tq, tq), 0) == lax.broadcasted_iota(I32, (tq, tq), 1), 1.0, 0.0).astype(BF16)
    ts_adm = jnp.maximum(ts, INT_MIN + 1)
    m_ref[...] = jnp.full((1, tq), 2 ** 30, I32)

    @pl.when(jnp.max(c_ge) > k_sel)
    def _():
        def tie_bisect(i, m):
            cand = m | (jnp.int32(1) << (tie_bits - 1 - i))
            cnt = count(lambda key, idx: (key == ts) & (idx < cand))
            return jnp.where(cnt < want, cand, m)
        m_ref[...] = lax.fori_loop(0, tie_bits, tie_bisect, jnp.zeros((1, tq), I32))

    m_last = m_ref[...]

    qscale = (ATT_HD ** -0.5) * 1.4426950408889634
    groups = range(KV_HEADS)
    qss = []
    for g in groups:
        qs = jnp.concatenate(
            [q_ref[:, (g * KV_GROUP + r) * ATT_HD:(g * KV_GROUP + r + 1) * ATT_HD] for r in range(KV_GROUP)], axis=0)
        qss.append((qs.astype(F32) * qscale).astype(BF16))
    mx_ref[...] = jnp.full(mx_ref.shape, NEG_BIG, F32)

    def logit_body(c, carry):
        r0 = pl.multiple_of(c * ck, ck)
        key = key_ref[c]
        sel = (key > ts_adm) | ((key == ts_adm) & (c * ck + krow <= m_last))
        sel_t = jnp.where(sel, 1.0, 0.0).astype(BF16)
        sel_q = lax.dot_general(eye, sel_t, nt, preferred_element_type=F32)
        b = (sel_q - 1.0) * (-NEG_BIG)
        b4 = jnp.concatenate([b] * KV_GROUP, axis=0)
        for g in groups:
            kc = k_ref[pl.ds(r0, ck), g * ATT_HD:(g + 1) * ATT_HD].astype(BF16)
            lg = lax.dot_general(qss[g], kc, nt, preferred_element_type=F32) + b4
            lg_ref[g, c] = lg
            m = mx_ref[g]
            for u in range(nsub):
                m = jnp.maximum(m, lg[:, u * LANES:(u + 1) * LANES])
            mx_ref[g] = m
        return carry

    lax.fori_loop(0, n_act, logit_body, 0)
    m_rows = [jnp.max(mx_ref[g], axis=-1, keepdims=True) for g in groups]
    acc_ref[...] = jnp.zeros(acc_ref.shape, F32)
    ones_cols = jnp.ones((ck, ATT_HD), BF16)

    def pv_body(c, carry):
        r0 = pl.multiple_of(c * ck, ck)
        for g in groups:
            p = jnp.exp2(lg_ref[g, c] - m_rows[g]).astype(BF16)
            vc = v_ref[pl.ds(r0, ck), g * ATT_HD:(g + 1) * ATT_HD].astype(BF16)
            acc_ref[g] += jnp.dot(p, jnp.concatenate([vc, ones_cols], axis=1), preferred_element_type=F32)
        return carry

    lax.fori_loop(0, n_act, pv_body, 0)
    for g in groups:
        a = acc_ref[g]
        o = a[:, :ATT_HD] / a[:, ATT_HD:]
        for r in range(KV_GROUP):
            h = g * KV_GROUP + r
            o_ref[:, h * ATT_HD:(h + 1) * ATT_HD] = o[r * tq:(r + 1) * tq].astype(o_ref.dtype)


def _attention(q_arr, q_map, k_arr, k_map, v_arr, v_map, qi_arr, qi_map, ki_arr, ki_map, wi_arr, wi_map,
               *, nb, t, l_pad, l_true, offset, tq):
    ck = 2 * LANES
    nck = l_pad // ck
    assert l_pad % ck == 0 and t % tq == 0 and nck * (ck // LANES) < 256
    nq = t // tq
    k_sel = min(TOPK_MAX, l_true // 4)
    tie_bits = max(1, (l_pad - 1).bit_length())
    kern = functools.partial(_attn_kernel, tq=tq, nq=nq, ck=ck, nck=nck, offset=offset, l_true=l_true,
                             k_sel=k_sel, tie_bits=tie_bits)
    hq = ATT_HEADS * ATT_HD
    return pl.pallas_call(
        kern,
        grid=(nb, nq),
        in_specs=[pl.BlockSpec(*q_map(tq)), pl.BlockSpec(*k_map(l_pad)), pl.BlockSpec(*v_map(l_pad)),
                  pl.BlockSpec(*qi_map(tq)), pl.BlockSpec(*ki_map(l_pad)), pl.BlockSpec(*wi_map(tq))],
        out_specs=pl.BlockSpec((tq, hq), lambda b, j: (b * nq + j, 0)),
        out_shape=jax.ShapeDtypeStruct((nb * t, hq), BF16),
        scratch_shapes=[pltpu.VMEM((nck, ck, tq), I32),
                        pltpu.VMEM((KV_HEADS, nck, KV_GROUP * tq, ck), F32),
                        pltpu.VMEM((KV_HEADS, KV_GROUP * tq, LANES), F32),
                        pltpu.VMEM((KV_HEADS, KV_GROUP * tq, 2 * ATT_HD), F32),
                        pltpu.VMEM((1, tq), I32)],
        compiler_params=_cp("arbitrary", "arbitrary"),
        name="dsa_attention",
    )(q_arr, k_arr, v_arr, qi_arr, ki_arr, wi_arr)


def _order_key(s):
    bits = lax.bitcast_convert_type(s, I32)
    bits = jnp.where(bits == INT_MIN, 0, bits)
    return jnp.where(bits >= 0, bits, bits ^ 0x7FFFFFFF)


def _sample_select_kernel(kic_ref, new_ref, qi_ref, wi_ref, mask_ref, key_ref, keyn_ref, m_ref,
                          *, ts, gs, past, ck, k_sel, tie_bits):
    nck = past // ck
    nsub = ck // LANES
    l_true = past + ts
    lane = lax.broadcasted_iota(I32, (1, LANES), 1)
    lane_s = lane // ts
    pos = past + lane % ts
    limit = jnp.minimum((pos // CHUNK + 1) * CHUNK, l_true)
    nt = (((1,), (1,)), ((), ()))

    qi = (qi_ref[...] * (IDX_DIM ** -0.5)).astype(BF16)
    qstack = jnp.concatenate([qi[:, h * IDX_DIM:(h + 1) * IDX_DIM] for h in range(IDX_HEADS)], axis=0)
    wt = wi_ref[...] * (IDX_HEADS ** -0.5)
    wrows = [wt[h:h + 1, :] for h in range(IDX_HEADS)]

    def own_stream(d, rows):
        out = d[(gs - 1) * rows:gs * rows]
        for s in range(gs - 2, -1, -1):
            out = jnp.where(lane_s == s, d[s * rows:(s + 1) * rows], out)
        return out

    def scores(kmat, rows):
        d_all = lax.dot_general(kmat, qstack, nt, preferred_element_type=F32)
        s = jnp.zeros((rows, LANES), F32)
        for h in range(IDX_HEADS):
            s = s + jnp.maximum(own_stream(d_all[:, h * LANES:(h + 1) * LANES], rows), 0.0) * wrows[h]
        return _order_key(s)

    def score_body(c, carry):
        for u in range(nsub):
            r0 = pl.multiple_of(c * ck + u * LANES, LANES)
            kmat = jnp.concatenate([kic_ref[s, pl.ds(r0, LANES), :] for s in range(gs)], axis=0).astype(BF16)
            kpos = r0 + lax.broadcasted_iota(I32, (LANES, LANES), 0)
            key_ref[c, u * LANES:(u + 1) * LANES, :] = jnp.where(kpos < limit, scores(kmat, LANES), INT_MIN)
        return carry

    lax.fori_loop(0, nck, score_body, 0)
    kn = new_ref[:, :IDX_DIM].astype(BF16)
    npos = past + lax.broadcasted_iota(I32, (ts, LANES), 0)
    keyn_ref[...] = jnp.where(npos < limit, scores(kn, ts), INT_MIN)

    krow = lax.broadcasted_iota(I32, (ck, LANES), 0)

    def tree(p, rows):
        parts = [p[r * SUBLANES:(r + 1) * SUBLANES] for r in range(rows // SUBLANES)]
        while len(parts) > 1:
            parts = [a + b for a, b in zip(parts[0::2], parts[1::2])]
        return parts[0]

    def count(pred):
        def body(c, acc):
            p = jnp.where(pred(key_ref[c], c * ck + krow), 1.0, 0.0)
            return acc + tree(p, ck)
        acc = lax.fori_loop(0, nck, body, jnp.zeros((SUBLANES, LANES), F32))
        acc = acc + tree(jnp.where(pred(keyn_ref[...], npos), 1.0, 0.0), ts)
        return jnp.sum(acc, axis=0, keepdims=True)

    def bisect(i, tu):
        cand_u = tu | (jnp.int32(1) << (31 - i))
        cand_s = cand_u ^ INT_MIN
        cnt = count(lambda key, idx: key >= cand_s)
        return jnp.where(cnt >= k_sel, cand_u, tu)

    tu = lax.fori_loop(0, 32, bisect, jnp.zeros((1, LANES), I32))
    ts_ = tu ^ INT_MIN
    c_gt = count(lambda key, idx: key > ts_)
    c_ge = count(lambda key, idx: key >= ts_)
    want = k_sel - c_gt
    m_ref[...] = jnp.full((1, LANES), 2 ** 30, I32)

    @pl.when(jnp.max(c_ge) > k_sel)
    def _():
        def tie_bisect(i, m):
            cand = m | (jnp.int32(1) << (tie_bits - 1 - i))
            cnt = count(lambda key, idx: (key == ts_) & (idx < cand))
            return jnp.where(cnt < want, cand, m)
        m_ref[...] = lax.fori_loop(0, tie_bits, tie_bisect, jnp.zeros((1, LANES), I32))

    m_last = m_ref[...]
    ts_adm = jnp.maximum(ts_, INT_MIN + 1)
    eye = jnp.where(lax.broadcasted_iota(I32, (LANES, LANES), 0) == lax.broadcasted_iota(I32, (LANES, LANES), 1),
                    1.0, 0.0).astype(BF16)

    def selected(key, idx):
        sel = (key > ts_adm) | ((key == ts_adm) & (idx <= m_last))
        return jnp.where(sel, 1.0, 0.0).astype(BF16)

    for c in range(nck):
        sel_q = lax.dot_general(eye, selected(key_ref[c], c * ck + krow), nt, preferred_element_type=F32)
        mask_ref[:, c * ck:(c + 1) * ck] = sel_q.astype(BF16)
    sel_n = lax.dot_general(eye, selected(keyn_ref[...], npos), nt, preferred_element_type=F32)
    mask_ref[:, past:past + LANES] = jnp.concatenate(
        [sel_n, jnp.zeros((LANES, LANES - ts), F32)], axis=1).astype(BF16)


def _sample_select(cache_ik, p_f, wi_t, np_, nbs, ts, past):
    gs = LANES // ts
    ck = 2 * LANES
    assert LANES % ts == 0 and nbs % gs == 0 and past % ck == 0 and np_ % LANES == 0 and ts % SUBLANES == 0
    l_true = past + ts
    k_sel = min(TOPK_MAX, l_true // 4)
    kern = functools.partial(_sample_select_kernel, ts=ts, gs=gs, past=past, ck=ck, k_sel=k_sel,
                             tie_bits=max(1, (l_true - 1).bit_length()))
    rb = np_ // LANES
    return pl.pallas_call(
        kern,
        grid=(nbs // gs,),
        in_specs=[pl.BlockSpec((gs, past, IDX_DIM), lambda g: (g, 0, 0)),
                  pl.BlockSpec((LANES, LANES), lambda g: (rb + g, KI_OFF // LANES)),
                  pl.BlockSpec((LANES, IDX_HEADS * IDX_DIM), lambda g: (rb + g, QI_OFF // (IDX_HEADS * IDX_DIM))),
                  pl.BlockSpec((IDX_HEADS, LANES), lambda g: (0, rb + g))],
        out_specs=pl.BlockSpec((LANES, past + LANES), lambda g: (g, 0)),
        out_shape=jax.ShapeDtypeStruct((nbs * ts, past + LANES), BF16),
        scratch_shapes=[pltpu.VMEM((past // ck, ck, LANES), I32),
                        pltpu.VMEM((ts, LANES), I32),
                        pltpu.VMEM((1, LANES), I32)],
        compiler_params=_cp("arbitrary"),
        name="sample_select",
    )(cache_ik, p_f, p_f, wi_t)


def _sample_attend_kernel(q_ref, kc_ref, vc_ref, kn_ref, vn_ref, mask_ref, o_ref, lg_ref, *, ts, past, ca):
    nt = (((1,), (1,)), ((), ()))
    qscale = (ATT_HD ** -0.5) * 1.4426950408889634
    nca = past // ca
    rows = KV_GROUP * ts

    def bias_of(m):
        b = (m.astype(F32) - 1.0) * (-NEG_BIG)
        return jnp.concatenate([b] * KV_GROUP, axis=0)

    bias_n = bias_of(mask_ref[:, past:past + ts])
    for g in range(KV_HEADS):
        qs = jnp.concatenate(
            [q_ref[:, (g * KV_GROUP + r) * ATT_HD:(g * KV_GROUP + r + 1) * ATT_HD] for r in range(KV_GROUP)], axis=0)
        qs = (qs.astype(F32) * qscale).astype(BF16)
        cols = slice(g * ATT_HD, (g + 1) * ATT_HD)
        lg_n = lax.dot_general(qs, kn_ref[:, cols].astype(BF16), nt, preferred_element_type=F32) + bias_n
        m_run = jnp.full((rows, LANES), NEG_BIG, F32)
        for c in range(nca):
            kc = kc_ref[c * ca:(c + 1) * ca, cols].astype(BF16)
            lg = lax.dot_general(qs, kc, nt, preferred_element_type=F32) + bias_of(mask_ref[:, c * ca:(c + 1) * ca])
            lg_ref[g, :, c * ca:(c + 1) * ca] = lg
            for u in range(ca // LANES):
                m_run = jnp.maximum(m_run, lg[:, u * LANES:(u + 1) * LANES])
        m_row = jnp.maximum(jnp.max(m_run, axis=-1, keepdims=True), jnp.max(lg_n, axis=-1, keepdims=True))
        ones_n = jnp.ones((ts, ATT_HD), BF16)
        acc = jnp.dot(jnp.exp2(lg_n - m_row).astype(BF16),
                      jnp.concatenate([vn_ref[:, cols].astype(BF16), ones_n], axis=1), preferred_element_type=F32)
        ones_c = jnp.ones((ca, ATT_HD), BF16)
        for c in range(nca):
            p = jnp.exp2(lg_ref[g, :, c * ca:(c + 1) * ca] - m_row).astype(BF16)
            vc = vc_ref[c * ca:(c + 1) * ca, cols].astype(BF16)
            acc = acc + jnp.dot(p, jnp.concatenate([vc, ones_c], axis=1), preferred_element_type=F32)
        o = acc[:, :ATT_HD] / acc[:, ATT_HD:]
        for r in range(KV_GROUP):
            h = g * KV_GROUP + r
            o_ref[:, h * ATT_HD:(h + 1) * ATT_HD] = o[r * ts:(r + 1) * ts].astype(o_ref.dtype)


def _sample_attend(p_bf, qa_blk, cache_k, cache_v, p_f, mask, np_, nbs, ts, past):
    kvw = KV_HEADS * ATT_HD
    hq = ATT_HEADS * ATT_HD
    ca = 4 * LANES
    assert past % ca == 0
    rb = np_ // ts
    kern = functools.partial(_sample_attend_kernel, ts=ts, past=past, ca=ca)
    return pl.pallas_call(
        kern,
        grid=(nbs,),
        in_specs=[pl.BlockSpec((ts, hq), lambda b: (rb + b, qa_blk)),
                  pl.BlockSpec((None, past, kvw), lambda b: (b, 0, 0)),
                  pl.BlockSpec((None, past, kvw), lambda b: (b, 0, 0)),
                  pl.BlockSpec((ts, kvw), lambda b: (rb + b, KA_OFF // kvw)),
                  pl.BlockSpec((ts, kvw), lambda b: (rb + b, VA_OFF // kvw)),
                  pl.BlockSpec((ts, past + LANES), lambda b: (b, 0))],
        out_specs=pl.BlockSpec((ts, hq), lambda b: (b, 0)),
        out_shape=jax.ShapeDtypeStruct((nbs * ts, hq), BF16),
        scratch_shapes=[pltpu.VMEM((KV_HEADS, KV_GROUP * ts, past), F32)],
        compiler_params=_cp("arbitrary"),
        name="sample_attend",
    )(p_bf, cache_k, cache_v, p_f, p_f, mask)


def _ret_kernel(q_ref, k_ref, v_ref, g_ref, cs_ref, sn_ref, dec_ref, xi_ref, zt_ref, gp_ref, gn_ref, s0_ref,
                y_ref, sf_ref, s_ref):
    c = pl.program_id(1)
    dk, dv = RET_DK, RET_DV

    @pl.when(c == 0)
    def _():
        s_ref[...] = s0_ref[...]

    cs = cs_ref[...]
    sn = sn_ref[...]

    def rot(x):
        xf = x.astype(F32)
        return xf * cs + pltpu.roll(xf, dk // 2, 1) * sn

    for h in range(RET_HEADS):
        q = rot(q_ref[:, h * dk:(h + 1) * dk])
        k = rot(k_ref[:, h * dk:(h + 1) * dk]) * (dk ** -0.5)
        qb = q.astype(BF16)
        kb = k.astype(BF16)
        v = v_ref[:, h * dv:(h + 1) * dv]
        inner = lax.dot_general(qb, kb, (((1,), (1,)), ((), ())), preferred_element_type=F32) * dec_ref[h]
        s = s_ref[h]
        o = jnp.dot(inner.astype(BF16), v, preferred_element_type=F32)
        o = o + jnp.dot(qb, s.astype(BF16), preferred_element_type=F32) * xi_ref[h]
        kz = (k * zt_ref[h]).T.astype(BF16)
        s_new = s * gp_ref[h] + jnp.dot(kz, v, preferred_element_type=F32)
        s_ref[h] = s_new
        sf_ref[h] = s_new

        mu = jnp.mean(o, axis=-1, keepdims=True)
        oc = o - mu
        var = jnp.mean(oc * oc, axis=-1, keepdims=True)
        yn = oc * lax.rsqrt(var + LN_EPS) * gn_ref[:, h * dv:(h + 1) * dv]
        gg = g_ref[:, h * dv:(h + 1) * dv].astype(F32)
        y_ref[:, h * dv:(h + 1) * dv] = (gg * jax.nn.sigmoid(gg) * yn).astype(y_ref.dtype)


def _retention(p_bf, row0, nb, t, cl, offset, s0, gn_g):
    nc = t // cl
    h_, dk, dv = RET_HEADS, RET_DK, RET_DV
    half = dk // 2
    pos = (offset + jnp.arange(t, dtype=I32)).astype(F32)
    inv = 1.0 / (ROPE_BASE ** (jnp.arange(half, dtype=F32) / half))
    ang = pos[:, None] * inv[None, :]
    cos, sin = jnp.cos(ang), jnp.sin(ang)
    cs2 = jnp.concatenate([cos, cos], axis=-1)
    sn2 = jnp.concatenate([-sin, sin], axis=-1)
    log_g = jnp.log1p(-jnp.exp2(-5.0 - jnp.arange(h_, dtype=F32)))
    n = jnp.arange(cl, dtype=F32)
    diff = n[:, None] - n[None, :]
    decay = jnp.where(diff >= 0, jnp.exp(log_g[:, None, None] * jnp.maximum(diff, 0.0)), 0.0)
    xi = jnp.exp(log_g[:, None] * (n + 1.0))[..., None]
    zeta = jnp.exp(log_g[:, None] * (cl - 1.0 - n))[..., None]
    gpow = jnp.broadcast_to(jnp.exp(log_g * cl)[:, None, None], (h_, 1, dv))
    rb = row0 // cl
    return pl.pallas_call(
        _ret_kernel,
        grid=(nb, nc),
        in_specs=[pl.BlockSpec((cl, h_ * dk), lambda b, c: (rb + b * nc + c, QR_OFF // (h_ * dk))),
                  pl.BlockSpec((cl, h_ * dk), lambda b, c: (rb + b * nc + c, KR_OFF // (h_ * dk))),
                  pl.BlockSpec((cl, h_ * dv), lambda b, c: (rb + b * nc + c, VR_OFF // (h_ * dv))),
                  pl.BlockSpec((cl, h_ * dv), lambda b, c: (rb + b * nc + c, GR_OFF // (h_ * dv))),
                  pl.BlockSpec((cl, dk), lambda b, c: (c, 0)),
                  pl.BlockSpec((cl, dk), lambda b, c: (c, 0)),
                  pl.BlockSpec((h_, cl, cl), lambda b, c: (0, 0, 0)),
                  pl.BlockSpec((h_, cl, 1), lambda b, c: (0, 0, 0)),
                  pl.BlockSpec((h_, cl, 1), lambda b, c: (0, 0, 0)),
                  pl.BlockSpec((h_, 1, dv), lambda b, c: (0, 0, 0)),
                  pl.BlockSpec((1, h_ * dv), lambda b, c: (0, 0)),
                  pl.BlockSpec((None, h_, dk, dv), lambda b, c: (b, 0, 0, 0))],
        out_specs=[pl.BlockSpec((cl, h_ * dv), lambda b, c: (b * nc + c, 0)),
                   pl.BlockSpec((None, h_, dk, dv), lambda b, c: (b, 0, 0, 0))],
        out_shape=[jax.ShapeDtypeStruct((nb * t, h_ * dv), BF16),
                   jax.ShapeDtypeStruct((nb, h_, dk, dv), F32)],
        scratch_shapes=[pltpu.VMEM((h_, dk, dv), F32)],
        compiler_params=_cp("arbitrary", "arbitrary"),
        name="retention",
    )(p_bf, p_bf, p_bf, p_bf, cs2, sn2, decay, xi, zeta, gpow, gn_g.reshape(1, -1), s0)


CONV_HALO = 32
CONV_ROWS = 64
CONV_BLOCK = 64


def _conv_kernel(ca_ref, cb_ref, st_ref, w_ref, b_ref, lg_ref, lb_ref, y_ref, tail_ref, sh_ref, h_ref, *, tt):
    t = pl.program_id(1)
    ext = tt + CONV_HALO

    @pl.when(t == 0)
    def _():
        sh_ref[0, 0:CONV_HALO] = st_ref[...]

    @pl.when(t > 0)
    def _():
        sh_ref[0, 0:CONV_HALO] = sh_ref[0, tt:ext]

    ca = ca_ref[...].astype(F32)
    cb = cb_ref[...].astype(F32)
    sh_ref[0, CONV_HALO:ext] = ca * jax.nn.sigmoid(cb)
    tail_ref[...] = sh_ref[0, tt:ext]
    for s in range(1, SUBLANES):
        sh_ref[s, 0:ext - SUBLANES] = sh_ref[0, s:ext - SUBLANES + s]

    lead = CONV_HALO - (CONV_W - 1)
    blk = min(CONV_BLOCK, tt)
    nblk = blk // SUBLANES

    for lc in range(CONV_CH // LANES):
        cols = slice(lc * LANES, (lc + 1) * LANES)
        taps = [jnp.broadcast_to(w_ref[k:k + 1, cols], (SUBLANES, LANES)) for k in range(CONV_W)]
        bias = jnp.broadcast_to(b_ref[:, cols], (SUBLANES, LANES))

        def conv_body(i, carry):
            r0 = pl.multiple_of(i * blk, blk)
            accs = [bias] * nblk
            for k in range(CONV_W):
                a, s = divmod(k + lead, SUBLANES)
                u = sh_ref[s, pl.ds(r0 + a * SUBLANES, blk), cols]
                accs = [accs[j] + taps[k] * u[j * SUBLANES:(j + 1) * SUBLANES] for j in range(nblk)]
            h_ref[pl.ds(r0, blk), cols] = jnp.concatenate(accs, axis=0)
            return carry

        lax.fori_loop(0, tt // blk, conv_body, 0)

    rows = min(CONV_ROWS, tt)

    def norm_body(i, carry):
        r0 = pl.multiple_of(i * rows, rows)
        hn = _layer_norm(h_ref[pl.ds(r0, rows), :], lg_ref[...], lb_ref[...])
        y_ref[pl.ds(r0, rows), :] = (hn * jax.nn.sigmoid(hn)).astype(y_ref.dtype)
        return carry

    lax.fori_loop(0, tt // rows, norm_body, 0)


def _conv_module(p_bf, row0, nb, t, tt, state, dw, dw_b, ln_g, ln_b):
    nt = t // tt
    lead = CONV_HALO - (CONV_W - 1)
    st = jnp.pad(state, ((0, 0), (lead, 0), (0, 0)))
    rb = row0 // tt
    kern = functools.partial(_conv_kernel, tt=tt)
    y, tail = pl.pallas_call(
        kern,
        grid=(nb, nt),
        in_specs=[pl.BlockSpec((tt, CONV_CH), lambda b, i: (rb + b * nt + i, CIN_OFF // CONV_CH)),
                  pl.BlockSpec((tt, CONV_CH), lambda b, i: (rb + b * nt + i, CIN_OFF // CONV_CH + 1)),
                  pl.BlockSpec((None, CONV_HALO, CONV_CH), lambda b, i: (b, 0, 0)),
                  pl.BlockSpec((CONV_W, CONV_CH), lambda b, i: (0, 0)),
                  pl.BlockSpec((1, CONV_CH), lambda b, i: (0, 0)),
                  pl.BlockSpec((1, CONV_CH), lambda b, i: (0, 0)),
                  pl.BlockSpec((1, CONV_CH), lambda b, i: (0, 0))],
        out_specs=[pl.BlockSpec((tt, CONV_CH), lambda b, i: (b * nt + i, 0)),
                   pl.BlockSpec((None, CONV_HALO, CONV_CH), lambda b, i: (b, 0, 0))],
        out_shape=[jax.ShapeDtypeStruct((nb * t, CONV_CH), BF16),
                   jax.ShapeDtypeStruct((nb, CONV_HALO, CONV_CH), F32)],
        scratch_shapes=[pltpu.VMEM((SUBLANES, tt + CONV_HALO, CONV_CH), F32),
                        pltpu.VMEM((tt, CONV_CH), F32)],
        compiler_params=_cp("arbitrary", "arbitrary"),
        name="conv_module",
    )(p_bf, p_bf, st, dw, dw_b.reshape(1, -1), ln_g.reshape(1, -1), ln_b.reshape(1, -1))
    return y, tail[:, lead:, :]


def _merge_kernel(ap_ref, rp_ref, cp_ref, as_ref, rs_ref, cs_ref, ga_ref, gb_ref, gc_ref, ba_ref, bb_ref, bc_ref,
                  wa_ref, wr_ref, wc_ref, z_ref, *, prompt_tiles):
    ga = jax.nn.sigmoid(ga_ref[...].astype(F32) + ba_ref[...])
    gb = jax.nn.sigmoid(gb_ref[...].astype(F32) + bb_ref[...])
    gc = jax.nn.sigmoid(gc_ref[...].astype(F32) + bc_ref[...])

    is_prompt = pl.program_id(1) < prompt_tiles
    ya = jnp.dot(jnp.where(is_prompt, ap_ref[...], as_ref[...]), wa_ref[...], preferred_element_type=F32)
    yb = jnp.dot(jnp.where(is_prompt, rp_ref[...], rs_ref[...]), wr_ref[...], preferred_element_type=F32)
    yc = jnp.dot(jnp.where(is_prompt, cp_ref[...], cs_ref[...]), wc_ref[...], preferred_element_type=F32)
    z_ref[...] = (ga * ya + gb * yb + gc * yc).astype(z_ref.dtype)


def _merge(att_p, ret_p, cnv_p, att_s, ret_s, cnv_s, p_bf, b_gate, wa, wr, wc, tm, tn):
    np_, ns = att_p.shape[0], att_s.shape[0]
    assert np_ % tm == 0 and ns % tm == 0
    m = np_ + ns
    pt = np_ // tm
    d = wa.shape[1]
    nj = d // tn
    g0 = GATE_OFF // tn
    bg = b_gate.reshape(1, -1)

    def prompt_rows(j, i):
        return (jnp.minimum(i, pt - 1), 0)

    def sample_rows(j, i):
        return (jnp.maximum(i - pt, 0), 0)

    return pl.pallas_call(
        functools.partial(_merge_kernel, prompt_tiles=pt),
        grid=(nj, m // tm),
        in_specs=[pl.BlockSpec((tm, att_p.shape[1]), prompt_rows),
                  pl.BlockSpec((tm, ret_p.shape[1]), prompt_rows),
                  pl.BlockSpec((tm, cnv_p.shape[1]), prompt_rows),
                  pl.BlockSpec((tm, att_s.shape[1]), sample_rows),
                  pl.BlockSpec((tm, ret_s.shape[1]), sample_rows),
                  pl.BlockSpec((tm, cnv_s.shape[1]), sample_rows),
                  pl.BlockSpec((tm, tn), lambda j, i: (i, g0 + j)),
                  pl.BlockSpec((tm, tn), lambda j, i: (i, g0 + nj + j)),
                  pl.BlockSpec((tm, tn), lambda j, i: (i, g0 + 2 * nj + j)),
                  pl.BlockSpec((1, tn), lambda j, i: (0, j)),
                  pl.BlockSpec((1, tn), lambda j, i: (0, nj + j)),
                  pl.BlockSpec((1, tn), lambda j, i: (0, 2 * nj + j)),
                  pl.BlockSpec((wa.shape[0], tn), lambda j, i: (0, j)),
                  pl.BlockSpec((wr.shape[0], tn), lambda j, i: (0, j)),
                  pl.BlockSpec((wc.shape[0], tn), lambda j, i: (0, j))],
        out_specs=pl.BlockSpec((tm, tn), lambda j, i: (i, j)),
        out_shape=jax.ShapeDtypeStruct((m, d), BF16),
        compiler_params=_cp("arbitrary", "arbitrary"),
        name="branch_merge",
    )(att_p, ret_p, cnv_p, att_s, ret_s, cnv_s, p_bf, p_bf, p_bf, bg, bg, bg, wa, wr, wc)


def _route(x, w_hl, b):
    tm = x.shape[0]
    x_hi = x.astype(BF16)
    x_lo = (x - x_hi.astype(F32)).astype(BF16)
    d_hi = jnp.dot(x_hi, w_hl, preferred_element_type=F32)
    d_lo = jnp.dot(x_lo, w_hl[:, :LANES], preferred_element_type=F32)
    logits = d_hi[:, :LANES] + d_hi[:, LANES:] + d_lo + b
    lane = lax.broadcasted_iota(I32, (tm, LANES), 1)
    valid = lane < N_EXPERTS
    lm = jnp.where(valid, logits, NEG_BIG)
    e = jnp.where(valid, jnp.exp(lm - jnp.max(lm, axis=-1, keepdims=True)), 0.0)
    aff = e / jnp.sum(e, axis=-1, keepdims=True)

    def top2(vals):
        v1 = jnp.max(vals, axis=-1, keepdims=True)
        i1 = jnp.min(jnp.where(vals == v1, lane, LANES), axis=-1, keepdims=True)
        rest = jnp.where(lane == i1, -2.0, vals)
        v2 = jnp.max(rest, axis=-1, keepdims=True)
        i2 = jnp.min(jnp.where(rest == v2, lane, LANES), axis=-1, keepdims=True)
        return v1, i1, v2, i2

    grp = lane // EXP_PER_GROUP
    best = jnp.zeros((tm, 1), I32)
    best_score = None
    for g in range(N_GROUPS):
        v1, _, v2, _ = top2(jnp.where(grp == g, aff, -1.0))
        score = v1 + v2
        if g == 0:
            best_score = score
        else:
            better = score > best_score
            best = jnp.where(better, g, best)
            best_score = jnp.where(better, score, best_score)
    v1, i1, v2, i2 = top2(jnp.where(grp == best, aff, -1.0))
    tot = v1 + v2
    return (jnp.where(lane == 0, i1.astype(F32), 0.0) + jnp.where(lane == 1, i2.astype(F32), 0.0)
            + jnp.where(lane == 2, v1 / tot, 0.0) + jnp.where(lane == 3, v2 / tot, 0.0))


def _outproj_ln_kernel(z_ref, w_ref, x_ref, g_ref, b_ref, wr_ref, br_ref, of_ref, ob_ref, rt_ref, *, alpha):
    mix = jnp.dot(z_ref[...], w_ref[...], preferred_element_type=F32)
    y = _layer_norm(alpha * x_ref[...] + mix, g_ref[...], b_ref[...])
    of_ref[...] = y
    ob_ref[...] = y.astype(BF16)
    rt_ref[...] = _route(y, wr_ref[...], br_ref[...])


def _outproj_ln(z, w_out, x, g, b, w_router, b_router, alpha, tm):
    m, d = x.shape
    wr = jnp.pad(w_router, ((0, 0), (0, LANES - N_EXPERTS)))
    wr_hi = wr.astype(BF16)
    wr = jnp.concatenate([wr_hi, (wr - wr_hi.astype(F32)).astype(BF16)], axis=1)
    br = jnp.pad(b_router, (0, LANES - N_EXPERTS)).reshape(1, -1)
    kern = functools.partial(_outproj_ln_kernel, alpha=alpha)
    return pl.pallas_call(
        kern,
        grid=(m // tm,),
        in_specs=[pl.BlockSpec((tm, d), lambda i: (i, 0)),
                  pl.BlockSpec((d, d), lambda i: (0, 0)),
                  pl.BlockSpec((tm, d), lambda i: (i, 0)),
                  pl.BlockSpec((1, d), lambda i: (0, 0)),
                  pl.BlockSpec((1, d), lambda i: (0, 0)),
                  pl.BlockSpec((d, 2 * LANES), lambda i: (0, 0)),
                  pl.BlockSpec((1, LANES), lambda i: (0, 0))],
        out_specs=[pl.BlockSpec((tm, d), lambda i: (i, 0)), pl.BlockSpec((tm, d), lambda i: (i, 0)),
                   pl.BlockSpec((tm, LANES), lambda i: (i, 0))],
        out_shape=[jax.ShapeDtypeStruct((m, d), F32), jax.ShapeDtypeStruct((m, d), BF16),
                   jax.ShapeDtypeStruct((m, LANES), F32)],
        compiler_params=_cp("arbitrary"),
        name="out_proj_ln1_route",
    )(z, w_out, x, g.reshape(1, -1), b.reshape(1, -1), wr, br)


MOE_TILE = 256
MOE_ISSUE_GROUPS = 8


def _gather_rows(idx_ref, src_hbm, dst_ref, sem, lo, hi):
    for r in range(lo, hi):
        pltpu.make_async_copy(src_hbm.at[pl.ds(idx_ref[0, 0, r], 1)], dst_ref.at[pl.ds(r, 1)], sem).start()


def _wait_rows(src_hbm, dst_ref, sem, n):
    pltpu.make_async_copy(src_hbm.at[pl.ds(0, n)], dst_ref, sem).wait()


def _moe_expert_kernel(te_ref, cur_ref, nxt_ref, x_hbm, w1_ref, w3_ref, w2_ref, o_ref,
                       xg_ref, sem, w1b_ref, w3b_ref, w2b_ref):
    t = pl.program_id(0)
    nt = pl.num_programs(0)
    tile = xg_ref.shape[1]
    slot = t % 2

    @pl.when(t == 0)
    def _():
        _gather_rows(cur_ref, x_hbm, xg_ref.at[0], sem.at[0], 0, tile)

    @pl.when((t == 0) | (te_ref[t] != te_ref[jnp.maximum(t - 1, 0)]))
    def _():
        w1b_ref[...] = w1_ref[...].astype(BF16)
        w3b_ref[...] = w3_ref[...].astype(BF16)
        w2b_ref[...] = w2_ref[...].astype(BF16)

    _wait_rows(x_hbm, xg_ref.at[slot], sem.at[slot], tile)
    d = xg_ref.shape[2]
    kc, rc = d // MOE_ISSUE_GROUPS, tile // MOE_ISSUE_GROUPS
    h1 = h3 = None
    for c in range(MOE_ISSUE_GROUPS):
        xk = xg_ref[slot, :, c * kc:(c + 1) * kc].astype(BF16)
        p1 = jnp.dot(xk, w1b_ref[c * kc:(c + 1) * kc, :], preferred_element_type=F32)
        p3 = jnp.dot(xk, w3b_ref[c * kc:(c + 1) * kc, :], preferred_element_type=F32)
        h1 = p1 if h1 is None else h1 + p1
        h3 = p3 if h3 is None else h3 + p3
        _gather_rows(nxt_ref, x_hbm, xg_ref.at[1 - slot], sem.at[1 - slot], c * rc, (c + 1) * rc)
    h = h1 * jax.nn.sigmoid(h1) * h3
    o_ref[...] = jnp.dot(h.astype(BF16), w2b_ref[...], preferred_element_type=F32)

    @pl.when(t == nt - 1)
    def _():
        _wait_rows(x_hbm, xg_ref.at[1 - slot], sem.at[1 - slot], tile)


def _moe_experts(x, w1, w3, w2, tile_expert, src_tiles):
    n, d = x.shape
    _, _, de = w1.shape
    nt = src_tiles.shape[0] - 1
    tile = src_tiles.shape[2]
    grid_spec = pltpu.PrefetchScalarGridSpec(
        num_scalar_prefetch=1,
        grid=(nt,),
        in_specs=[pl.BlockSpec((1, 1, tile), lambda t, te: (t, 0, 0), memory_space=pltpu.SMEM),
                  pl.BlockSpec((1, 1, tile), lambda t, te: (t + 1, 0, 0), memory_space=pltpu.SMEM),
                  pl.BlockSpec(memory_space=pl.ANY),
                  pl.BlockSpec((None, d, de), lambda t, te: (te[t], 0, 0)),
                  pl.BlockSpec((None, d, de), lambda t, te: (te[t], 0, 0)),
                  pl.BlockSpec((None, de, d), lambda t, te: (te[t], 0, 0))],
        out_specs=pl.BlockSpec((tile, d), lambda t, te: (t, 0)),
        scratch_shapes=[pltpu.VMEM((2, tile, d), F32),
                        pltpu.SemaphoreType.DMA((2,)),
                        pltpu.VMEM((d, de), BF16),
                        pltpu.VMEM((d, de), BF16),
                        pltpu.VMEM((de, d), BF16)],
    )
    return pl.pallas_call(
        _moe_expert_kernel,
        grid_spec=grid_spec,
        out_shape=jax.ShapeDtypeStruct((nt * tile, d), F32),
        compiler_params=_cp("arbitrary"),
        name="moe_experts",
    )(tile_expert, src_tiles, src_tiles, x, w1, w3, w2)


def _moe_combine_kernel(cur_ref, nxt_ref, y_hbm, rt_ref, x_ref, g_ref, b_ref, of_ref, ob_ref, yg_ref, sem, *, alpha):
    t = pl.program_id(0)
    nt = pl.num_programs(0)
    tm = x_ref.shape[0]
    slot = t % 2

    @pl.when(t == 0)
    def _():
        _gather_rows(cur_ref, y_hbm, yg_ref.at[0], sem.at[0], 0, 2 * tm)

    _gather_rows(nxt_ref, y_hbm, yg_ref.at[1 - slot], sem.at[1 - slot], 0, 2 * tm)
    _wait_rows(y_hbm, yg_ref.at[slot], sem.at[slot], 2 * tm)
    rt = rt_ref[...]
    moe = rt[:, 2:3] * yg_ref[slot, 0:tm] + rt[:, 3:4] * yg_ref[slot, tm:2 * tm]
    y = _layer_norm(alpha * x_ref[...] + moe, g_ref[...], b_ref[...])
    of_ref[...] = y
    ob_ref[...] = y.astype(BF16)

    @pl.when(t == nt - 1)
    def _():
        _wait_rows(y_hbm, yg_ref.at[1 - slot], sem.at[1 - slot], 2 * tm)


def _moe_combine_ln(y_sorted, pos_tiles, route, x, g, b, alpha, tm):
    m, d = x.shape
    kern = functools.partial(_moe_combine_kernel, alpha=alpha)
    return pl.pallas_call(
        kern,
        grid=(m // tm,),
        in_specs=[pl.BlockSpec((1, 1, 2 * tm), lambda t: (t, 0, 0), memory_space=pltpu.SMEM),
                  pl.BlockSpec((1, 1, 2 * tm), lambda t: (t + 1, 0, 0), memory_space=pltpu.SMEM),
                  pl.BlockSpec(memory_space=pl.ANY),
                  pl.BlockSpec((tm, LANES), lambda t: (t, 0)),
                  pl.BlockSpec((tm, d), lambda t: (t, 0)),
                  pl.BlockSpec((1, d), lambda t: (0, 0)),
                  pl.BlockSpec((1, d), lambda t: (0, 0))],
        out_specs=[pl.BlockSpec((tm, d), lambda t: (t, 0)), pl.BlockSpec((tm, d), lambda t: (t, 0))],
        out_shape=[jax.ShapeDtypeStruct((m, d), F32), jax.ShapeDtypeStruct((m, d), BF16)],
        scratch_shapes=[pltpu.VMEM((2, 2 * tm, d), F32), pltpu.SemaphoreType.DMA((2,))],
        compiler_params=_cp("arbitrary"),
        name="moe_combine_ln2",
    )(pos_tiles, pos_tiles, y_sorted, route, x, g.reshape(1, -1), b.reshape(1, -1))


def _moe_layout(route, tile, tm):
    n = route.shape[0]
    e_flat = route[:, 0:2].astype(I32).reshape(-1)
    npair = 2 * n
    nt = -(-npair // tile) + N_EXPERTS
    onehot = (e_flat[:, None] == jnp.arange(N_EXPERTS, dtype=I32)[None, :]).astype(I32)
    csum = jnp.cumsum(onehot, axis=0)
    counts = csum[-1]
    rank = jnp.sum((csum - onehot) * onehot, axis=1)
    padded = (counts + tile - 1) // tile * tile
    ends = jnp.cumsum(padded)
    starts = ends - padded
    pos = starts[e_flat] + rank
    rows = nt * tile
    src = jnp.zeros((rows,), I32).at[pos].set(jnp.arange(npair, dtype=I32) // 2, unique_indices=True)
    tile_expert = jnp.minimum(jnp.searchsorted(ends, jnp.arange(nt, dtype=I32) * tile, side='right'),
                              N_EXPERTS - 1).astype(I32)
    src_tiles = jnp.concatenate([src, jnp.zeros((tile,), I32)]).reshape(nt + 1, 1, tile)
    pos_t = pos.reshape(n // tm, tm, 2).transpose(0, 2, 1).reshape(n // tm, 1, 2 * tm)
    pos_tiles = jnp.concatenate([pos_t, jnp.zeros((1, 1, 2 * tm), I32)], axis=0)
    return tile_expert, src_tiles, pos_tiles


def kernel(x_prompt, x_sample, cache_k, cache_v, cache_idx_k, state_ret, state_conv, w_in, b_gate, w_att_o, ret_gn_g,
           w_ret_o, conv_dw, conv_dw_b, conv_ln_g, conv_ln_b, w_conv_o, w_out, ln1_g, ln1_b, w_router, b_router,
           moe_w1, moe_w3, moe_w2, ln2_g, ln2_b):
    nbp, tp, d = x_prompt.shape
    nbs, ts, _ = x_sample.shape
    depth = w_in.shape[0]
    past = cache_k.shape[2]
    np_, ns = nbp * tp, nbs * ts
    n = np_ + ns
    alpha = (2 * depth) ** 0.25
    kvw = KV_HEADS * ATT_HD

    tm = _pow2_tile(n, 512)
    tq_p = _pow2_tile(tp, 256)
    cl_p = _pow2_tile(tp, 256)
    tt_p = _pow2_tile(tp, 256)
    ls_true = past + ts
    ls_pad = -(-ls_true // (2 * LANES)) * (2 * LANES)
    assert tp % (2 * LANES) == 0 and np_ % max(tq_p, cl_p, tt_p, ts) == 0

    x = jnp.concatenate([x_prompt.reshape(np_, d), x_sample.reshape(ns, d)], axis=0)
    xb = x.astype(BF16)

    c1 = ATT_HEADS * ATT_HD
    c2 = c1 + F32_USED
    rps, cps, kss, vss, ikss, rss, css = ([] for _ in range(7))
    ks = jnp.zeros((depth, nbp, tp, KV_HEADS, ATT_HD), F32)
    vs = jnp.zeros((depth, nbp, tp, KV_HEADS, ATT_HD), F32)
    iks = jnp.zeros((depth, nbp, tp, IDX_DIM), F32)
    zero_ret = jnp.zeros((nbp, RET_HEADS, RET_DK, RET_DV), F32)
    zero_conv = jnp.zeros((nbp, CONV_W - 1, CONV_CH), F32)

    for l in range(depth):
        w_l = w_in[l]
        w_bf = jnp.concatenate([w_l[:, c2:], w_l[:, :c1]], axis=1).astype(BF16)
        w_f = jnp.pad(w_l[:, c1:c2], ((0, 0), (0, F32_WIDTH - F32_USED))).astype(BF16)
        p_bf = _matmul(xb, w_bf, BF16, tm, 1536)
        p_f = _matmul(xb, w_f, F32, tm, F32_WIDTH)

        k_new = p_f[:, KA_OFF:KA_OFF + kvw]
        v_new = p_f[:, VA_OFF:VA_OFF + kvw]
        ik_new = p_f[:, KI_OFF:KI_OFF + IDX_DIM]
        wi_t = p_f[:, WI_OFF:WI_OFF + IDX_HEADS].T
        ks = ks.at[l].set(k_new[:np_].reshape(nbp, tp, KV_HEADS, ATT_HD))
        vs = vs.at[l].set(v_new[:np_].reshape(nbp, tp, KV_HEADS, ATT_HD))
        iks = iks.at[l].set(ik_new[:np_].reshape(nbp, tp, IDX_DIM))
        kss.append(k_new[np_:].reshape(nbs, ts, KV_HEADS, ATT_HD))
        vss.append(v_new[np_:].reshape(nbs, ts, KV_HEADS, ATT_HD))
        ikss.append(ik_new[np_:].reshape(nbs, ts, IDX_DIM))

        nq = tp // tq_p
        hq = ATT_HEADS * ATT_HD
        qa_blk = (GATE_OFF + 3 * d) // hq
        att_p = _attention(
            p_bf, lambda tq: ((tq, hq), lambda b, j: (b * nq + j, qa_blk)),
            p_f, lambda lp: ((lp, kvw), lambda b, j: (b, KA_OFF // kvw)),
            p_f, lambda lp: ((lp, kvw), lambda b, j: (b, VA_OFF // kvw)),
            p_f, lambda tq: ((tq, IDX_HEADS * IDX_DIM), lambda b, j: (b * nq + j, QI_OFF // (IDX_HEADS * IDX_DIM))),
            p_f, lambda lp: ((lp, LANES), lambda b, j: (b, KI_OFF // LANES)),
            wi_t[:, :np_], lambda tq: ((IDX_HEADS, tq), lambda b, j: (0, b * nq + j)),
            nb=nbp, t=tp, l_pad=tp, l_true=tp, offset=0, tq=tq_p)

        mask_s = _sample_select(cache_idx_k[l], p_f, wi_t, np_, nbs, ts, past)
        att_s = _sample_attend(p_bf, qa_blk, cache_k[l].reshape(nbs, past, kvw), cache_v[l].reshape(nbs, past, kvw),
                               p_f, mask_s, np_, nbs, ts, past)

        ret_p, rs_p = _retention(p_bf, 0, nbp, tp, cl_p, 0, zero_ret, ret_gn_g[l])
        ret_s, rs_s = _retention(p_bf, np_, nbs, ts, ts, past, state_ret[l], ret_gn_g[l])
        rps.append(rs_p)
        rss.append(rs_s)

        cnv_p, cs_p = _conv_module(p_bf, 0, nbp, tp, tt_p, zero_conv, conv_dw[l], conv_dw_b[l], conv_ln_g[l],
                                   conv_ln_b[l])
        cnv_s, cs_s = _conv_module(p_bf, np_, nbs, ts, ts, state_conv[l], conv_dw[l], conv_dw_b[l], conv_ln_g[l],
                                   conv_ln_b[l])
        cps.append(cs_p)
        css.append(cs_s)

        z = _merge(att_p, ret_p, cnv_p, att_s, ret_s, cnv_s, p_bf, b_gate[l], w_att_o[l].astype(BF16),
                   w_ret_o[l].astype(BF16), w_conv_o[l].astype(BF16), math.gcd(tm, ns), 1024)
        tm_c = _pow2_tile(n, 256)
        x, xb, route = _outproj_ln(z, w_out[l].astype(BF16), x, ln1_g[l], ln1_b[l], w_router, b_router, alpha, tm_c)
        tile_expert, src_tiles, pos_tiles = _moe_layout(route, MOE_TILE, tm_c)
        y_sorted = _moe_experts(x, moe_w1[l], moe_w3[l], moe_w2[l], tile_expert, src_tiles)
        x, xb = _moe_combine_ln(y_sorted, pos_tiles, route, x, ln2_g[l], ln2_b[l], alpha, tm_c)

    y_prompt = x[:np_].reshape(nbp, tp, d)
    y_sample = x[np_:].reshape(nbs, ts, d)
    st = jnp.stack
    return (y_prompt, y_sample, ks, vs, iks, st(rps), st(cps), st(kss), st(vss), st(ikss), st(rss), st(css))
```

```python
import functools
import math

import jax
import jax.numpy as jnp
from jax import lax
from jax.experimental import pallas as pl
from jax.experimental.pallas import tpu as pltpu

F32 = jnp.float32
BF16 = jnp.bfloat16
I32 = jnp.int32

CHUNK = 64
TOPK_MAX = 256
ATT_HEADS, ATT_HD, KV_HEADS = 8, 128, 2
KV_GROUP = ATT_HEADS // KV_HEADS
IDX_HEADS, IDX_DIM = 8, 64
RET_HEADS, RET_DK, RET_DV = 8, 128, 256
ROPE_BASE = 10000.0
CONV_CH, CONV_W = 1024, 31
N_EXPERTS, N_GROUPS = 16, 4
EXP_PER_GROUP = N_EXPERTS // N_GROUPS
LN_EPS = 1e-5

LANES = 128
SUBLANES = 8

INT_MIN = -2 ** 31
NEG_BIG = -1e30

QR_OFF = 0
KR_OFF = QR_OFF + RET_HEADS * RET_DK
VR_OFF = KR_OFF + RET_HEADS * RET_DK
GR_OFF = VR_OFF + RET_HEADS * RET_DV
CIN_OFF = GR_OFF + RET_HEADS * RET_DV
GATE_OFF = CIN_OFF + 2 * CONV_CH
KA_OFF = 0
VA_OFF = KA_OFF + KV_HEADS * ATT_HD
QI_OFF = VA_OFF + KV_HEADS * ATT_HD
KI_OFF = QI_OFF + IDX_HEADS * IDX_DIM
WI_OFF = KI_OFF + IDX_DIM
F32_USED = WI_OFF + IDX_HEADS
F32_WIDTH = -(-F32_USED // LANES) * LANES

VMEM_LIMIT = 56 * 1024 * 1024


def _cp(*sem):
    return pltpu.CompilerParams(dimension_semantics=sem, vmem_limit_bytes=VMEM_LIMIT)


def _pow2_tile(n, pref):
    t = pref
    while n % t:
        t //= 2
    return t


def _layer_norm(x, g, b):
    mu = jnp.mean(x, axis=-1, keepdims=True)
    xc = x - mu
    var = jnp.mean(xc * xc, axis=-1, keepdims=True)
    return xc * lax.rsqrt(var + LN_EPS) * g + b


def _mm_kernel(x_ref, w_ref, o_ref):
    o_ref[...] = jnp.dot(x_ref[...], w_ref[...], preferred_element_type=F32).astype(o_ref.dtype)


def _matmul(x, w, out_dtype, tm, tn):
    m, k = x.shape
    n = w.shape[1]
    return pl.pallas_call(
        _mm_kernel,
        grid=(n // tn, m // tm),
        in_specs=[pl.BlockSpec((tm, k), lambda j, i: (i, 0)),
                  pl.BlockSpec((k, tn), lambda j, i: (0, j))],
        out_specs=pl.BlockSpec((tm, tn), lambda j, i: (i, j)),
        out_shape=jax.ShapeDtypeStruct((m, n), out_dtype),
        compiler_params=_cp("arbitrary", "arbitrary"),
        name="in_proj",
    )(x, w)


def _attn_kernel(q_ref, k_ref, v_ref, qi_ref, ki_ref, wi_ref, o_ref,
                 key_ref, lg_ref, mx_ref, acc_ref, m_ref,
                 *, tq, nq, ck, nck, offset, l_true, k_sel, tie_bits):
    j = pl.program_id(1)
    nsub = ck // LANES
    qlane = lax.broadcasted_iota(I32, (1, tq), 1)
    krow = lax.broadcasted_iota(I32, (ck, tq), 0)
    pos = offset + j * tq + qlane
    limit = jnp.minimum((pos // CHUNK + 1) * CHUNK, l_true)
    if nq == 1:
        n_act = (min(((offset + tq - 1) // CHUNK + 1) * CHUNK, l_true) + ck - 1) // ck
    else:
        last_pos = offset + (j + 1) * tq - 1
        max_limit = jnp.minimum((last_pos // CHUNK + 1) * CHUNK, l_true)
        n_act = (max_limit + ck - 1) // ck

    qi = (qi_ref[...] * (IDX_DIM ** -0.5)).astype(BF16)
    qis = [qi[:, h * IDX_DIM:(h + 1) * IDX_DIM] for h in range(IDX_HEADS)]
    stacked = tq % LANES == 0
    if stacked:
        qstack = jnp.concatenate(qis, axis=0)
    wt = wi_ref[...] * (IDX_HEADS ** -0.5)
    wrows = [wt[h:h + 1, :] for h in range(IDX_HEADS)]
    nt = (((1,), (1,)), ((), ()))

    def score_body(c, carry):
        for u in range(nsub):
            r0 = pl.multiple_of(c * ck + u * LANES, LANES)
            kiu = ki_ref[pl.ds(r0, LANES), :][:, :IDX_DIM].astype(BF16)
            if stacked:
                d_all = lax.dot_general(kiu, qstack, nt, preferred_element_type=F32)
                ds = [d_all[:, h * tq:(h + 1) * tq] for h in range(IDX_HEADS)]
            else:
                ds = [lax.dot_general(kiu, qis[h], nt, preferred_element_type=F32) for h in range(IDX_HEADS)]
            s = jnp.zeros((LANES, tq), F32)
            for h in range(IDX_HEADS):
                s = s + jnp.maximum(ds[h], 0.0) * wrows[h]
            bits = lax.bitcast_convert_type(s, I32)
            bits = jnp.where(bits == INT_MIN, 0, bits)
            key = jnp.where(bits >= 0, bits, bits ^ 0x7FFFFFFF)
            kpos = r0 + lax.broadcasted_iota(I32, (LANES, tq), 0)
            key_ref[c, u * LANES:(u + 1) * LANES, :] = jnp.where(kpos < limit, key, INT_MIN)
        return carry

    lax.fori_loop(0, n_act, score_body, 0)

    def count(pred):
        def body(c, acc):
            p = jnp.where(pred(key_ref[c], c * ck + krow), 1.0, 0.0)
            parts = [p[r * SUBLANES:(r + 1) * SUBLANES] for r in range(ck // SUBLANES)]
            while len(parts) > 1:
                parts = [a + b for a, b in zip(parts[0::2], parts[1::2])]
            return acc + parts[0]
        acc = lax.fori_loop(0, n_act, body, jnp.zeros((SUBLANES, tq), F32))
        return jnp.sum(acc, axis=0, keepdims=True)

    def bisect(i, tu):
        cand_u = tu | (jnp.int32(1) << (31 - i))
        cand_s = cand_u ^ INT_MIN
        cnt = count(lambda key, idx: key >= cand_s)
        return jnp.where(cnt >= k_sel, cand_u, tu)

    tu = lax.fori_loop(0, 32, bisect, jnp.zeros((1, tq), I32))
    ts = tu ^ INT_MIN
    c_gt = count(lambda key, idx: key > ts)
    c_ge = count(lambda key, idx: key >= ts)
    want = k_sel - c_gt

    eye = jnp.where(lax.broadcasted_iota(I32, (tq, tq), 0) == lax.broadcasted_iota(I32, (tq, tq), 1), 1.0, 0.0).astype(BF16)
    ts_adm = jnp.maximum(ts, INT_MIN + 1)
    m_ref[...] = jnp.full((1, tq), 2 ** 30, I32)

    @pl.when(jnp.max(c_ge) > k_sel)
    def _():
        def tie_bisect(i, m):
            cand = m | (jnp.int32(1) << (tie_bits - 1 - i))
            cnt = count(lambda key, idx: (key == ts) & (idx < cand))
            return jnp.where(cnt < want, cand, m)
        m_ref[...] = lax.fori_loop(0, tie_bits, tie_bisect, jnp.zeros((1, tq), I32))

    m_last = m_ref[...]

    qscale = (ATT_HD ** -0.5) * 1.4426950408889634
    groups = range(KV_HEADS)
    qss = []
    for g in groups:
        qs = jnp.concatenate(
            [q_ref[:, (g * KV_GROUP + r) * ATT_HD:(g * KV_GROUP + r + 1) * ATT_HD] for r in range(KV_GROUP)], axis=0)
        qss.append((qs.astype(F32) * qscale).astype(BF16))
    mx_ref[...] = jnp.full(mx_ref.shape, NEG_BIG, F32)

    def logit_body(c, carry):
        r0 = pl.multiple_of(c * ck, ck)
        key = key_ref[c]
        sel = (key > ts_adm) | ((key == ts_adm) & (c * ck + krow <= m_last))
        sel_t = jnp.where(sel, 1.0, 0.0).astype(BF16)
        sel_q = lax.dot_general(eye, sel_t, nt, preferred_element_type=F32)
        b = (sel_q - 1.0) * (-NEG_BIG)
        b4 = jnp.concatenate([b] * KV_GROUP, axis=0)
        for g in groups:
            kc = k_ref[pl.ds(r0, ck), g * ATT_HD:(g + 1) * ATT_HD].astype(BF16)
            lg = lax.dot_general(qss[g], kc, nt, preferred_element_type=F32) + b4
            lg_ref[g, c] = lg
            m = mx_ref[g]
            for u in range(nsub):
                m = jnp.maximum(m, lg[:, u * LANES:(u + 1) * LANES])
            mx_ref[g] = m
        return carry

    lax.fori_loop(0, n_act, logit_body, 0)
    m_rows = [jnp.max(mx_ref[g], axis=-1, keepdims=True) for g in groups]
    acc_ref[...] = jnp.zeros(acc_ref.shape, F32)
    ones_cols = jnp.ones((ck, ATT_HD), BF16)

    def pv_body(c, carry):
        r0 = pl.multiple_of(c * ck, ck)
        for g in groups:
            p = jnp.exp2(lg_ref[g, c] - m_rows[g]).astype(BF16)
            vc = v_ref[pl.ds(r0, ck), g * ATT_HD:(g + 1) * ATT_HD].astype(BF16)
            acc_ref[g] += jnp.dot(p, jnp.concatenate([vc, ones_cols], axis=1), preferred_element_type=F32)
        return carry

    lax.fori_loop(0, n_act, pv_body, 0)
    for g in groups:
        a = acc_ref[g]
        o = a[:, :ATT_HD] / a[:, ATT_HD:]
        for r in range(KV_GROUP):
            h = g * KV_GROUP + r
            o_ref[:, h * ATT_HD:(h + 1) * ATT_HD] = o[r * tq:(r + 1) * tq].astype(o_ref.dtype)


def _attention(q_arr, q_map, k_arr, k_map, v_arr, v_map, qi_arr, qi_map, ki_arr, ki_map, wi_arr, wi_map,
               *, nb, t, l_pad, l_true, offset, tq):
    ck = 2 * LANES
    nck = l_pad // ck
    assert l_pad % ck == 0 and t % tq == 0 and nck * (ck // LANES) < 256
    nq = t // tq
    k_sel = min(TOPK_MAX, l_true // 4)
    tie_bits = max(1, (l_pad - 1).bit_length())
    kern = functools.partial(_attn_kernel, tq=tq, nq=nq, ck=ck, nck=nck, offset=offset, l_true=l_true,
                             k_sel=k_sel, tie_bits=tie_bits)
    hq = ATT_HEADS * ATT_HD
    return pl.pallas_call(
        kern,
        grid=(nb, nq),
        in_specs=[pl.BlockSpec(*q_map(tq)), pl.BlockSpec(*k_map(l_pad)), pl.BlockSpec(*v_map(l_pad)),
                  pl.BlockSpec(*qi_map(tq)), pl.BlockSpec(*ki_map(l_pad)), pl.BlockSpec(*wi_map(tq))],
        out_specs=pl.BlockSpec((tq, hq), lambda b, j: (b * nq + j, 0)),
        out_shape=jax.ShapeDtypeStruct((nb * t, hq), BF16),
        scratch_shapes=[pltpu.VMEM((nck, ck, tq), I32),
                        pltpu.VMEM((KV_HEADS, nck, KV_GROUP * tq, ck), F32),
                        pltpu.VMEM((KV_HEADS, KV_GROUP * tq, LANES), F32),
                        pltpu.VMEM((KV_HEADS, KV_GROUP * tq, 2 * ATT_HD), F32),
                        pltpu.VMEM((1, tq), I32)],
        compiler_params=_cp("arbitrary", "arbitrary"),
        name="dsa_attention",
    )(q_arr, k_arr, v_arr, qi_arr, ki_arr, wi_arr)


def _order_key(s):
    bits = lax.bitcast_convert_type(s, I32)
    bits = jnp.where(bits == INT_MIN, 0, bits)
    return jnp.where(bits >= 0, bits, bits ^ 0x7FFFFFFF)


def _sample_select_kernel(kic_ref, new_ref, qi_ref, wi_ref, mask_ref, key_ref, keyn_ref, m_ref,
                          *, ts, gs, past, ck, k_sel, tie_bits):
    nck = past // ck
    nsub = ck // LANES
    l_true = past + ts
    lane = lax.broadcasted_iota(I32, (1, LANES), 1)
    lane_s = lane // ts
    pos = past + lane % ts
    limit = jnp.minimum((pos // CHUNK + 1) * CHUNK, l_true)
    nt = (((1,), (1,)), ((), ()))

    qi = (qi_ref[...] * (IDX_DIM ** -0.5)).astype(BF16)
    qstack = jnp.concatenate([qi[:, h * IDX_DIM:(h + 1) * IDX_DIM] for h in range(IDX_HEADS)], axis=0)
    wt = wi_ref[...] * (IDX_HEADS ** -0.5)
    wrows = [wt[h:h + 1, :] for h in range(IDX_HEADS)]

    def own_stream(d, rows):
        out = d[(gs - 1) * rows:gs * rows]
        for s in range(gs - 2, -1, -1):
            out = jnp.where(lane_s == s, d[s * rows:(s + 1) * rows], out)
        return out

    def scores(kmat, rows):
        d_all = lax.dot_general(kmat, qstack, nt, preferred_element_type=F32)
        s = jnp.zeros((rows, LANES), F32)
        for h in range(IDX_HEADS):
            s = s + jnp.maximum(own_stream(d_all[:, h * LANES:(h + 1) * LANES], rows), 0.0) * wrows[h]
        return _order_key(s)

    def score_body(c, carry):
        for u in range(nsub):
            r0 = pl.multiple_of(c * ck + u * LANES, LANES)
            kmat = jnp.concatenate([kic_ref[s, pl.ds(r0, LANES), :] for s in range(gs)], axis=0).astype(BF16)
            kpos = r0 + lax.broadcasted_iota(I32, (LANES, LANES), 0)
            key_ref[c, u * LANES:(u + 1) * LANES, :] = jnp.where(kpos < limit, scores(kmat, LANES), INT_MIN)
        return carry

    lax.fori_loop(0, nck, score_body, 0)
    kn = new_ref[:, :IDX_DIM].astype(BF16)
    npos = past + lax.broadcasted_iota(I32, (ts, LANES), 0)
    keyn_ref[...] = jnp.where(npos < limit, scores(kn, ts), INT_MIN)

    krow = lax.broadcasted_iota(I32, (ck, LANES), 0)

    def tree(p, rows):
        parts = [p[r * SUBLANES:(r + 1) * SUBLANES] for r in range(rows // SUBLANES)]
        while len(parts) > 1:
            parts = [a + b for a, b in zip(parts[0::2], parts[1::2])]
        return parts[0]

    def count(pred):
        def body(c, acc):
            p = jnp.where(pred(key_ref[c], c * ck + krow), 1.0, 0.0)
            return acc + tree(p, ck)
        acc = lax.fori_loop(0, nck, body, jnp.zeros((SUBLANES, LANES), F32))
        acc = acc + tree(jnp.where(pred(keyn_ref[...], npos), 1.0, 0.0), ts)
        return jnp.sum(acc, axis=0, keepdims=True)

    def bisect(i, tu):
        cand_u = tu | (jnp.int32(1) << (31 - i))
        cand_s = cand_u ^ INT_MIN
        cnt = count(lambda key, idx: key >= cand_s)
        return jnp.where(cnt >= k_sel, cand_u, tu)

    tu = lax.fori_loop(0, 32, bisect, jnp.zeros((1, LANES), I32))
    ts_ = tu ^ INT_MIN
    c_gt = count(lambda key, idx: key > ts_)
    c_ge = count(lambda key, idx: key >= ts_)
    want = k_sel - c_gt
    m_ref[...] = jnp.full((1, LANES), 2 ** 30, I32)

    @pl.when(jnp.max(c_ge) > k_sel)
    def _():
        def tie_bisect(i, m):
            cand = m | (jnp.int32(1) << (tie_bits - 1 - i))
            cnt = count(lambda key, idx: (key == ts_) & (idx < cand))
            return jnp.where(cnt < want, cand, m)
        m_ref[...] = lax.fori_loop(0, tie_bits, tie_bisect, jnp.zeros((1, LANES), I32))

    m_last = m_ref[...]
    ts_adm = jnp.maximum(ts_, INT_MIN + 1)
    eye = jnp.where(lax.broadcasted_iota(I32, (LANES, LANES), 0) == lax.broadcasted_iota(I32, (LANES, LANES), 1),
                    1.0, 0.0).astype(BF16)

    def selected(key, idx):
        sel = (key > ts_adm) | ((key == ts_adm) & (idx <= m_last))
        return jnp.where(sel, 1.0, 0.0).astype(BF16)

    for c in range(nck):
        sel_q = lax.dot_general(eye, selected(key_ref[c], c * ck + krow), nt, preferred_element_type=F32)
        mask_ref[:, c * ck:(c + 1) * ck] = sel_q.astype(BF16)
    sel_n = lax.dot_general(eye, selected(keyn_ref[...], npos), nt, preferred_element_type=F32)
    mask_ref[:, past:past + LANES] = jnp.concatenate(
        [sel_n, jnp.zeros((LANES, LANES - ts), F32)], axis=1).astype(BF16)


def _sample_select(cache_ik, p_f, wi_t, np_, nbs, ts, past):
    gs = LANES // ts
    ck = 2 * LANES
    assert LANES % ts == 0 and nbs % gs == 0 and past % ck == 0 and np_ % LANES == 0 and ts % SUBLANES == 0
    l_true = past + ts
    k_sel = min(TOPK_MAX, l_true // 4)
    kern = functools.partial(_sample_select_kernel, ts=ts, gs=gs, past=past, ck=ck, k_sel=k_sel,
                             tie_bits=max(1, (l_true - 1).bit_length()))
    rb = np_ // LANES
    return pl.pallas_call(
        kern,
        grid=(nbs // gs,),
        in_specs=[pl.BlockSpec((gs, past, IDX_DIM), lambda g: (g, 0, 0)),
                  pl.BlockSpec((LANES, LANES), lambda g: (rb + g, KI_OFF // LANES)),
                  pl.BlockSpec((LANES, IDX_HEADS * IDX_DIM), lambda g: (rb + g, QI_OFF // (IDX_HEADS * IDX_DIM))),
                  pl.BlockSpec((IDX_HEADS, LANES), lambda g: (0, rb + g))],
        out_specs=pl.BlockSpec((LANES, past + LANES), lambda g: (g, 0)),
        out_shape=jax.ShapeDtypeStruct((nbs * ts, past + LANES), BF16),
        scratch_shapes=[pltpu.VMEM((past // ck, ck, LANES), I32),
                        pltpu.VMEM((ts, LANES), I32),
                        pltpu.VMEM((1, LANES), I32)],
        compiler_params=_cp("arbitrary"),
        name="sample_select",
    )(cache_ik, p_f, p_f, wi_t)


def _sample_attend_kernel(q_ref, kc_ref, vc_ref, kn_ref, vn_ref, mask_ref, o_ref, lg_ref, *, ts, past, ca):
    nt = (((1,), (1,)), ((), ()))
    qscale = (ATT_HD ** -0.5) * 1.4426950408889634
    nca = past // ca
    rows = KV_GROUP * ts

    def bias_of(m):
        b = (m.astype(F32) - 1.0) * (-NEG_BIG)
        return jnp.concatenate([b] * KV_GROUP, axis=0)

    bias_n = bias_of(mask_ref[:, past:past + ts])
    for g in range(KV_HEADS):
        qs = jnp.concatenate(
            [q_ref[:, (g * KV_GROUP + r) * ATT_HD:(g * KV_GROUP + r + 1) * ATT_HD] for r in range(KV_GROUP)], axis=0)
        qs = (qs.astype(F32) * qscale).astype(BF16)
        cols = slice(g * ATT_HD, (g + 1) * ATT_HD)
        lg_n = lax.dot_general(qs, kn_ref[:, cols].astype(BF16), nt, preferred_element_type=F32) + bias_n
        m_run = jnp.full((rows, LANES), NEG_BIG, F32)
        for c in range(nca):
            kc = kc_ref[c * ca:(c + 1) * ca, cols].astype(BF16)
            lg = lax.dot_general(qs, kc, nt, preferred_element_type=F32) + bias_of(mask_ref[:, c * ca:(c + 1) * ca])
            lg_ref[g, :, c * ca:(c + 1) * ca] = lg
            for u in range(ca // LANES):
                m_run = jnp.maximum(m_run, lg[:, u * LANES:(u + 1) * LANES])
        m_row = jnp.maximum(jnp.max(m_run, axis=-1, keepdims=True), jnp.max(lg_n, axis=-1, keepdims=True))
        ones_n = jnp.ones((ts, ATT_HD), BF16)
        acc = jnp.dot(jnp.exp2(lg_n - m_row).astype(BF16),
                      jnp.concatenate([vn_ref[:, cols].astype(BF16), ones_n], axis=1), preferred_element_type=F32)
        ones_c = jnp.ones((ca, ATT_HD), BF16)
        for c in range(nca):
            p = jnp.exp2(lg_ref[g, :, c * ca:(c + 1) * ca] - m_row).astype(BF16)
            vc = vc_ref[c * ca:(c + 1) * ca, cols].astype(BF16)
            acc = acc + jnp.dot(p, jnp.concatenate([vc, ones_c], axis=1), preferred_element_type=F32)
        o = acc[:, :ATT_HD] / acc[:, ATT_HD:]
        for r in range(KV_GROUP):
            h = g * KV_GROUP + r
            o_ref[:, h * ATT_HD:(h + 1) * ATT_HD] = o[r * ts:(r + 1) * ts].astype(o_ref.dtype)


def _sample_attend(p_bf, qa_blk, cache_k, cache_v, p_f, mask, np_, nbs, ts, past):
    kvw = KV_HEADS * ATT_HD
    hq = ATT_HEADS * ATT_HD
    ca = 4 * LANES
    assert past % ca == 0
    rb = np_ // ts
    kern = functools.partial(_sample_attend_kernel, ts=ts, past=past, ca=ca)
    return pl.pallas_call(
        kern,
        grid=(nbs,),
        in_specs=[pl.BlockSpec((ts, hq), lambda b: (rb + b, qa_blk)),
                  pl.BlockSpec((None, past, kvw), lambda b: (b, 0, 0)),
                  pl.BlockSpec((None, past, kvw), lambda b: (b, 0, 0)),
                  pl.BlockSpec((ts, kvw), lambda b: (rb + b, KA_OFF // kvw)),
                  pl.BlockSpec((ts, kvw), lambda b: (rb + b, VA_OFF // kvw)),
                  pl.BlockSpec((ts, past + LANES), lambda b: (b, 0))],
        out_specs=pl.BlockSpec((ts, hq), lambda b: (b, 0)),
        out_shape=jax.ShapeDtypeStruct((nbs * ts, hq), BF16),
        scratch_shapes=[pltpu.VMEM((KV_HEADS, KV_GROUP * ts, past), F32)],
        compiler_params=_cp("arbitrary"),
        name="sample_attend",
    )(p_bf, cache_k, cache_v, p_f, p_f, mask)


def _ret_kernel(q_ref, k_ref, v_ref, g_ref, cs_ref, sn_ref, dec_ref, xi_ref, zt_ref, gp_ref, gn_ref, s0_ref,
                y_ref, sf_ref, s_ref):
    c = pl.program_id(1)
    dk, dv = RET_DK, RET_DV

    @pl.when(c == 0)
    def _():
        s_ref[...] = s0_ref[...]

    cs = cs_ref[...]
    sn = sn_ref[...]

    def rot(x):
        xf = x.astype(F32)
        return xf * cs + pltpu.roll(xf, dk // 2, 1) * sn

    for h in range(RET_HEADS):
        q = rot(q_ref[:, h * dk:(h + 1) * dk])
        k = rot(k_ref[:, h * dk:(h + 1) * dk]) * (dk ** -0.5)
        qb = q.astype(BF16)
        kb = k.astype(BF16)
        v = v_ref[:, h * dv:(h + 1) * dv]
        inner = lax.dot_general(qb, kb, (((1,), (1,)), ((), ())), preferred_element_type=F32) * dec_ref[h]
        s = s_ref[h]
        o = jnp.dot(inner.astype(BF16), v, preferred_element_type=F32)
        o = o + jnp.dot(qb, s.astype(BF16), preferred_element_type=F32) * xi_ref[h]
        kz = (k * zt_ref[h]).T.astype(BF16)
        s_new = s * gp_ref[h] + jnp.dot(kz, v, preferred_element_type=F32)
        s_ref[h] = s_new
        sf_ref[h] = s_new

        mu = jnp.mean(o, axis=-1, keepdims=True)
        oc = o - mu
        var = jnp.mean(oc * oc, axis=-1, keepdims=True)
        yn = oc * lax.rsqrt(var + LN_EPS) * gn_ref[:, h * dv:(h + 1) * dv]
        gg = g_ref[:, h * dv:(h + 1) * dv].astype(F32)
        y_ref[:, h * dv:(h + 1) * dv] = (gg * jax.nn.sigmoid(gg) * yn).astype(y_ref.dtype)


def _retention(p_bf, row0, nb, t, cl, offset, s0, gn_g):
    nc = t // cl
    h_, dk, dv = RET_HEADS, RET_DK, RET_DV
    half = dk // 2
    pos = (offset + jnp.arange(t, dtype=I32)).astype(F32)
    inv = 1.0 / (ROPE_BASE ** (jnp.arange(half, dtype=F32) / half))
    ang = pos[:, None] * inv[None, :]
    cos, sin = jnp.cos(ang), jnp.sin(ang)
    cs2 = jnp.concatenate([cos, cos], axis=-1)
    sn2 = jnp.concatenate([-sin, sin], axis=-1)
    log_g = jnp.log1p(-jnp.exp2(-5.0 - jnp.arange(h_, dtype=F32)))
    n = jnp.arange(cl, dtype=F32)
    diff = n[:, None] - n[None, :]
    decay = jnp.where(diff >= 0, jnp.exp(log_g[:, None, None] * jnp.maximum(diff, 0.0)), 0.0)
    xi = jnp.exp(log_g[:, None] * (n + 1.0))[..., None]
    zeta = jnp.exp(log_g[:, None] * (cl - 1.0 - n))[..., None]
    gpow = jnp.broadcast_to(jnp.exp(log_g * cl)[:, None, None], (h_, 1, dv))
    rb = row0 // cl
    return pl.pallas_call(
        _ret_kernel,
        grid=(nb, nc),
        in_specs=[pl.BlockSpec((cl, h_ * dk), lambda b, c: (rb + b * nc + c, QR_OFF // (h_ * dk))),
                  pl.BlockSpec((cl, h_ * dk), lambda b, c: (rb + b * nc + c, KR_OFF // (h_ * dk))),
                  pl.BlockSpec((cl, h_ * dv), lambda b, c: (rb + b * nc + c, VR_OFF // (h_ * dv))),
                  pl.BlockSpec((cl, h_ * dv), lambda b, c: (rb + b * nc + c, GR_OFF // (h_ * dv))),
                  pl.BlockSpec((cl, dk), lambda b, c: (c, 0)),
                  pl.BlockSpec((cl, dk), lambda b, c: (c, 0)),
                  pl.BlockSpec((h_, cl, cl), lambda b, c: (0, 0, 0)),
                  pl.BlockSpec((h_, cl, 1), lambda b, c: (0, 0, 0)),
                  pl.BlockSpec((h_, cl, 1), lambda b, c: (0, 0, 0)),
                  pl.BlockSpec((h_, 1, dv), lambda b, c: (0, 0, 0)),
                  pl.BlockSpec((1, h_ * dv), lambda b, c: (0, 0)),
                  pl.BlockSpec((None, h_, dk, dv), lambda b, c: (b, 0, 0, 0))],
        out_specs=[pl.BlockSpec((cl, h_ * dv), lambda b, c: (b * nc + c, 0)),
                   pl.BlockSpec((None, h_, dk, dv), lambda b, c: (b, 0, 0, 0))],
        out_shape=[jax.ShapeDtypeStruct((nb * t, h_ * dv), BF16),
                   jax.ShapeDtypeStruct((nb, h_, dk, dv), F32)],
        scratch_shapes=[pltpu.VMEM((h_, dk, dv), F32)],
        compiler_params=_cp("arbitrary", "arbitrary"),
        name="retention",
    )(p_bf, p_bf, p_bf, p_bf, cs2, sn2, decay, xi, zeta, gpow, gn_g.reshape(1, -1), s0)


CONV_HALO = 32
CONV_ROWS = 64
CONV_BLOCK = 64


def _conv_kernel(ca_ref, cb_ref, st_ref, w_ref, b_ref, lg_ref, lb_ref, y_ref, tail_ref, sh_ref, h_ref, *, tt):
    t = pl.program_id(1)
    ext = tt + CONV_HALO

    @pl.when(t == 0)
    def _():
        sh_ref[0, 0:CONV_HALO] = st_ref[...]

    @pl.when(t > 0)
    def _():
        sh_ref[0, 0:CONV_HALO] = sh_ref[0, tt:ext]

    ca = ca_ref[...].astype(F32)
    cb = cb_ref[...].astype(F32)
    sh_ref[0, CONV_HALO:ext] = ca * jax.nn.sigmoid(cb)
    tail_ref[...] = sh_ref[0, tt:ext]
    for s in range(1, SUBLANES):
        sh_ref[s, 0:ext - SUBLANES] = sh_ref[0, s:ext - SUBLANES + s]

    lead = CONV_HALO - (CONV_W - 1)
    blk = min(CONV_BLOCK, tt)
    nblk = blk // SUBLANES

    for lc in range(CONV_CH // LANES):
        cols = slice(lc * LANES, (lc + 1) * LANES)
        taps = [jnp.broadcast_to(w_ref[k:k + 1, cols], (SUBLANES, LANES)) for k in range(CONV_W)]
        bias = jnp.broadcast_to(b_ref[:, cols], (SUBLANES, LANES))

        def conv_body(i, carry):
            r0 = pl.multiple_of(i * blk, blk)
            accs = [bias] * nblk
            for k in range(CONV_W):
                a, s = divmod(k + lead, SUBLANES)
                u = sh_ref[s, pl.ds(r0 + a * SUBLANES, blk), cols]
                accs = [accs[j] + taps[k] * u[j * SUBLANES:(j + 1) * SUBLANES] for j in range(nblk)]
            h_ref[pl.ds(r0, blk), cols] = jnp.concatenate(accs, axis=0)
            return carry

        lax.fori_loop(0, tt // blk, conv_body, 0)

    rows = min(CONV_ROWS, tt)

    def norm_body(i, carry):
        r0 = pl.multiple_of(i * rows, rows)
        hn = _layer_norm(h_ref[pl.ds(r0, rows), :], lg_ref[...], lb_ref[...])
        y_ref[pl.ds(r0, rows), :] = (hn * jax.nn.sigmoid(hn)).astype(y_ref.dtype)
        return carry

    lax.fori_loop(0, tt // rows, norm_body, 0)


def _conv_module(p_bf, row0, nb, t, tt, state, dw, dw_b, ln_g, ln_b):
    nt = t // tt
    lead = CONV_HALO - (CONV_W - 1)
    st = jnp.pad(state, ((0, 0), (lead, 0), (0, 0)))
    rb = row0 // tt
    kern = functools.partial(_conv_kernel, tt=tt)
    y, tail = pl.pallas_call(
        kern,
        grid=(nb, nt),
        in_specs=[pl.BlockSpec((tt, CONV_CH), lambda b, i: (rb + b * nt + i, CIN_OFF // CONV_CH)),
                  pl.BlockSpec((tt, CONV_CH), lambda b, i: (rb + b * nt + i, CIN_OFF // CONV_CH + 1)),
                  pl.BlockSpec((None, CONV_HALO, CONV_CH), lambda b, i: (b, 0, 0)),
                  pl.BlockSpec((CONV_W, CONV_CH), lambda b, i: (0, 0)),
                  pl.BlockSpec((1, CONV_CH), lambda b, i: (0, 0)),
                  pl.BlockSpec((1, CONV_CH), lambda b, i: (0, 0)),
                  pl.BlockSpec((1, CONV_CH), lambda b, i: (0, 0))],
        out_specs=[pl.BlockSpec((tt, CONV_CH), lambda b, i: (b * nt + i, 0)),
                   pl.BlockSpec((None, CONV_HALO, CONV_CH), lambda b, i: (b, 0, 0))],
        out_shape=[jax.ShapeDtypeStruct((nb * t, CONV_CH), BF16),
                   jax.ShapeDtypeStruct((nb, CONV_HALO, CONV_CH), F32)],
        scratch_shapes=[pltpu.VMEM((SUBLANES, tt + CONV_HALO, CONV_CH), F32),
                        pltpu.VMEM((tt, CONV_CH), F32)],
        compiler_params=_cp("arbitrary", "arbitrary"),
        name="conv_module",
    )(p_bf, p_bf, st, dw, dw_b.reshape(1, -1), ln_g.reshape(1, -1), ln_b.reshape(1, -1))
    return y, tail[:, lead:, :]


def _merge_kernel(ap_ref, rp_ref, cp_ref, as_ref, rs_ref, cs_ref, ga_ref, gb_ref, gc_ref, ba_ref, bb_ref, bc_ref,
                  wa_ref, wr_ref, wc_ref, z_ref, *, prompt_tiles):
    ga = jax.nn.sigmoid(ga_ref[...].astype(F32) + ba_ref[...])
    gb = jax.nn.sigmoid(gb_ref[...].astype(F32) + bb_ref[...])
    gc = jax.nn.sigmoid(gc_ref[...].astype(F32) + bc_ref[...])

    is_prompt = pl.program_id(1) < prompt_tiles
    ya = jnp.dot(jnp.where(is_prompt, ap_ref[...], as_ref[...]), wa_ref[...], preferred_element_type=F32)
    yb = jnp.dot(jnp.where(is_prompt, rp_ref[...], rs_ref[...]), wr_ref[...], preferred_element_type=F32)
    yc = jnp.dot(jnp.where(is_prompt, cp_ref[...], cs_ref[...]), wc_ref[...], preferred_element_type=F32)
    z_ref[...] = (ga * ya + gb * yb + gc * yc).astype(z_ref.dtype)


def _merge(att_p, ret_p, cnv_p, att_s, ret_s, cnv_s, p_bf, b_gate, wa, wr, wc, tm, tn):
    np_, ns = att_p.shape[0], att_s.shape[0]
    assert np_ % tm == 0 and ns % tm == 0
    m = np_ + ns
    pt = np_ // tm
    d = wa.shape[1]
    nj = d // tn
    g0 = GATE_OFF // tn
    bg = b_gate.reshape(1, -1)

    def prompt_rows(j, i):
        return (jnp.minimum(i, pt - 1), 0)

    def sample_rows(j, i):
        return (jnp.maximum(i - pt, 0), 0)

    return pl.pallas_call(
        functools.partial(_merge_kernel, prompt_tiles=pt),
        grid=(nj, m // tm),
        in_specs=[pl.BlockSpec((tm, att_p.shape[1]), prompt_rows),
                  pl.BlockSpec((tm, ret_p.shape[1]), prompt_rows),
                  pl.BlockSpec((tm, cnv_p.shape[1]), prompt_rows),
                  pl.BlockSpec((tm, att_s.shape[1]), sample_rows),
                  pl.BlockSpec((tm, ret_s.shape[1]), sample_rows),
                  pl.BlockSpec((tm, cnv_s.shape[1]), sample_rows),
                  pl.BlockSpec((tm, tn), lambda j, i: (i, g0 + j)),
                  pl.BlockSpec((tm, tn), lambda j, i: (i, g0 + nj + j)),
                  pl.BlockSpec((tm, tn), lambda j, i: (i, g0 + 2 * nj + j)),
                  pl.BlockSpec((1, tn), lambda j, i: (0, j)),
                  pl.BlockSpec((1, tn), lambda j, i: (0, nj + j)),
                  pl.BlockSpec((1, tn), lambda j, i: (0, 2 * nj + j)),
                  pl.BlockSpec((wa.shape[0], tn), lambda j, i: (0, j)),
                  pl.BlockSpec((wr.shape[0], tn), lambda j, i: (0, j)),
                  pl.BlockSpec((wc.shape[0], tn), lambda j, i: (0, j))],
        out_specs=pl.BlockSpec((tm, tn), lambda j, i: (i, j)),
        out_shape=jax.ShapeDtypeStruct((m, d), BF16),
        compiler_params=_cp("arbitrary", "arbitrary"),
        name="branch_merge",
    )(att_p, ret_p, cnv_p, att_s, ret_s, cnv_s, p_bf, p_bf, p_bf, bg, bg, bg, wa, wr, wc)


def _route(x, w_hl, b):
    tm = x.shape[0]
    x_hi = x.astype(BF16)
    x_lo = (x - x_hi.astype(F32)).astype(BF16)
    d_hi = jnp.dot(x_hi, w_hl, preferred_element_type=F32)
    d_lo = jnp.dot(x_lo, w_hl[:, :LANES], preferred_element_type=F32)
    logits = d_hi[:, :LANES] + d_hi[:, LANES:] + d_lo + b
    lane = lax.broadcasted_iota(I32, (tm, LANES), 1)
    valid = lane < N_EXPERTS
    lm = jnp.where(valid, logits, NEG_BIG)
    e = jnp.where(valid, jnp.exp(lm - jnp.max(lm, axis=-1, keepdims=True)), 0.0)
    aff = e / jnp.sum(e, axis=-1, keepdims=True)

    def top2(vals):
        v1 = jnp.max(vals, axis=-1, keepdims=True)
        i1 = jnp.min(jnp.where(vals == v1, lane, LANES), axis=-1, keepdims=True)
        rest = jnp.where(lane == i1, -2.0, vals)
        v2 = jnp.max(rest, axis=-1, keepdims=True)
        i2 = jnp.min(jnp.where(rest == v2, lane, LANES), axis=-1, keepdims=True)
        return v1, i1, v2, i2

    grp = lane // EXP_PER_GROUP
    best = jnp.zeros((tm, 1), I32)
    best_score = None
    for g in range(N_GROUPS):
        v1, _, v2, _ = top2(jnp.where(grp == g, aff, -1.0))
        score = v1 + v2
        if g == 0:
            best_score = score
        else:
            better = score > best_score
            best = jnp.where(better, g, best)
            best_score = jnp.where(better, score, best_score)
    v1, i1, v2, i2 = top2(jnp.where(grp == best, aff, -1.0))
    tot = v1 + v2
    return (jnp.where(lane == 0, i1.astype(F32), 0.0) + jnp.where(lane == 1, i2.astype(F32), 0.0)
            + jnp.where(lane == 2, v1 / tot, 0.0) + jnp.where(lane == 3, v2 / tot, 0.0))


def _outproj_ln_kernel(z_ref, w_ref, x_ref, g_ref, b_ref, wr_ref, br_ref, of_ref, ob_ref, rt_ref, *, alpha):
    mix = jnp.dot(z_ref[...], w_ref[...], preferred_element_type=F32)
    y = _layer_norm(alpha * x_ref[...] + mix, g_ref[...], b_ref[...])
    of_ref[...] = y
    ob_ref[...] = y.astype(BF16)
    rt_ref[...] = _route(y, wr_ref[...], br_ref[...])


def _outproj_ln(z, w_out, x, g, b, w_router, b_router, alpha, tm):
    m, d = x.shape
    wr = jnp.pad(w_router, ((0, 0), (0, LANES - N_EXPERTS)))
    wr_hi = wr.astype(BF16)
    wr = jnp.concatenate([wr_hi, (wr - wr_hi.astype(F32)).astype(BF16)], axis=1)
    br = jnp.pad(b_router, (0, LANES - N_EXPERTS)).reshape(1, -1)
    kern = functools.partial(_outproj_ln_kernel, alpha=alpha)
    return pl.pallas_call(
        kern,
        grid=(m // tm,),
        in_specs=[pl.BlockSpec((tm, d), lambda i: (i, 0)),
                  pl.BlockSpec((d, d), lambda i: (0, 0)),
                  pl.BlockSpec((tm, d), lambda i: (i, 0)),
                  pl.BlockSpec((1, d), lambda i: (0, 0)),
                  pl.BlockSpec((1, d), lambda i: (0, 0)),
                  pl.BlockSpec((d, 2 * LANES), lambda i: (0, 0)),
                  pl.BlockSpec((1, LANES), lambda i: (0, 0))],
        out_specs=[pl.BlockSpec((tm, d), lambda i: (i, 0)), pl.BlockSpec((tm, d), lambda i: (i, 0)),
                   pl.BlockSpec((tm, LANES), lambda i: (i, 0))],
        out_shape=[jax.ShapeDtypeStruct((m, d), F32), jax.ShapeDtypeStruct((m, d), BF16),
                   jax.ShapeDtypeStruct((m, LANES), F32)],
        compiler_params=_cp("arbitrary"),
        name="out_proj_ln1_route",
    )(z, w_out, x, g.reshape(1, -1), b.reshape(1, -1), wr, br)


MOE_TILE = 256
MOE_ISSUE_GROUPS = 8


def _gather_rows(idx_ref, src_hbm, dst_ref, sem, lo, hi):
    for r in range(lo, hi):
        pltpu.make_async_copy(src_hbm.at[pl.ds(idx_ref[0, 0, r], 1)], dst_ref.at[pl.ds(r, 1)], sem).start()


def _wait_rows(src_hbm, dst_ref, sem, n):
    pltpu.make_async_copy(src_hbm.at[pl.ds(0, n)], dst_ref, sem).wait()


def _moe_expert_kernel(te_ref, cur_ref, nxt_ref, x_hbm, w1_ref, w3_ref, w2_ref, o_ref,
                       xg_ref, sem, w1b_ref, w3b_ref, w2b_ref):
    t = pl.program_id(0)
    nt = pl.num_programs(0)
    tile = xg_ref.shape[1]
    slot = t % 2

    @pl.when(t == 0)
    def _():
        _gather_rows(cur_ref, x_hbm, xg_ref.at[0], sem.at[0], 0, tile)

    @pl.when((t == 0) | (te_ref[t] != te_ref[jnp.maximum(t - 1, 0)]))
    def _():
        w1b_ref[...] = w1_ref[...].astype(BF16)
        w3b_ref[...] = w3_ref[...].astype(BF16)
        w2b_ref[...] = w2_ref[...].astype(BF16)

    _wait_rows(x_hbm, xg_ref.at[slot], sem.at[slot], tile)
    d = xg_ref.shape[2]
    kc, rc = d // MOE_ISSUE_GROUPS, tile // MOE_ISSUE_GROUPS
    h1 = h3 = None
    for c in range(MOE_ISSUE_GROUPS):
        xk = xg_ref[slot, :, c * kc:(c + 1) * kc].astype(BF16)
        p1 = jnp.dot(xk, w1b_ref[c * kc:(c + 1) * kc, :], preferred_element_type=F32)
        p3 = jnp.dot(xk, w3b_ref[c * kc:(c + 1) * kc, :], preferred_element_type=F32)
        h1 = p1 if h1 is None else h1 + p1
        h3 = p3 if h3 is None else h3 + p3
        _gather_rows(nxt_ref, x_hbm, xg_ref.at[1 - slot], sem.at[1 - slot], c * rc, (c + 1) * rc)
    h = h1 * jax.nn.sigmoid(h1) * h3
    o_ref[...] = jnp.dot(h.astype(BF16), w2b_ref[...], preferred_element_type=F32)

    @pl.when(t == nt - 1)
    def _():
        _wait_rows(x_hbm, xg_ref.at[1 - slot], sem.at[1 - slot], tile)


def _moe_experts(x, w1, w3, w2, tile_expert, src_tiles):
    n, d = x.shape
    _, _, de = w1.shape
    nt = src_tiles.shape[0] - 1
    tile = src_tiles.shape[2]
    grid_spec = pltpu.PrefetchScalarGridSpec(
        num_scalar_prefetch=1,
        grid=(nt,),
        in_specs=[pl.BlockSpec((1, 1, tile), lambda t, te: (t, 0, 0), memory_space=pltpu.SMEM),
                  pl.BlockSpec((1, 1, tile), lambda t, te: (t + 1, 0, 0), memory_space=pltpu.SMEM),
                  pl.BlockSpec(memory_space=pl.ANY),
                  pl.BlockSpec((None, d, de), lambda t, te: (te[t], 0, 0)),
                  pl.BlockSpec((None, d, de), lambda t, te: (te[t], 0, 0)),
                  pl.BlockSpec((None, de, d), lambda t, te: (te[t], 0, 0))],
        out_specs=pl.BlockSpec((tile, d), lambda t, te: (t, 0)),
        scratch_shapes=[pltpu.VMEM((2, tile, d), F32),
                        pltpu.SemaphoreType.DMA((2,)),
                        pltpu.VMEM((d, de), BF16),
                        pltpu.VMEM((d, de), BF16),
                        pltpu.VMEM((de, d), BF16)],
    )
    return pl.pallas_call(
        _moe_expert_kernel,
        grid_spec=grid_spec,
        out_shape=jax.ShapeDtypeStruct((nt * tile, d), F32),
        compiler_params=_cp("arbitrary"),
        name="moe_experts",
    )(tile_expert, src_tiles, src_tiles, x, w1, w3, w2)


def _moe_combine_kernel(cur_ref, nxt_ref, y_hbm, rt_ref, x_ref, g_ref, b_ref, of_ref, ob_ref, yg_ref, sem, *, alpha):
    t = pl.program_id(0)
    nt = pl.num_programs(0)
    tm = x_ref.shape[0]
    slot = t % 2

    @pl.when(t == 0)
    def _():
        _gather_rows(cur_ref, y_hbm, yg_ref.at[0], sem.at[0], 0, 2 * tm)

    _gather_rows(nxt_ref, y_hbm, yg_ref.at[1 - slot], sem.at[1 - slot], 0, 2 * tm)
    _wait_rows(y_hbm, yg_ref.at[slot], sem.at[slot], 2 * tm)
    rt = rt_ref[...]
    moe = rt[:, 2:3] * yg_ref[slot, 0:tm] + rt[:, 3:4] * yg_ref[slot, tm:2 * tm]
    y = _layer_norm(alpha * x_ref[...] + moe, g_ref[...], b_ref[...])
    of_ref[...] = y
    ob_ref[...] = y.astype(BF16)

    @pl.when(t == nt - 1)
    def _():
        _wait_rows(y_hbm, yg_ref.at[1 - slot], sem.at[1 - slot], 2 * tm)


def _moe_combine_ln(y_sorted, pos_tiles, route, x, g, b, alpha, tm):
    m, d = x.shape
    kern = functools.partial(_moe_combine_kernel, alpha=alpha)
    return pl.pallas_call(
        kern,
        grid=(m // tm,),
        in_specs=[pl.BlockSpec((1, 1, 2 * tm), lambda t: (t, 0, 0), memory_space=pltpu.SMEM),
                  pl.BlockSpec((1, 1, 2 * tm), lambda t: (t + 1, 0, 0), memory_space=pltpu.SMEM),
                  pl.BlockSpec(memory_space=pl.ANY),
                  pl.BlockSpec((tm, LANES), lambda t: (t, 0)),
                  pl.BlockSpec((tm, d), lambda t: (t, 0)),
                  pl.BlockSpec((1, d), lambda t: (0, 0)),
                  pl.BlockSpec((1, d), lambda t: (0, 0))],
        out_specs=[pl.BlockSpec((tm, d), lambda t: (t, 0)), pl.BlockSpec((tm, d), lambda t: (t, 0))],
        out_shape=[jax.ShapeDtypeStruct((m, d), F32), jax.ShapeDtypeStruct((m, d), BF16)],
        scratch_shapes=[pltpu.VMEM((2, 2 * tm, d), F32), pltpu.SemaphoreType.DMA((2,))],
        compiler_params=_cp("arbitrary"),
        name="moe_combine_ln2",
    )(pos_tiles, pos_tiles, y_sorted, route, x, g.reshape(1, -1), b.reshape(1, -1))


def _moe_layout(route, tile, tm):
    n = route.shape[0]
    e_flat = route[:, 0:2].astype(I32).reshape(-1)
    npair = 2 * n
    nt = -(-npair // tile) + N_EXPERTS
    onehot = (e_flat[:, None] == jnp.arange(N_EXPERTS, dtype=I32)[None, :]).astype(I32)
    csum = jnp.cumsum(onehot, axis=0)
    counts = csum[-1]
    rank = jnp.sum((csum - onehot) * onehot, axis=1)
    padded = (counts + tile - 1) // tile * tile
    ends = jnp.cumsum(padded)
    starts = ends - padded
    pos = starts[e_flat] + rank
    rows = nt * tile
    src = jnp.zeros((rows,), I32).at[pos].set(jnp.arange(npair, dtype=I32) // 2, unique_indices=True)
    tile_expert = jnp.minimum(jnp.searchsorted(ends, jnp.arange(nt, dtype=I32) * tile, side='right'),
                              N_EXPERTS - 1).astype(I32)
    src_tiles = jnp.concatenate([src, jnp.zeros((tile,), I32)]).reshape(nt + 1, 1, tile)
    pos_t = pos.reshape(n // tm, tm, 2).transpose(0, 2, 1).reshape(n // tm, 1, 2 * tm)
    pos_tiles = jnp.concatenate([pos_t, jnp.zeros((1, 1, 2 * tm), I32)], axis=0)
    return tile_expert, src_tiles, pos_tiles


def kernel(x_prompt, x_sample, cache_k, cache_v, cache_idx_k, state_ret, state_conv, w_in, b_gate, w_att_o, ret_gn_g,
           w_ret_o, conv_dw, conv_dw_b, conv_ln_g, conv_ln_b, w_conv_o, w_out, ln1_g, ln1_b, w_router, b_router,
           moe_w1, moe_w3, moe_w2, ln2_g, ln2_b):
    nbp, tp, d = x_prompt.shape
    nbs, ts, _ = x_sample.shape
    depth = w_in.shape[0]
    past = cache_k.shape[2]
    np_, ns = nbp * tp, nbs * ts
    n = np_ + ns
    alpha = (2 * depth) ** 0.25
    kvw = KV_HEADS * ATT_HD

    tm = _pow2_tile(n, 512)
    tm_in = 3 * tm // 2 if n % (3 * tm // 2) == 0 else tm
    tq_p = _pow2_tile(tp, 256)
    cl_p = _pow2_tile(tp, 256)
    tt_p = _pow2_tile(tp, 256)
    ls_true = past + ts
    ls_pad = -(-ls_true // (2 * LANES)) * (2 * LANES)
    assert tp % (2 * LANES) == 0 and np_ % max(tq_p, cl_p, tt_p, ts) == 0

    x = jnp.concatenate([x_prompt.reshape(np_, d), x_sample.reshape(ns, d)], axis=0)
    xb = x.astype(BF16)

    c1 = ATT_HEADS * ATT_HD
    c2 = c1 + F32_USED
    rps, cps, kss, vss, ikss, rss, css = ([] for _ in range(7))
    ks = jnp.zeros((depth, nbp, tp, KV_HEADS, ATT_HD), F32)
    vs = jnp.zeros((depth, nbp, tp, KV_HEADS, ATT_HD), F32)
    iks = jnp.zeros((depth, nbp, tp, IDX_DIM), F32)
    zero_ret = jnp.zeros((nbp, RET_HEADS, RET_DK, RET_DV), F32)
    zero_conv = jnp.zeros((nbp, CONV_W - 1, CONV_CH), F32)

    for l in range(depth):
        w_l = w_in[l]
        w_bf = jnp.concatenate([w_l[:, c2:], w_l[:, :c1]], axis=1).astype(BF16)
        w_f = jnp.pad(w_l[:, c1:c2], ((0, 0), (0, F32_WIDTH - F32_USED))).astype(BF16)
        p_bf = _matmul(xb, w_bf, BF16, tm_in, 3072)
        p_f = _matmul(xb, w_f, F32, tm_in, F32_WIDTH)

        k_new = p_f[:, KA_OFF:KA_OFF + kvw]
        v_new = p_f[:, VA_OFF:VA_OFF + kvw]
        ik_new = p_f[:, KI_OFF:KI_OFF + IDX_DIM]
        wi_t = p_f[:, WI_OFF:WI_OFF + IDX_HEADS].T
        ks = ks.at[l].set(k_new[:np_].reshape(nbp, tp, KV_HEADS, ATT_HD))
        vs = vs.at[l].set(v_new[:np_].reshape(nbp, tp, KV_HEADS, ATT_HD))
        iks = iks.at[l].set(ik_new[:np_].reshape(nbp, tp, IDX_DIM))
        kss.append(k_new[np_:].reshape(nbs, ts, KV_HEADS, ATT_HD))
        vss.append(v_new[np_:].reshape(nbs, ts, KV_HEADS, ATT_HD))
        ikss.append(ik_new[np_:].reshape(nbs, ts, IDX_DIM))

        nq = tp // tq_p
        hq = ATT_HEADS * ATT_HD
        qa_blk = (GATE_OFF + 3 * d) // hq
        att_p = _attention(
            p_bf, lambda tq: ((tq, hq), lambda b, j: (b * nq + j, qa_blk)),
            p_f, lambda lp: ((lp, kvw), lambda b, j: (b, KA_OFF // kvw)),
            p_f, lambda lp: ((lp, kvw), lambda b, j: (b, VA_OFF // kvw)),
            p_f, lambda tq: ((tq, IDX_HEADS * IDX_DIM), lambda b, j: (b * nq + j, QI_OFF // (IDX_HEADS * IDX_DIM))),
            p_f, lambda lp: ((lp, LANES), lambda b, j: (b, KI_OFF // LANES)),
            wi_t[:, :np_], lambda tq: ((IDX_HEADS, tq), lambda b, j: (0, b * nq + j)),
            nb=nbp, t=tp, l_pad=tp, l_true=tp, offset=0, tq=tq_p)

        mask_s = _sample_select(cache_idx_k[l], p_f, wi_t, np_, nbs, ts, past)
        att_s = _sample_attend(p_bf, qa_blk, cache_k[l].reshape(nbs, past, kvw), cache_v[l].reshape(nbs, past, kvw),
                               p_f, mask_s, np_, nbs, ts, past)

        ret_p, rs_p = _retention(p_bf, 0, nbp, tp, cl_p, 0, zero_ret, ret_gn_g[l])
        ret_s, rs_s = _retention(p_bf, np_, nbs, ts, ts, past, state_ret[l], ret_gn_g[l])
        rps.append(rs_p)
        rss.append(rs_s)

        cnv_p, cs_p = _conv_module(p_bf, 0, nbp, tp, tt_p, zero_conv, conv_dw[l], conv_dw_b[l], conv_ln_g[l],
                                   conv_ln_b[l])
        cnv_s, cs_s = _conv_module(p_bf, np_, nbs, ts, ts, state_conv[l], conv_dw[l], conv_dw_b[l], conv_ln_g[l],
                                   conv_ln_b[l])
        cps.append(cs_p)
        css.append(cs_s)

        z = _merge(att_p, ret_p, cnv_p, att_s, ret_s, cnv_s, p_bf, b_gate[l], w_att_o[l].astype(BF16),
                   w_ret_o[l].astype(BF16), w_conv_o[l].astype(BF16), math.gcd(tm, ns), 1024)
        tm_c = _pow2_tile(n, 256)
        x, xb, route = _outproj_ln(z, w_out[l].astype(BF16), x, ln1_g[l], ln1_b[l], w_router, b_router, alpha, tm_c)
        tile_expert, src_tiles, pos_tiles = _moe_layout(route, MOE_TILE, tm_c)
        y_sorted = _moe_experts(x, moe_w1[l], moe_w3[l], moe_w2[l], tile_expert, src_tiles)
        x, xb = _moe_combine_ln(y_sorted, pos_tiles, route, x, ln2_g[l], ln2_b[l], alpha, tm_c)

    y_prompt = x[:np_].reshape(nbp, tp, d)
    y_sample = x[np_:].reshape(nbs, ts, d)
    st = jnp.stack
    return (y_prompt, y_sample, ks, vs, iks, st(rps), st(cps), st(kss), st(vss), st(ikss), st(rss), st(css))
```

```python
import functools
import math

import jax
import jax.numpy as jnp
from jax import lax
from jax.experimental import pallas as pl
from jax.experimental.pallas import tpu as pltpu

F32 = jnp.float32
BF16 = jnp.bfloat16
I32 = jnp.int32

CHUNK = 64
TOPK_MAX = 256
ATT_HEADS, ATT_HD, KV_HEADS = 8, 128, 2
KV_GROUP = ATT_HEADS // KV_HEADS
IDX_HEADS, IDX_DIM = 8, 64
RET_HEADS, RET_DK, RET_DV = 8, 128, 256
ROPE_BASE = 10000.0
CONV_CH, CONV_W = 1024, 31
N_EXPERTS, N_GROUPS = 16, 4
EXP_PER_GROUP = N_EXPERTS // N_GROUPS
LN_EPS = 1e-5

LANES = 128
SUBLANES = 8

INT_MIN = -2 ** 31
NEG_BIG = -1e30

QR_OFF = 0
KR_OFF = QR_OFF + RET_HEADS * RET_DK
VR_OFF = KR_OFF + RET_HEADS * RET_DK
GR_OFF = VR_OFF + RET_HEADS * RET_DV
CIN_OFF = GR_OFF + RET_HEADS * RET_DV
GATE_OFF = CIN_OFF + 2 * CONV_CH
KA_OFF = 0
VA_OFF = KA_OFF + KV_HEADS * ATT_HD
QI_OFF = VA_OFF + KV_HEADS * ATT_HD
KI_OFF = QI_OFF + IDX_HEADS * IDX_DIM
WI_OFF = KI_OFF + IDX_DIM
F32_USED = WI_OFF + IDX_HEADS
F32_WIDTH = -(-F32_USED // LANES) * LANES

VMEM_LIMIT = 56 * 1024 * 1024


def _cp(*sem):
    return pltpu.CompilerParams(dimension_semantics=sem, vmem_limit_bytes=VMEM_LIMIT)


def _pow2_tile(n, pref):
    t = pref
    while n % t:
        t //= 2
    return t


def _layer_norm(x, g, b):
    mu = jnp.mean(x, axis=-1, keepdims=True)
    xc = x - mu
    var = jnp.mean(xc * xc, axis=-1, keepdims=True)
    return xc * lax.rsqrt(var + LN_EPS) * g + b


def _mm_kernel(x_ref, w_ref, o_ref):
    o_ref[...] = jnp.dot(x_ref[...], w_ref[...], preferred_element_type=F32).astype(o_ref.dtype)


def _matmul(x, w, out_dtype, tm, tn):
    m, k = x.shape
    n = w.shape[1]
    return pl.pallas_call(
        _mm_kernel,
        grid=(n // tn, m // tm),
        in_specs=[pl.BlockSpec((tm, k), lambda j, i: (i, 0)),
                  pl.BlockSpec((k, tn), lambda j, i: (0, j))],
        out_specs=pl.BlockSpec((tm, tn), lambda j, i: (i, j)),
        out_shape=jax.ShapeDtypeStruct((m, n), out_dtype),
        compiler_params=_cp("arbitrary", "arbitrary"),
        name="in_proj",
    )(x, w)


def _attn_kernel(q_ref, k_ref, v_ref, qi_ref, ki_ref, wi_ref, o_ref,
                 key_ref, lg_ref, mx_ref, acc_ref, m_ref,
                 *, tq, nq, ck, nck, offset, l_true, k_sel, tie_bits):
    j = pl.program_id(1)
    nsub = ck // LANES
    qlane = lax.broadcasted_iota(I32, (1, tq), 1)
    krow = lax.broadcasted_iota(I32, (ck, tq), 0)
    pos = offset + j * tq + qlane
    limit = jnp.minimum((pos // CHUNK + 1) * CHUNK, l_true)
    if nq == 1:
        n_act = (min(((offset + tq - 1) // CHUNK + 1) * CHUNK, l_true) + ck - 1) // ck
    else:
        last_pos = offset + (j + 1) * tq - 1
        max_limit = jnp.minimum((last_pos // CHUNK + 1) * CHUNK, l_true)
        n_act = (max_limit + ck - 1) // ck

    qi = (qi_ref[...] * (IDX_DIM ** -0.5)).astype(BF16)
    qis = [qi[:, h * IDX_DIM:(h + 1) * IDX_DIM] for h in range(IDX_HEADS)]
    stacked = tq % LANES == 0
    if stacked:
        qstack = jnp.concatenate(qis, axis=0)
    wt = wi_ref[...] * (IDX_HEADS ** -0.5)
    wrows = [wt[h:h + 1, :] for h in range(IDX_HEADS)]
    nt = (((1,), (1,)), ((), ()))

    def score_body(c, carry):
        for u in range(nsub):
            r0 = pl.multiple_of(c * ck + u * LANES, LANES)
            kiu = ki_ref[pl.ds(r0, LANES), :][:, :IDX_DIM].astype(BF16)
            if stacked:
                d_all = lax.dot_general(kiu, qstack, nt, preferred_element_type=F32)
                ds = [d_all[:, h * tq:(h + 1) * tq] for h in range(IDX_HEADS)]
            else:
                ds = [lax.dot_general(kiu, qis[h], nt, preferred_element_type=F32) for h in range(IDX_HEADS)]
            s = jnp.zeros((LANES, tq), F32)
            for h in range(IDX_HEADS):
                s = s + jnp.maximum(ds[h], 0.0) * wrows[h]
            bits = lax.bitcast_convert_type(s, I32)
            bits = jnp.where(bits == INT_MIN, 0, bits)
            key = jnp.where(bits >= 0, bits, bits ^ 0x7FFFFFFF)
            kpos = r0 + lax.broadcasted_iota(I32, (LANES, tq), 0)
            key_ref[c, u * LANES:(u + 1) * LANES, :] = jnp.where(kpos < limit, key, INT_MIN)
        return carry

    lax.fori_loop(0, n_act, score_body, 0)

    def count(pred):
        def body(c, acc):
            p = jnp.where(pred(key_ref[c], c * ck + krow), 1.0, 0.0)
            parts = [p[r * SUBLANES:(r + 1) * SUBLANES] for r in range(ck // SUBLANES)]
            while len(parts) > 1:
                parts = [a + b for a, b in zip(parts[0::2], parts[1::2])]
            return acc + parts[0]
        acc = lax.fori_loop(0, n_act, body, jnp.zeros((SUBLANES, tq), F32))
        return jnp.sum(acc, axis=0, keepdims=True)

    def bisect(i, tu):
        cand_u = tu | (jnp.int32(1) << (31 - i))
        cand_s = cand_u ^ INT_MIN
        cnt = count(lambda key, idx: key >= cand_s)
        return jnp.where(cnt >= k_sel, cand_u, tu)

    tu = lax.fori_loop(0, 32, bisect, jnp.zeros((1, tq), I32))
    ts = tu ^ INT_MIN
    c_gt = count(lambda key, idx: key > ts)
    c_ge = count(lambda key, idx: key >= ts)
    want = k_sel - c_gt

    eye = jnp.where(lax.broadcasted_iota(I32, (tq, tq), 0) == lax.broadcasted_iota(I32, (tq, tq), 1), 1.0, 0.0).astype(BF16)
    ts_adm = jnp.maximum(ts, INT_MIN + 1)
    m_ref[...] = jnp.full((1, tq), 2 ** 30, I32)

    @pl.when(jnp.max(c_ge) > k_sel)
    def _():
        def tie_bisect(i, m):
            cand = m | (jnp.int32(1) << (tie_bits - 1 - i))
            cnt = count(lambda key, idx: (key == ts) & (idx < cand))
            return jnp.where(cnt < want, cand, m)
        m_ref[...] = lax.fori_loop(0, tie_bits, tie_bisect, jnp.zeros((1, tq), I32))

    m_last = m_ref[...]

    qscale = (ATT_HD ** -0.5) * 1.4426950408889634
    groups = range(KV_HEADS)
    qss = []
    for g in groups:
        qs = jnp.concatenate(
            [q_ref[:, (g * KV_GROUP + r) * ATT_HD:(g * KV_GROUP + r + 1) * ATT_HD] for r in range(KV_GROUP)], axis=0)
        qss.append((qs.astype(F32) * qscale).astype(BF16))
    mx_ref[...] = jnp.full(mx_ref.shape, NEG_BIG, F32)

    def logit_body(c, carry):
        r0 = pl.multiple_of(c * ck, ck)
        key = key_ref[c]
        sel = (key > ts_adm) | ((key == ts_adm) & (c * ck + krow <= m_last))
        sel_t = jnp.where(sel, 1.0, 0.0).astype(BF16)
        sel_q = lax.dot_general(eye, sel_t, nt, preferred_element_type=F32)
        b = (sel_q - 1.0) * (-NEG_BIG)
        b4 = jnp.concatenate([b] * KV_GROUP, axis=0)
        for g in groups:
            kc = k_ref[pl.ds(r0, ck), g * ATT_HD:(g + 1) * ATT_HD].astype(BF16)
            lg = lax.dot_general(qss[g], kc, nt, preferred_element_type=F32) + b4
            lg_ref[g, c] = lg
            m = mx_ref[g]
            for u in range(nsub):
                m = jnp.maximum(m, lg[:, u * LANES:(u + 1) * LANES])
            mx_ref[g] = m
        return carry

    lax.fori_loop(0, n_act, logit_body, 0)
    m_rows = [jnp.max(mx_ref[g], axis=-1, keepdims=True) for g in groups]
    acc_ref[...] = jnp.zeros(acc_ref.shape, F32)
    ones_cols = jnp.ones((ck, ATT_HD), BF16)

    def pv_body(c, carry):
        r0 = pl.multiple_of(c * ck, ck)
        for g in groups:
            p = jnp.exp2(lg_ref[g, c] - m_rows[g]).astype(BF16)
            vc = v_ref[pl.ds(r0, ck), g * ATT_HD:(g + 1) * ATT_HD].astype(BF16)
            acc_ref[g] += jnp.dot(p, jnp.concatenate([vc, ones_cols], axis=1), preferred_element_type=F32)
        return carry

    lax.fori_loop(0, n_act, pv_body, 0)
    for g in groups:
        a = acc_ref[g]
        o = a[:, :ATT_HD] / a[:, ATT_HD:]
        for r in range(KV_GROUP):
            h = g * KV_GROUP + r
            o_ref[:, h * ATT_HD:(h + 1) * ATT_HD] = o[r * tq:(r + 1) * tq].astype(o_ref.dtype)


def _attention(q_arr, q_map, k_arr, k_map, v_arr, v_map, qi_arr, qi_map, ki_arr, ki_map, wi_arr, wi_map,
               *, nb, t, l_pad, l_true, offset, tq):
    ck = 2 * LANES
    nck = l_pad // ck
    assert l_pad % ck == 0 and t % tq == 0 and nck * (ck // LANES) < 256
    nq = t // tq
    k_sel = min(TOPK_MAX, l_true // 4)
    tie_bits = max(1, (l_pad - 1).bit_length())
    kern = functools.partial(_attn_kernel, tq=tq, nq=nq, ck=ck, nck=nck, offset=offset, l_true=l_true,
                             k_sel=k_sel, tie_bits=tie_bits)
    hq = ATT_HEADS * ATT_HD
    return pl.pallas_call(
        kern,
        grid=(nb, nq),
        in_specs=[pl.BlockSpec(*q_map(tq)), pl.BlockSpec(*k_map(l_pad)), pl.BlockSpec(*v_map(l_pad)),
                  pl.BlockSpec(*qi_map(tq)), pl.BlockSpec(*ki_map(l_pad)), pl.BlockSpec(*wi_map(tq))],
        out_specs=pl.BlockSpec((tq, hq), lambda b, j: (b * nq + j, 0)),
        out_shape=jax.ShapeDtypeStruct((nb * t, hq), BF16),
        scratch_shapes=[pltpu.VMEM((nck, ck, tq), I32),
                        pltpu.VMEM((KV_HEADS, nck, KV_GROUP * tq, ck), F32),
                        pltpu.VMEM((KV_HEADS, KV_GROUP * tq, LANES), F32),
                        pltpu.VMEM((KV_HEADS, KV_GROUP * tq, 2 * ATT_HD), F32),
                        pltpu.VMEM((1, tq), I32)],
        compiler_params=_cp("arbitrary", "arbitrary"),
        name="dsa_attention",
    )(q_arr, k_arr, v_arr, qi_arr, ki_arr, wi_arr)


def _order_key(s):
    bits = lax.bitcast_convert_type(s, I32)
    bits = jnp.where(bits == INT_MIN, 0, bits)
    return jnp.where(bits >= 0, bits, bits ^ 0x7FFFFFFF)


def _sample_select_kernel(kic_ref, new_ref, qi_ref, wi_ref, mask_ref, key_ref, keyn_ref, m_ref,
                          *, ts, gs, past, ck, k_sel, tie_bits):
    nck = past // ck
    nsub = ck // LANES
    l_true = past + ts
    lane = lax.broadcasted_iota(I32, (1, LANES), 1)
    lane_s = lane // ts
    pos = past + lane % ts
    limit = jnp.minimum((pos // CHUNK + 1) * CHUNK, l_true)
    nt = (((1,), (1,)), ((), ()))

    qi = (qi_ref[...] * (IDX_DIM ** -0.5)).astype(BF16)
    qstack = jnp.concatenate([qi[:, h * IDX_DIM:(h + 1) * IDX_DIM] for h in range(IDX_HEADS)], axis=0)
    wt = wi_ref[...] * (IDX_HEADS ** -0.5)
    wrows = [wt[h:h + 1, :] for h in range(IDX_HEADS)]

    def own_stream(d, rows):
        out = d[(gs - 1) * rows:gs * rows]
        for s in range(gs - 2, -1, -1):
            out = jnp.where(lane_s == s, d[s * rows:(s + 1) * rows], out)
        return out

    def scores(kmat, rows):
        d_all = lax.dot_general(kmat, qstack, nt, preferred_element_type=F32)
        s = jnp.zeros((rows, LANES), F32)
        for h in range(IDX_HEADS):
            s = s + jnp.maximum(own_stream(d_all[:, h * LANES:(h + 1) * LANES], rows), 0.0) * wrows[h]
        return _order_key(s)

    def score_body(c, carry):
        for u in range(nsub):
            r0 = pl.multiple_of(c * ck + u * LANES, LANES)
            kmat = jnp.concatenate([kic_ref[s, pl.ds(r0, LANES), :] for s in range(gs)], axis=0).astype(BF16)
            kpos = r0 + lax.broadcasted_iota(I32, (LANES, LANES), 0)
            key_ref[c, u * LANES:(u + 1) * LANES, :] = jnp.where(kpos < limit, scores(kmat, LANES), INT_MIN)
        return carry

    lax.fori_loop(0, nck, score_body, 0)
    kn = new_ref[:, :IDX_DIM].astype(BF16)
    npos = past + lax.broadcasted_iota(I32, (ts, LANES), 0)
    keyn_ref[...] = jnp.where(npos < limit, scores(kn, ts), INT_MIN)

    krow = lax.broadcasted_iota(I32, (ck, LANES), 0)

    def tree(p, rows):
        parts = [p[r * SUBLANES:(r + 1) * SUBLANES] for r in range(rows // SUBLANES)]
        while len(parts) > 1:
            parts = [a + b for a, b in zip(parts[0::2], parts[1::2])]
        return parts[0]

    def count(pred):
        def body(c, acc):
            p = jnp.where(pred(key_ref[c], c * ck + krow), 1.0, 0.0)
            return acc + tree(p, ck)
        acc = lax.fori_loop(0, nck, body, jnp.zeros((SUBLANES, LANES), F32))
        acc = acc + tree(jnp.where(pred(keyn_ref[...], npos), 1.0, 0.0), ts)
        return jnp.sum(acc, axis=0, keepdims=True)

    def bisect(i, tu):
        cand_u = tu | (jnp.int32(1) << (31 - i))
        cand_s = cand_u ^ INT_MIN
        cnt = count(lambda key, idx: key >= cand_s)
        return jnp.where(cnt >= k_sel, cand_u, tu)

    tu = lax.fori_loop(0, 32, bisect, jnp.zeros((1, LANES), I32))
    ts_ = tu ^ INT_MIN
    c_gt = count(lambda key, idx: key > ts_)
    c_ge = count(lambda key, idx: key >= ts_)
    want = k_sel - c_gt
    m_ref[...] = jnp.full((1, LANES), 2 ** 30, I32)

    @pl.when(jnp.max(c_ge) > k_sel)
    def _():
        def tie_bisect(i, m):
            cand = m | (jnp.int32(1) << (tie_bits - 1 - i))
            cnt = count(lambda key, idx: (key == ts_) & (idx < cand))
            return jnp.where(cnt < want, cand, m)
        m_ref[...] = lax.fori_loop(0, tie_bits, tie_bisect, jnp.zeros((1, LANES), I32))

    m_last = m_ref[...]
    ts_adm = jnp.maximum(ts_, INT_MIN + 1)
    eye = jnp.where(lax.broadcasted_iota(I32, (LANES, LANES), 0) == lax.broadcasted_iota(I32, (LANES, LANES), 1),
                    1.0, 0.0).astype(BF16)

    def selected(key, idx):
        sel = (key > ts_adm) | ((key == ts_adm) & (idx <= m_last))
        return jnp.where(sel, 1.0, 0.0).astype(BF16)

    for c in range(nck):
        sel_q = lax.dot_general(eye, selected(key_ref[c], c * ck + krow), nt, preferred_element_type=F32)
        mask_ref[:, c * ck:(c + 1) * ck] = sel_q.astype(BF16)
    sel_n = lax.dot_general(eye, selected(keyn_ref[...], npos), nt, preferred_element_type=F32)
    mask_ref[:, past:past + LANES] = jnp.concatenate(
        [sel_n, jnp.zeros((LANES, LANES - ts), F32)], axis=1).astype(BF16)


def _sample_select(cache_ik, p_f, wi_t, np_, nbs, ts, past):
    gs = LANES // ts
    ck = 2 * LANES
    assert LANES % ts == 0 and nbs % gs == 0 and past % ck == 0 and np_ % LANES == 0 and ts % SUBLANES == 0
    l_true = past + ts
    k_sel = min(TOPK_MAX, l_true // 4)
    kern = functools.partial(_sample_select_kernel, ts=ts, gs=gs, past=past, ck=ck, k_sel=k_sel,
                             tie_bits=max(1, (l_true - 1).bit_length()))
    rb = np_ // LANES
    return pl.pallas_call(
        kern,
        grid=(nbs // gs,),
        in_specs=[pl.BlockSpec((gs, past, IDX_DIM), lambda g: (g, 0, 0)),
                  pl.BlockSpec((LANES, LANES), lambda g: (rb + g, KI_OFF // LANES)),
                  pl.BlockSpec((LANES, IDX_HEADS * IDX_DIM), lambda g: (rb + g, QI_OFF // (IDX_HEADS * IDX_DIM))),
                  pl.BlockSpec((IDX_HEADS, LANES), lambda g: (0, rb + g))],
        out_specs=pl.BlockSpec((LANES, past + LANES), lambda g: (g, 0)),
        out_shape=jax.ShapeDtypeStruct((nbs * ts, past + LANES), BF16),
        scratch_shapes=[pltpu.VMEM((past // ck, ck, LANES), I32),
                        pltpu.VMEM((ts, LANES), I32),
                        pltpu.VMEM((1, LANES), I32)],
        compiler_params=_cp("arbitrary"),
        name="sample_select",
    )(cache_ik, p_f, p_f, wi_t)


def _sample_attend_kernel(q_ref, kc_ref, vc_ref, kn_ref, vn_ref, mask_ref, o_ref, lg_ref, *, ts, past, ca):
    nt = (((1,), (1,)), ((), ()))
    qscale = (ATT_HD ** -0.5) * 1.4426950408889634
    nca = past // ca
    rows = KV_GROUP * ts

    def bias_of(m):
        b = (m.astype(F32) - 1.0) * (-NEG_BIG)
        return jnp.concatenate([b] * KV_GROUP, axis=0)

    bias_n = bias_of(mask_ref[:, past:past + ts])
    for g in range(KV_HEADS):
        qs = jnp.concatenate(
            [q_ref[:, (g * KV_GROUP + r) * ATT_HD:(g * KV_GROUP + r + 1) * ATT_HD] for r in range(KV_GROUP)], axis=0)
        qs = (qs.astype(F32) * qscale).astype(BF16)
        cols = slice(g * ATT_HD, (g + 1) * ATT_HD)
        lg_n = lax.dot_general(qs, kn_ref[:, cols].astype(BF16), nt, preferred_element_type=F32) + bias_n
        m_run = jnp.full((rows, LANES), NEG_BIG, F32)
        for c in range(nca):
            kc = kc_ref[c * ca:(c + 1) * ca, cols].astype(BF16)
            lg = lax.dot_general(qs, kc, nt, preferred_element_type=F32) + bias_of(mask_ref[:, c * ca:(c + 1) * ca])
            lg_ref[g, :, c * ca:(c + 1) * ca] = lg
            for u in range(ca // LANES):
                m_run = jnp.maximum(m_run, lg[:, u * LANES:(u + 1) * LANES])
        m_row = jnp.maximum(jnp.max(m_run, axis=-1, keepdims=True), jnp.max(lg_n, axis=-1, keepdims=True))
        ones_n = jnp.ones((ts, ATT_HD), BF16)
        acc = jnp.dot(jnp.exp2(lg_n - m_row).astype(BF16),
                      jnp.concatenate([vn_ref[:, cols].astype(BF16), ones_n], axis=1), preferred_element_type=F32)
        ones_c = jnp.ones((ca, ATT_HD), BF16)
        for c in range(nca):
            p = jnp.exp2(lg_ref[g, :, c * ca:(c + 1) * ca] - m_row).astype(BF16)
            vc = vc_ref[c * ca:(c + 1) * ca, cols].astype(BF16)
            acc = acc + jnp.dot(p, jnp.concatenate([vc, ones_c], axis=1), preferred_element_type=F32)
        o = acc[:, :ATT_HD] / acc[:, ATT_HD:]
        for r in range(KV_GROUP):
            h = g * KV_GROUP + r
            o_ref[:, h * ATT_HD:(h + 1) * ATT_HD] = o[r * ts:(r + 1) * ts].astype(o_ref.dtype)


def _sample_attend(p_bf, qa_blk, cache_k, cache_v, p_f, mask, np_, nbs, ts, past):
    kvw = KV_HEADS * ATT_HD
    hq = ATT_HEADS * ATT_HD
    ca = 4 * LANES
    assert past % ca == 0
    rb = np_ // ts
    kern = functools.partial(_sample_attend_kernel, ts=ts, past=past, ca=ca)
    return pl.pallas_call(
        kern,
        grid=(nbs,),
        in_specs=[pl.BlockSpec((ts, hq), lambda b: (rb + b, qa_blk)),
                  pl.BlockSpec((None, past, kvw), lambda b: (b, 0, 0)),
                  pl.BlockSpec((None, past, kvw), lambda b: (b, 0, 0)),
                  pl.BlockSpec((ts, kvw), lambda b: (rb + b, KA_OFF // kvw)),
                  pl.BlockSpec((ts, kvw), lambda b: (rb + b, VA_OFF // kvw)),
                  pl.BlockSpec((ts, past + LANES), lambda b: (b, 0))],
        out_specs=pl.BlockSpec((ts, hq), lambda b: (b, 0)),
        out_shape=jax.ShapeDtypeStruct((nbs * ts, hq), BF16),
        scratch_shapes=[pltpu.VMEM((KV_HEADS, KV_GROUP * ts, past), F32)],
        compiler_params=_cp("arbitrary"),
        name="sample_attend",
    )(p_bf, cache_k, cache_v, p_f, p_f, mask)


def _ret_kernel(q_ref, k_ref, v_ref, g_ref, cs_ref, sn_ref, dec_ref, xi_ref, zt_ref, gp_ref, gn_ref, s0_ref,
                y_ref, sf_ref, s_ref):
    c = pl.program_id(1)
    dk, dv = RET_DK, RET_DV

    @pl.when(c == 0)
    def _():
        s_ref[...] = s0_ref[...]

    cs = cs_ref[...]
    sn = sn_ref[...]

    def rot(x):
        xf = x.astype(F32)
        return xf * cs + pltpu.roll(xf, dk // 2, 1) * sn

    for h in range(RET_HEADS):
        q = rot(q_ref[:, h * dk:(h + 1) * dk])
        k = rot(k_ref[:, h * dk:(h + 1) * dk]) * (dk ** -0.5)
        qb = q.astype(BF16)
        kb = k.astype(BF16)
        v = v_ref[:, h * dv:(h + 1) * dv]
        inner = lax.dot_general(qb, kb, (((1,), (1,)), ((), ())), preferred_element_type=F32) * dec_ref[h]
        s = s_ref[h]
        o = jnp.dot(inner.astype(BF16), v, preferred_element_type=F32)
        o = o + jnp.dot(qb, s.astype(BF16), preferred_element_type=F32) * xi_ref[h]
        kz = (k * zt_ref[h]).T.astype(BF16)
        s_new = s * gp_ref[h] + jnp.dot(kz, v, preferred_element_type=F32)
        s_ref[h] = s_new
        sf_ref[h] = s_new

        mu = jnp.mean(o, axis=-1, keepdims=True)
        oc = o - mu
        var = jnp.mean(oc * oc, axis=-1, keepdims=True)
        yn = oc * lax.rsqrt(var + LN_EPS) * gn_ref[:, h * dv:(h + 1) * dv]
        gg = g_ref[:, h * dv:(h + 1) * dv].astype(F32)
        y_ref[:, h * dv:(h + 1) * dv] = (gg * jax.nn.sigmoid(gg) * yn).astype(y_ref.dtype)


def _retention(p_bf, row0, nb, t, cl, offset, s0, gn_g):
    nc = t // cl
    h_, dk, dv = RET_HEADS, RET_DK, RET_DV
    half = dk // 2
    pos = (offset + jnp.arange(t, dtype=I32)).astype(F32)
    inv = 1.0 / (ROPE_BASE ** (jnp.arange(half, dtype=F32) / half))
    ang = pos[:, None] * inv[None, :]
    cos, sin = jnp.cos(ang), jnp.sin(ang)
    cs2 = jnp.concatenate([cos, cos], axis=-1)
    sn2 = jnp.concatenate([-sin, sin], axis=-1)
    log_g = jnp.log1p(-jnp.exp2(-5.0 - jnp.arange(h_, dtype=F32)))
    n = jnp.arange(cl, dtype=F32)
    diff = n[:, None] - n[None, :]
    decay = jnp.where(diff >= 0, jnp.exp(log_g[:, None, None] * jnp.maximum(diff, 0.0)), 0.0)
    xi = jnp.exp(log_g[:, None] * (n + 1.0))[..., None]
    zeta = jnp.exp(log_g[:, None] * (cl - 1.0 - n))[..., None]
    gpow = jnp.broadcast_to(jnp.exp(log_g * cl)[:, None, None], (h_, 1, dv))
    rb = row0 // cl
    return pl.pallas_call(
        _ret_kernel,
        grid=(nb, nc),
        in_specs=[pl.BlockSpec((cl, h_ * dk), lambda b, c: (rb + b * nc + c, QR_OFF // (h_ * dk))),
                  pl.BlockSpec((cl, h_ * dk), lambda b, c: (rb + b * nc + c, KR_OFF // (h_ * dk))),
                  pl.BlockSpec((cl, h_ * dv), lambda b, c: (rb + b * nc + c, VR_OFF // (h_ * dv))),
                  pl.BlockSpec((cl, h_ * dv), lambda b, c: (rb + b * nc + c, GR_OFF // (h_ * dv))),
                  pl.BlockSpec((cl, dk), lambda b, c: (c, 0)),
                  pl.BlockSpec((cl, dk), lambda b, c: (c, 0)),
                  pl.BlockSpec((h_, cl, cl), lambda b, c: (0, 0, 0)),
                  pl.BlockSpec((h_, cl, 1), lambda b, c: (0, 0, 0)),
                  pl.BlockSpec((h_, cl, 1), lambda b, c: (0, 0, 0)),
                  pl.BlockSpec((h_, 1, dv), lambda b, c: (0, 0, 0)),
                  pl.BlockSpec((1, h_ * dv), lambda b, c: (0, 0)),
                  pl.BlockSpec((None, h_, dk, dv), lambda b, c: (b, 0, 0, 0))],
        out_specs=[pl.BlockSpec((cl, h_ * dv), lambda b, c: (b * nc + c, 0)),
                   pl.BlockSpec((None, h_, dk, dv), lambda b, c: (b, 0, 0, 0))],
        out_shape=[jax.ShapeDtypeStruct((nb * t, h_ * dv), BF16),
                   jax.ShapeDtypeStruct((nb, h_, dk, dv), F32)],
        scratch_shapes=[pltpu.VMEM((h_, dk, dv), F32)],
        compiler_params=_cp("arbitrary", "arbitrary"),
        name="retention",
    )(p_bf, p_bf, p_bf, p_bf, cs2, sn2, decay, xi, zeta, gpow, gn_g.reshape(1, -1), s0)


CONV_HALO = 32
CONV_ROWS = 64
CONV_BLOCK = 64


def _conv_kernel(ca_ref, cb_ref, st_ref, w_ref, b_ref, lg_ref, lb_ref, y_ref, tail_ref, sh_ref, h_ref, *, tt):
    t = pl.program_id(1)
    ext = tt + CONV_HALO

    @pl.when(t == 0)
    def _():
        sh_ref[0, 0:CONV_HALO] = st_ref[...]

    @pl.when(t > 0)
    def _():
        sh_ref[0, 0:CONV_HALO] = sh_ref[0, tt:ext]

    ca = ca_ref[...].astype(F32)
    cb = cb_ref[...].astype(F32)
    sh_ref[0, CONV_HALO:ext] = ca * jax.nn.sigmoid(cb)
    tail_ref[...] = sh_ref[0, tt:ext]
    for s in range(1, SUBLANES):
        sh_ref[s, 0:ext - SUBLANES] = sh_ref[0, s:ext - SUBLANES + s]

    lead = CONV_HALO - (CONV_W - 1)
    blk = min(CONV_BLOCK, tt)
    nblk = blk // SUBLANES

    for lc in range(CONV_CH // LANES):
        cols = slice(lc * LANES, (lc + 1) * LANES)
        taps = [jnp.broadcast_to(w_ref[k:k + 1, cols], (SUBLANES, LANES)) for k in range(CONV_W)]
        bias = jnp.broadcast_to(b_ref[:, cols], (SUBLANES, LANES))

        def conv_body(i, carry):
            r0 = pl.multiple_of(i * blk, blk)
            accs = [bias] * nblk
            for k in range(CONV_W):
                a, s = divmod(k + lead, SUBLANES)
                u = sh_ref[s, pl.ds(r0 + a * SUBLANES, blk), cols]
                accs = [accs[j] + taps[k] * u[j * SUBLANES:(j + 1) * SUBLANES] for j in range(nblk)]
            h_ref[pl.ds(r0, blk), cols] = jnp.concatenate(accs, axis=0)
            return carry

        lax.fori_loop(0, tt // blk, conv_body, 0)

    rows = min(CONV_ROWS, tt)

    def norm_body(i, carry):
        r0 = pl.multiple_of(i * rows, rows)
        hn = _layer_norm(h_ref[pl.ds(r0, rows), :], lg_ref[...], lb_ref[...])
        y_ref[pl.ds(r0, rows), :] = (hn * jax.nn.sigmoid(hn)).astype(y_ref.dtype)
        return carry

    lax.fori_loop(0, tt // rows, norm_body, 0)


def _conv_module(p_bf, row0, nb, t, tt, state, dw, dw_b, ln_g, ln_b):
    nt = t // tt
    lead = CONV_HALO - (CONV_W - 1)
    st = jnp.pad(state, ((0, 0), (lead, 0), (0, 0)))
    rb = row0 // tt
    kern = functools.partial(_conv_kernel, tt=tt)
    y, tail = pl.pallas_call(
        kern,
        grid=(nb, nt),
        in_specs=[pl.BlockSpec((tt, CONV_CH), lambda b, i: (rb + b * nt + i, CIN_OFF // CONV_CH)),
                  pl.BlockSpec((tt, CONV_CH), lambda b, i: (rb + b * nt + i, CIN_OFF // CONV_CH + 1)),
                  pl.BlockSpec((None, CONV_HALO, CONV_CH), lambda b, i: (b, 0, 0)),
                  pl.BlockSpec((CONV_W, CONV_CH), lambda b, i: (0, 0)),
                  pl.BlockSpec((1, CONV_CH), lambda b, i: (0, 0)),
                  pl.BlockSpec((1, CONV_CH), lambda b, i: (0, 0)),
                  pl.BlockSpec((1, CONV_CH), lambda b, i: (0, 0))],
        out_specs=[pl.BlockSpec((tt, CONV_CH), lambda b, i: (b * nt + i, 0)),
                   pl.BlockSpec((None, CONV_HALO, CONV_CH), lambda b, i: (b, 0, 0))],
        out_shape=[jax.ShapeDtypeStruct((nb * t, CONV_CH), BF16),
                   jax.ShapeDtypeStruct((nb, CONV_HALO, CONV_CH), F32)],
        scratch_shapes=[pltpu.VMEM((SUBLANES, tt + CONV_HALO, CONV_CH), F32),
                        pltpu.VMEM((tt, CONV_CH), F32)],
        compiler_params=_cp("arbitrary", "arbitrary"),
        name="conv_module",
    )(p_bf, p_bf, st, dw, dw_b.reshape(1, -1), ln_g.reshape(1, -1), ln_b.reshape(1, -1))
    return y, tail[:, lead:, :]


def _merge_kernel(ap_ref, rp_ref, cp_ref, as_ref, rs_ref, cs_ref, ga_ref, gb_ref, gc_ref, ba_ref, bb_ref, bc_ref,
                  wa_ref, wr_ref, wc_ref, z_ref, *, prompt_tiles):
    ga = jax.nn.sigmoid(ga_ref[...].astype(F32) + ba_ref[...])
    gb = jax.nn.sigmoid(gb_ref[...].astype(F32) + bb_ref[...])
    gc = jax.nn.sigmoid(gc_ref[...].astype(F32) + bc_ref[...])

    is_prompt = pl.program_id(1) < prompt_tiles
    ya = jnp.dot(jnp.where(is_prompt, ap_ref[...], as_ref[...]), wa_ref[...], preferred_element_type=F32)
    yb = jnp.dot(jnp.where(is_prompt, rp_ref[...], rs_ref[...]), wr_ref[...], preferred_element_type=F32)
    yc = jnp.dot(jnp.where(is_prompt, cp_ref[...], cs_ref[...]), wc_ref[...], preferred_element_type=F32)
    z_ref[...] = (ga * ya + gb * yb + gc * yc).astype(z_ref.dtype)


def _merge(att_p, ret_p, cnv_p, att_s, ret_s, cnv_s, p_bf, b_gate, wa, wr, wc, tm, tn):
    np_, ns = att_p.shape[0], att_s.shape[0]
    assert np_ % tm == 0 and ns % tm == 0
    m = np_ + ns
    pt = np_ // tm
    d = wa.shape[1]
    nj = d // tn
    g0 = GATE_OFF // tn
    bg = b_gate.reshape(1, -1)

    def prompt_rows(j, i):
        return (jnp.minimum(i, pt - 1), 0)

    def sample_rows(j, i):
        return (jnp.maximum(i - pt, 0), 0)

    return pl.pallas_call(
        functools.partial(_merge_kernel, prompt_tiles=pt),
        grid=(nj, m // tm),
        in_specs=[pl.BlockSpec((tm, att_p.shape[1]), prompt_rows),
                  pl.BlockSpec((tm, ret_p.shape[1]), prompt_rows),
                  pl.BlockSpec((tm, cnv_p.shape[1]), prompt_rows),
                  pl.BlockSpec((tm, att_s.shape[1]), sample_rows),
                  pl.BlockSpec((tm, ret_s.shape[1]), sample_rows),
                  pl.BlockSpec((tm, cnv_s.shape[1]), sample_rows),
                  pl.BlockSpec((tm, tn), lambda j, i: (i, g0 + j)),
                  pl.BlockSpec((tm, tn), lambda j, i: (i, g0 + nj + j)),
                  pl.BlockSpec((tm, tn), lambda j, i: (i, g0 + 2 * nj + j)),
                  pl.BlockSpec((1, tn), lambda j, i: (0, j)),
                  pl.BlockSpec((1, tn), lambda j, i: (0, nj + j)),
                  pl.BlockSpec((1, tn), lambda j, i: (0, 2 * nj + j)),
                  pl.BlockSpec((wa.shape[0], tn), lambda j, i: (0, j)),
                  pl.BlockSpec((wr.shape[0], tn), lambda j, i: (0, j)),
                  pl.BlockSpec((wc.shape[0], tn), lambda j, i: (0, j))],
        out_specs=pl.BlockSpec((tm, tn), lambda j, i: (i, j)),
        out_shape=jax.ShapeDtypeStruct((m, d), BF16),
        compiler_params=_cp("arbitrary", "arbitrary"),
        name="branch_merge",
    )(att_p, ret_p, cnv_p, att_s, ret_s, cnv_s, p_bf, p_bf, p_bf, bg, bg, bg, wa, wr, wc)


def _route(x, w_hl, b):
    tm = x.shape[0]
    x_hi = x.astype(BF16)
    x_lo = (x - x_hi.astype(F32)).astype(BF16)
    d_hi = jnp.dot(x_hi, w_hl, preferred_element_type=F32)
    d_lo = jnp.dot(x_lo, w_hl[:, :LANES], preferred_element_type=F32)
    logits = d_hi[:, :LANES] + d_hi[:, LANES:] + d_lo + b
    lane = lax.broadcasted_iota(I32, (tm, LANES), 1)
    valid = lane < N_EXPERTS
    lm = jnp.where(valid, logits, NEG_BIG)
    e = jnp.where(valid, jnp.exp(lm - jnp.max(lm, axis=-1, keepdims=True)), 0.0)
    aff = e / jnp.sum(e, axis=-1, keepdims=True)

    def top2(vals):
        v1 = jnp.max(vals, axis=-1, keepdims=True)
        i1 = jnp.min(jnp.where(vals == v1, lane, LANES), axis=-1, keepdims=True)
        rest = jnp.where(lane == i1, -2.0, vals)
        v2 = jnp.max(rest, axis=-1, keepdims=True)
        i2 = jnp.min(jnp.where(rest == v2, lane, LANES), axis=-1, keepdims=True)
        return v1, i1, v2, i2

    grp = lane // EXP_PER_GROUP
    best = jnp.zeros((tm, 1), I32)
    best_score = None
    for g in range(N_GROUPS):
        v1, _, v2, _ = top2(jnp.where(grp == g, aff, -1.0))
        score = v1 + v2
        if g == 0:
            best_score = score
        else:
            better = score > best_score
            best = jnp.where(better, g, best)
            best_score = jnp.where(better, score, best_score)
    v1, i1, v2, i2 = top2(jnp.where(grp == best, aff, -1.0))
    tot = v1 + v2
    return (jnp.where(lane == 0, i1.astype(F32), 0.0) + jnp.where(lane == 1, i2.astype(F32), 0.0)
            + jnp.where(lane == 2, v1 / tot, 0.0) + jnp.where(lane == 3, v2 / tot, 0.0))


def _outproj_ln_kernel(z_ref, w_ref, x_ref, g_ref, b_ref, wr_ref, br_ref, of_ref, ob_ref, rt_ref, *, alpha):
    mix = jnp.dot(z_ref[...], w_ref[...], preferred_element_type=F32)
    y = _layer_norm(alpha * x_ref[...] + mix, g_ref[...], b_ref[...])
    of_ref[...] = y
    ob_ref[...] = y.astype(BF16)
    rt_ref[...] = _route(y, wr_ref[...], br_ref[...])


def _outproj_ln(z, w_out, x, g, b, w_router, b_router, alpha, tm):
    m, d = x.shape
    wr = jnp.pad(w_router, ((0, 0), (0, LANES - N_EXPERTS)))
    wr_hi = wr.astype(BF16)
    wr = jnp.concatenate([wr_hi, (wr - wr_hi.astype(F32)).astype(BF16)], axis=1)
    br = jnp.pad(b_router, (0, LANES - N_EXPERTS)).reshape(1, -1)
    kern = functools.partial(_outproj_ln_kernel, alpha=alpha)
    return pl.pallas_call(
        kern,
        grid=(m // tm,),
        in_specs=[pl.BlockSpec((tm, d), lambda i: (i, 0)),
                  pl.BlockSpec((d, d), lambda i: (0, 0)),
                  pl.BlockSpec((tm, d), lambda i: (i, 0)),
                  pl.BlockSpec((1, d), lambda i: (0, 0)),
                  pl.BlockSpec((1, d), lambda i: (0, 0)),
                  pl.BlockSpec((d, 2 * LANES), lambda i: (0, 0)),
                  pl.BlockSpec((1, LANES), lambda i: (0, 0))],
        out_specs=[pl.BlockSpec((tm, d), lambda i: (i, 0)), pl.BlockSpec((tm, d), lambda i: (i, 0)),
                   pl.BlockSpec((tm, LANES), lambda i: (i, 0))],
        out_shape=[jax.ShapeDtypeStruct((m, d), F32), jax.ShapeDtypeStruct((m, d), BF16),
                   jax.ShapeDtypeStruct((m, LANES), F32)],
        compiler_params=_cp("arbitrary"),
        name="out_proj_ln1_route",
    )(z, w_out, x, g.reshape(1, -1), b.reshape(1, -1), wr, br)


MOE_TILE = 256
MOE_ISSUE_GROUPS = 8


def _gather_rows(idx_ref, src_hbm, dst_ref, sem, lo, hi):
    for r in range(lo, hi):
        pltpu.make_async_copy(src_hbm.at[pl.ds(idx_ref[0, 0, r], 1)], dst_ref.at[pl.ds(r, 1)], sem).start()


def _wait_rows(src_hbm, dst_ref, sem, n):
    pltpu.make_async_copy(src_hbm.at[pl.ds(0, n)], dst_ref, sem).wait()


def _moe_expert_kernel(te_ref, cur_ref, nxt_ref, x_hbm, w1_ref, w3_ref, w2_ref, o_ref,
                       xg_ref, sem, w1b_ref, w3b_ref, w2b_ref):
    t = pl.program_id(0)
    nt = pl.num_programs(0)
    tile = xg_ref.shape[1]
    slot = t % 2

    @pl.when(t == 0)
    def _():
        _gather_rows(cur_ref, x_hbm, xg_ref.at[0], sem.at[0], 0, tile)

    @pl.when((t == 0) | (te_ref[t] != te_ref[jnp.maximum(t - 1, 0)]))
    def _():
        w1b_ref[...] = w1_ref[...].astype(BF16)
        w3b_ref[...] = w3_ref[...].astype(BF16)
        w2b_ref[...] = w2_ref[...].astype(BF16)

    _wait_rows(x_hbm, xg_ref.at[slot], sem.at[slot], tile)
    d = xg_ref.shape[2]
    kc, rc = d // MOE_ISSUE_GROUPS, tile // MOE_ISSUE_GROUPS
    h1 = h3 = None
    for c in range(MOE_ISSUE_GROUPS):
        xk = xg_ref[slot, :, c * kc:(c + 1) * kc].astype(BF16)
        p1 = jnp.dot(xk, w1b_ref[c * kc:(c + 1) * kc, :], preferred_element_type=F32)
        p3 = jnp.dot(xk, w3b_ref[c * kc:(c + 1) * kc, :], preferred_element_type=F32)
        h1 = p1 if h1 is None else h1 + p1
        h3 = p3 if h3 is None else h3 + p3
        _gather_rows(nxt_ref, x_hbm, xg_ref.at[1 - slot], sem.at[1 - slot], c * rc, (c + 1) * rc)
    h = h1 * jax.nn.sigmoid(h1) * h3
    o_ref[...] = jnp.dot(h.astype(BF16), w2b_ref[...], preferred_element_type=F32)

    @pl.when(t == nt - 1)
    def _():
        _wait_rows(x_hbm, xg_ref.at[1 - slot], sem.at[1 - slot], tile)


def _moe_experts(x, w1, w3, w2, tile_expert, src_tiles):
    n, d = x.shape
    _, _, de = w1.shape
    nt = src_tiles.shape[0] - 1
    tile = src_tiles.shape[2]
    grid_spec = pltpu.PrefetchScalarGridSpec(
        num_scalar_prefetch=1,
        grid=(nt,),
        in_specs=[pl.BlockSpec((1, 1, tile), lambda t, te: (t, 0, 0), memory_space=pltpu.SMEM),
                  pl.BlockSpec((1, 1, tile), lambda t, te: (t + 1, 0, 0), memory_space=pltpu.SMEM),
                  pl.BlockSpec(memory_space=pl.ANY),
                  pl.BlockSpec((None, d, de), lambda t, te: (te[t], 0, 0)),
                  pl.BlockSpec((None, d, de), lambda t, te: (te[t], 0, 0)),
                  pl.BlockSpec((None, de, d), lambda t, te: (te[t], 0, 0))],
        out_specs=pl.BlockSpec((tile, d), lambda t, te: (t, 0)),
        scratch_shapes=[pltpu.VMEM((2, tile, d), F32),
                        pltpu.SemaphoreType.DMA((2,)),
                        pltpu.VMEM((d, de), BF16),
                        pltpu.VMEM((d, de), BF16),
                        pltpu.VMEM((de, d), BF16)],
    )
    return pl.pallas_call(
        _moe_expert_kernel,
        grid_spec=grid_spec,
        out_shape=jax.ShapeDtypeStruct((nt * tile, d), F32),
        compiler_params=_cp("arbitrary"),
        name="moe_experts",
    )(tile_expert, src_tiles, src_tiles, x, w1, w3, w2)


def _moe_combine_kernel(cur_ref, nxt_ref, y_hbm, rt_ref, x_ref, g_ref, b_ref, of_ref, ob_ref, yg_ref, sem, *, alpha):
    t = pl.program_id(0)
    nt = pl.num_programs(0)
    tm = x_ref.shape[0]
    slot = t % 2

    @pl.when(t == 0)
    def _():
        _gather_rows(cur_ref, y_hbm, yg_ref.at[0], sem.at[0], 0, 2 * tm)

    _wait_rows(y_hbm, yg_ref.at[slot], sem.at[slot], 2 * tm)
    rg = tm // MOE_ISSUE_GROUPS
    for c in range(MOE_ISSUE_GROUPS):
        rows = slice(c * rg, (c + 1) * rg)
        rt = rt_ref[rows, :]
        moe = rt[:, 2:3] * yg_ref[slot, c * rg:(c + 1) * rg] + rt[:, 3:4] * yg_ref[slot, tm + c * rg:tm + (c + 1) * rg]
        y = _layer_norm(alpha * x_ref[rows, :] + moe, g_ref[...], b_ref[...])
        of_ref[rows, :] = y
        ob_ref[rows, :] = y.astype(BF16)
        _gather_rows(nxt_ref, y_hbm, yg_ref.at[1 - slot], sem.at[1 - slot], 2 * c * rg, 2 * (c + 1) * rg)

    @pl.when(t == nt - 1)
    def _():
        _wait_rows(y_hbm, yg_ref.at[1 - slot], sem.at[1 - slot], 2 * tm)


def _moe_combine_ln(y_sorted, pos_tiles, route, x, g, b, alpha, tm):
    m, d = x.shape
    kern = functools.partial(_moe_combine_kernel, alpha=alpha)
    return pl.pallas_call(
        kern,
        grid=(m // tm,),
        in_specs=[pl.BlockSpec((1, 1, 2 * tm), lambda t: (t, 0, 0), memory_space=pltpu.SMEM),
                  pl.BlockSpec((1, 1, 2 * tm), lambda t: (t + 1, 0, 0), memory_space=pltpu.SMEM),
                  pl.BlockSpec(memory_space=pl.ANY),
                  pl.BlockSpec((tm, LANES), lambda t: (t, 0)),
                  pl.BlockSpec((tm, d), lambda t: (t, 0)),
                  pl.BlockSpec((1, d), lambda t: (0, 0)),
                  pl.BlockSpec((1, d), lambda t: (0, 0))],
        out_specs=[pl.BlockSpec((tm, d), lambda t: (t, 0)), pl.BlockSpec((tm, d), lambda t: (t, 0))],
        out_shape=[jax.ShapeDtypeStruct((m, d), F32), jax.ShapeDtypeStruct((m, d), BF16)],
        scratch_shapes=[pltpu.VMEM((2, 2 * tm, d), F32), pltpu.SemaphoreType.DMA((2,))],
        compiler_params=_cp("arbitrary"),
        name="moe_combine_ln2",
    )(pos_tiles, pos_tiles, y_sorted, route, x, g.reshape(1, -1), b.reshape(1, -1))


def _moe_layout(route, tile, tm):
    n = route.shape[0]
    e_flat = route[:, 0:2].astype(I32).reshape(-1)
    npair = 2 * n
    nt = -(-npair // tile) + N_EXPERTS
    onehot = (e_flat[:, None] == jnp.arange(N_EXPERTS, dtype=I32)[None, :]).astype(I32)
    csum = jnp.cumsum(onehot, axis=0)
    counts = csum[-1]
    rank = jnp.sum((csum - onehot) * onehot, axis=1)
    padded = (counts + tile - 1) // tile * tile
    ends = jnp.cumsum(padded)
    starts = ends - padded
    pos = starts[e_flat] + rank
    rows = nt * tile
    src = jnp.zeros((rows,), I32).at[pos].set(jnp.arange(npair, dtype=I32) // 2, unique_indices=True)
    tile_expert = jnp.minimum(jnp.searchsorted(ends, jnp.arange(nt, dtype=I32) * tile, side='right'),
                              N_EXPERTS - 1).astype(I32)
    src_tiles = jnp.concatenate([src, jnp.zeros((tile,), I32)]).reshape(nt + 1, 1, tile)
    pos_t = pos.reshape(n // tm, tm, 2).transpose(0, 2, 1).reshape(n // tm, 1, 2 * tm)
    pos_tiles = jnp.concatenate([pos_t, jnp.zeros((1, 1, 2 * tm), I32)], axis=0)
    return tile_expert, src_tiles, pos_tiles


def kernel(x_prompt, x_sample, cache_k, cache_v, cache_idx_k, state_ret, state_conv, w_in, b_gate, w_att_o, ret_gn_g,
           w_ret_o, conv_dw, conv_dw_b, conv_ln_g, conv_ln_b, w_conv_o, w_out, ln1_g, ln1_b, w_router, b_router,
           moe_w1, moe_w3, moe_w2, ln2_g, ln2_b):
    nbp, tp, d = x_prompt.shape
    nbs, ts, _ = x_sample.shape
    depth = w_in.shape[0]
    past = cache_k.shape[2]
    np_, ns = nbp * tp, nbs * ts
    n = np_ + ns
    alpha = (2 * depth) ** 0.25
    kvw = KV_HEADS * ATT_HD

    tm = _pow2_tile(n, 512)
    tm_in = 3 * tm // 2 if n % (3 * tm // 2) == 0 else tm
    tq_p = _pow2_tile(tp, 256)
    cl_p = _pow2_tile(tp, 256)
    tt_p = _pow2_tile(tp, 256)
    ls_true = past + ts
    ls_pad = -(-ls_true // (2 * LANES)) * (2 * LANES)
    assert tp % (2 * LANES) == 0 and np_ % max(tq_p, cl_p, tt_p, ts) == 0

    x = jnp.concatenate([x_prompt.reshape(np_, d), x_sample.reshape(ns, d)], axis=0)
    xb = x.astype(BF16)

    c1 = ATT_HEADS * ATT_HD
    c2 = c1 + F32_USED
    rps, cps, kss, vss, ikss, rss, css = ([] for _ in range(7))
    ks = jnp.zeros((depth, nbp, tp, KV_HEADS, ATT_HD), F32)
    vs = jnp.zeros((depth, nbp, tp, KV_HEADS, ATT_HD), F32)
    iks = jnp.zeros((depth, nbp, tp, IDX_DIM), F32)
    zero_ret = jnp.zeros((nbp, RET_HEADS, RET_DK, RET_DV), F32)
    zero_conv = jnp.zeros((nbp, CONV_W - 1, CONV_CH), F32)

    for l in range(depth):
        w_l = w_in[l]
        w_bf = jnp.concatenate([w_l[:, c2:], w_l[:, :c1]], axis=1).astype(BF16)
        w_f = jnp.pad(w_l[:, c1:c2], ((0, 0), (0, F32_WIDTH - F32_USED))).astype(BF16)
        p_bf = _matmul(xb, w_bf, BF16, tm_in, 3072)
        p_f = _matmul(xb, w_f, F32, tm_in, F32_WIDTH)

        k_new = p_f[:, KA_OFF:KA_OFF + kvw]
        v_new = p_f[:, VA_OFF:VA_OFF + kvw]
        ik_new = p_f[:, KI_OFF:KI_OFF + IDX_DIM]
        wi_t = p_f[:, WI_OFF:WI_OFF + IDX_HEADS].T
        ks = ks.at[l].set(k_new[:np_].reshape(nbp, tp, KV_HEADS, ATT_HD))
        vs = vs.at[l].set(v_new[:np_].reshape(nbp, tp, KV_HEADS, ATT_HD))
        iks = iks.at[l].set(ik_new[:np_].reshape(nbp, tp, IDX_DIM))
        kss.append(k_new[np_:].reshape(nbs, ts, KV_HEADS, ATT_HD))
        vss.append(v_new[np_:].reshape(nbs, ts, KV_HEADS, ATT_HD))
        ikss.append(ik_new[np_:].reshape(nbs, ts, IDX_DIM))

        nq = tp // tq_p
        hq = ATT_HEADS * ATT_HD
        qa_blk = (GATE_OFF + 3 * d) // hq
        att_p = _attention(
            p_bf, lambda tq: ((tq, hq), lambda b, j: (b * nq + j, qa_blk)),
            p_f, lambda lp: ((lp, kvw), lambda b, j: (b, KA_OFF // kvw)),
            p_f, lambda lp: ((lp, kvw), lambda b, j: (b, VA_OFF // kvw)),
            p_f, lambda tq: ((tq, IDX_HEADS * IDX_DIM), lambda b, j: (b * nq + j, QI_OFF // (IDX_HEADS * IDX_DIM))),
            p_f, lambda lp: ((lp, LANES), lambda b, j: (b, KI_OFF // LANES)),
            wi_t[:, :np_], lambda tq: ((IDX_HEADS, tq), lambda b, j: (0, b * nq + j)),
            nb=nbp, t=tp, l_pad=tp, l_true=tp, offset=0, tq=tq_p)

        mask_s = _sample_select(cache_idx_k[l], p_f, wi_t, np_, nbs, ts, past)
        att_s = _sample_attend(p_bf, qa_blk, cache_k[l].reshape(nbs, past, kvw), cache_v[l].reshape(nbs, past, kvw),
                               p_f, mask_s, np_, nbs, ts, past)

        ret_p, rs_p = _retention(p_bf, 0, nbp, tp, cl_p, 0, zero_ret, ret_gn_g[l])
        ret_s, rs_s = _retention(p_bf, np_, nbs, ts, ts, past, state_ret[l], ret_gn_g[l])
        rps.append(rs_p)
        rss.append(rs_s)

        cnv_p, cs_p = _conv_module(p_bf, 0, nbp, tp, tt_p, zero_conv, conv_dw[l], conv_dw_b[l], conv_ln_g[l],
                                   conv_ln_b[l])
        cnv_s, cs_s = _conv_module(p_bf, np_, nbs, ts, ts, state_conv[l], conv_dw[l], conv_dw_b[l], conv_ln_g[l],
                                   conv_ln_b[l])
        cps.append(cs_p)
        css.append(cs_s)

        z = _merge(att_p, ret_p, cnv_p, att_s, ret_s, cnv_s, p_bf, b_gate[l], w_att_o[l].astype(BF16),
                   w_ret_o[l].astype(BF16), w_conv_o[l].astype(BF16), math.gcd(tm, ns), 1024)
        tm_c = _pow2_tile(n, 256)
        x, xb, route = _outproj_ln(z, w_out[l].astype(BF16), x, ln1_g[l], ln1_b[l], w_router, b_router, alpha, tm_c)
        tile_expert, src_tiles, pos_tiles = _moe_layout(route, MOE_TILE, tm_c)
        y_sorted = _moe_experts(x, moe_w1[l], moe_w3[l], moe_w2[l], tile_expert, src_tiles)
        x, xb = _moe_combine_ln(y_sorted, pos_tiles, route, x, ln2_g[l], ln2_b[l], alpha, tm_c)

    y_prompt = x[:np_].reshape(nbp, tp, d)
    y_sample = x[np_:].reshape(nbs, ts, d)
    st = jnp.stack
    return (y_prompt, y_sample, ks, vs, iks, st(rps), st(cps), st(kss), st(vss), st(ikss), st(rss), st(css))
```
